```python
import math
import jax, jax.numpy as jnp
from jax import lax
import numpy as np

D_MODEL = 2048
BATCH = 1
SEQ = 16384
DEPTH = 1

GRID_W = 64
CTX_LEN = 256
HEAD_DIM = 128
N_HEADS = 8
N_KV_HEADS = 2
GROUP = N_HEADS // N_KV_HEADS
WINDOW = 128
BLK = 128
Q_DIM = N_HEADS * HEAD_DIM
KV_DIM = N_KV_HEADS * HEAD_DIM
N_FGROUPS = 4
FG = 256
F_DIM = N_FGROUPS * FG
D_IN = Q_DIM + 2 * KV_DIM + F_DIM
D_MIX = Q_DIM + F_DIM
N_EXPERTS = 16
CAP_FACTOR = 2
EXPERT_FF = 1024
ROPE_THETA = 10000.0
RMS_EPS = 1e-6
NEG_INF = -1e30

kernel_name = "hybrid_swa_fnet_ec_moe_dit"


def _rmsnorm(x, g):
    xf = x.astype(jnp.float32)
    y = xf * lax.rsqrt(jnp.mean(xf * xf, axis=-1, keepdims=True) + RMS_EPS)
    return (y * g.astype(jnp.float32)).astype(x.dtype)


def _modulate(h, shift, scale):
    return h * (1 + scale) + shift


def _axial_rope_tables(n, dtype):
    rows = n // GRID_W
    row = jnp.repeat(jnp.arange(rows, dtype=jnp.float32), GRID_W)
    col = jnp.tile(jnp.arange(GRID_W, dtype=jnp.float32), rows)
    quarter = HEAD_DIM // 4
    freqs = ROPE_THETA ** (-jnp.arange(quarter, dtype=jnp.float32) / quarter)
    ar = row[:, None] * freqs[None, :]
    ac = col[:, None] * freqs[None, :]
    ang = jnp.concatenate([ar, ar, ac, ac], axis=-1)
    return jnp.cos(ang).astype(dtype), jnp.sin(ang).astype(dtype)


def _apply_rope(x, cos, sin):
    shp = x.shape
    xr = x.reshape(shp[:-1] + (2, 2, HEAD_DIM // 4))
    rot = jnp.stack([-xr[..., 1, :], xr[..., 0, :]], axis=-2).reshape(shp)
    return x * cos[None, :, None, :] + rot * sin[None, :, None, :]


def _softmax_with_sink(s, sink):
    s = s.astype(jnp.float32)
    sk = jnp.broadcast_to(sink.astype(jnp.float32), s.shape[:-1] + (1,))
    p = jax.nn.softmax(jnp.concatenate([s, sk], axis=-1), axis=-1)
    return p[..., :-1]


def _split_proj(p):
    q = p[..., :Q_DIM]
    k = p[..., Q_DIM:Q_DIM + KV_DIM]
    v = p[..., Q_DIM + KV_DIM:Q_DIM + 2 * KV_DIM]
    u = p[..., Q_DIM + 2 * KV_DIM:]
    b, n = p.shape[0], p.shape[1]
    return (q.reshape(b, n, N_HEADS, HEAD_DIM), k.reshape(b, n, N_KV_HEADS, HEAD_DIM),
            v.reshape(b, n, N_KV_HEADS, HEAD_DIM), u)


def _windowed_attention(q, k, v, k_ctx, v_ctx, sink):
    b, n = q.shape[0], q.shape[1]
    nb = n // BLK
    scale = 1.0 / math.sqrt(HEAD_DIM)
    qb = q.reshape(b, nb, BLK, N_KV_HEADS, GROUP, HEAD_DIM)

    def band(t):
        tb = jnp.pad(t.reshape(b, nb, BLK, N_KV_HEADS, HEAD_DIM),
                     ((0, 0), (1, 1), (0, 0), (0, 0), (0, 0)))
        return jnp.concatenate([tb[:, :-2], tb[:, 1:-1], tb[:, 2:]], axis=2)

    kw, vw = band(k), band(v)
    blk = jnp.arange(nb)[:, None, None]
    qpos = blk * BLK + jnp.arange(BLK)[None, :, None]
    kpos = blk * BLK - BLK + jnp.arange(3 * BLK)[None, None, :]
    valid = (jnp.abs(qpos - kpos) <= WINDOW) & (kpos >= 0) & (kpos < n)

    s_loc = jnp.einsum('bnqhgd,bnkhd->bhgnqk', qb, kw).astype(jnp.float32) * scale
    s_loc = jnp.where(valid, s_loc, NEG_INF)
    s_ctx = jnp.einsum('bnqhgd,bmhd->bhgnqm', qb, k_ctx).astype(jnp.float32) * scale
    sink_b = sink.reshape(N_KV_HEADS, GROUP)[None, :, :, None, None, None]
    p = _softmax_with_sink(jnp.concatenate([s_loc, s_ctx], axis=-1), sink_b).astype(v.dtype)
    p_loc, p_ctx = p[..., :3 * BLK], p[..., 3 * BLK:]
    o = (jnp.einsum('bhgnqk,bnkhd->bnqhgd', p_loc, vw)
         + jnp.einsum('bhgnqm,bmhd->bnqhgd', p_ctx, v_ctx))
    return o.reshape(b, n, Q_DIM)


def _context_attention(q, k, v, sink):
    b, m = q.shape[0], q.shape[1]
    scale = 1.0 / math.sqrt(HEAD_DIM)
    qg = q.reshape(b, m, N_KV_HEADS, GROUP, HEAD_DIM)
    s = jnp.einsum('bqhgd,bkhd->bhgqk', qg, k).astype(jnp.float32) * scale
    sink_b = sink.reshape(N_KV_HEADS, GROUP)[None, :, :, None, None]
    p = _softmax_with_sink(s, sink_b).astype(v.dtype)
    return jnp.einsum('bhgqk,bkhd->bqhgd', p, v).reshape(b, m, Q_DIM)


def _fourier_mix(u, w_f):
    b, n = u.shape[0], u.shape[1]
    ug = u.reshape(b, n, N_FGROUPS, FG).astype(jnp.float32)
    y = jnp.fft.fft2(ug, axes=(1, 3), norm='ortho').real.astype(u.dtype)
    return jnp.einsum('bngc,gcd->bngd', y, w_f).reshape(b, n, F_DIM)


def _mixer(h_x, h_c, w_in, w_out, w_f, sink, with_ctx_out):
    qx, kx, vx, ux = _split_proj(h_x @ w_in)
    qc, kc, vc, uc = _split_proj(h_c @ w_in)
    cos, sin = _axial_rope_tables(h_x.shape[1], h_x.dtype)
    qx, kx = _apply_rope(qx, cos, sin), _apply_rope(kx, cos, sin)
    ax = _windowed_attention(qx, kx, vx, kc, vc, sink)
    fx = _fourier_mix(ux, w_f)
    out_x = jnp.concatenate([ax, fx], axis=-1) @ w_out
    if not with_ctx_out:
        return out_x, None
    ac = _context_attention(qc, kc, vc, sink)
    fc = _fourier_mix(uc, w_f)
    out_c = jnp.concatenate([ac, fc], axis=-1) @ w_out
    return out_x, out_c


def _expert_choice_ffn(h, w_router, w_gate, w_up, w_down):
    b, n, d = h.shape
    cap = max(1, CAP_FACTOR * n // N_EXPERTS)
    logits = jnp.einsum('bnd,de->ben', h, w_router).astype(jnp.float32)
    aff = jax.nn.softmax(logits, axis=1)
    gates, idx = lax.top_k(aff, cap)
    xs = jax.vmap(lambda hb, ib: hb[ib])(h, idx)
    g = jnp.einsum('becd,edf->becf', xs, w_gate)
    u = jnp.einsum('becd,edf->becf', xs, w_up)
    y = jnp.einsum('becf,efd->becd', jax.nn.silu(g) * u, w_down)
    y = y * gates[..., None].astype(y.dtype)

    def combine(ib, yb):
        return jnp.zeros((n, d), yb.dtype).at[ib.reshape(-1)].add(yb.reshape(-1, d))

    return jax.vmap(combine)(idx, y)


def setup_inputs(seed: int = 0) -> dict:
    key = jax.random.key(seed)
    ks = jax.random.split(key, 20)
    f32 = jnp.float32
    L, D = DEPTH, D_MODEL
    nrm = lambda k, shp, s: jax.random.normal(k, shp, f32) * s
    return {
        'x': nrm(ks[0], (BATCH, SEQ, D), 1.0),
        'c': nrm(ks[1], (BATCH, D), 1.0),
        'ctx': nrm(ks[2], (BATCH, CTX_LEN, D), 1.0),
        'c_ctx': nrm(ks[3], (D,), 1.0),
        'w_mod': nrm(ks[4], (L, D, 6 * D), 0.5 * D ** -0.5),
        'b_mod': nrm(ks[5], (L, 6 * D), 0.02),
        'norm_mix': 1.0 + nrm(ks[6], (L, D), 0.05),
        'w_in': nrm(ks[7], (L, D, D_IN), D ** -0.5),
        'sink': nrm(ks[8], (L, N_HEADS), 0.5),
        'w_fourier': nrm(ks[9], (L, N_FGROUPS, FG, FG), FG ** -0.5),
        'w_out': nrm(ks[10], (L, D_MIX, D), D_MIX ** -0.5),
        'norm_ffn': 1.0 + nrm(ks[11], (L, D), 0.05),
        'w_router': nrm(ks[12], (L, D, N_EXPERTS), D ** -0.5),
        'w_gate': nrm(ks[13], (L, N_EXPERTS, D, EXPERT_FF), D ** -0.5),
        'w_up': nrm(ks[14], (L, N_EXPERTS, D, EXPERT_FF), D ** -0.5),
        'w_down': nrm(ks[15], (L, N_EXPERTS, EXPERT_FF, D), EXPERT_FF ** -0.5),
        'norm_final': 1.0 + nrm(ks[16], (D,), 0.05),
    }


def reference(x, c, ctx, c_ctx, w_mod, b_mod, norm_mix, w_in, sink, w_fourier, w_out,
              norm_ffn, w_router, w_gate, w_up, w_down, norm_final):
    b = x.shape[0]
    for l in range(DEPTH):
        last = l == DEPTH - 1
        mod_x = (jax.nn.silu(c) @ w_mod[l] + b_mod[l]).reshape(b, 6, 1, D_MODEL)
        mod_c = (jax.nn.silu(c_ctx) @ w_mod[l] + b_mod[l]).reshape(6, D_MODEL)

        h_x = _modulate(_rmsnorm(x, norm_mix[l]), mod_x[:, 0], mod_x[:, 1])
        h_c = _modulate(_rmsnorm(ctx, norm_mix[l]), mod_c[0], mod_c[1])
        out_x, out_c = _mixer(h_x, h_c, w_in[l], w_out[l], w_fourier[l], sink[l], not last)
        x = x + mod_x[:, 2] * out_x

        g_x = _modulate(_rmsnorm(x, norm_ffn[l]), mod_x[:, 3], mod_x[:, 4])
        x = x + mod_x[:, 5] * _expert_choice_ffn(g_x, w_router[l], w_gate[l], w_up[l], w_down[l])

        if not last:
            ctx = ctx + mod_c[2] * out_c
            g_c = _modulate(_rmsnorm(ctx, norm_ffn[l]), mod_c[3], mod_c[4])
            ctx = ctx + mod_c[5] * _expert_choice_ffn(g_c, w_router[l], w_gate[l], w_up[l], w_down[l])
    return _rmsnorm(x, norm_final)
```

```python
import functools
import math

import jax
import jax.numpy as jnp
from jax import lax
from jax.experimental import pallas as pl
from jax.experimental.pallas import tpu as pltpu

F32 = jnp.float32
BF16 = jnp.bfloat16
I32 = jnp.int32

HEAD_DIM = 128
N_HEADS = 8
N_KV_HEADS = 2
GROUP = N_HEADS // N_KV_HEADS
WINDOW = 128
Q_DIM = N_HEADS * HEAD_DIM
KV_DIM = N_KV_HEADS * HEAD_DIM
N_FGROUPS = 4
FG = 256
F_DIM = N_FGROUPS * FG
D_IN = Q_DIM + 2 * KV_DIM + F_DIM
N_EXPERTS = 16
CAP_FACTOR = 2
GRID_W = 64
ROPE_THETA = 10000.0
RMS_EPS = 1e-6
NEG_INF = -1e30

LANES = 128
SUBLANES_F32 = 8
VMEM_LIMIT_BYTES = 56 * 1024 * 1024

TOK_TILE = 2 * LANES
PIECE = 48
ROW_ALIGN = 16
WINDOW_ROWS = ROW_ALIGN + PIECE
GATE_LANES = LANES
DFT_NB = LANES
OUTPROJ_CHUNKS = 2
PROJ_CHUNKS = 2
NOT_SELECTED = -(1 << 20)


def _cparams(*sem):
    return pltpu.CompilerParams(dimension_semantics=sem, vmem_limit_bytes=VMEM_LIMIT_BYTES)


def _rms_mod(x, nw, shift, scale):
    ms = jnp.mean(x * x, axis=-1, keepdims=True)
    h = x * lax.rsqrt(ms + RMS_EPS) * nw
    return h * (1.0 + scale) + shift


def _mod_kernel(ct_ref, w_ref, b_ref, o_ref):
    ct = ct_ref[...]
    s = ct * jax.nn.sigmoid(ct)
    w = w_ref[...]
    r0 = jnp.sum(s[:, 0:1] * w, axis=0, keepdims=True)
    r1 = jnp.sum(s[:, 1:2] * w, axis=0, keepdims=True)
    o_ref[...] = jnp.concatenate([r0, r1], axis=0) + b_ref[...]


def _modulation(ct, w_mod, b_mod):
    d, n = w_mod.shape
    tn = math.gcd(n, 1024)
    return pl.pallas_call(
        _mod_kernel,
        out_shape=jax.ShapeDtypeStruct((2, n), F32),
        grid=(n // tn,),
        in_specs=[pl.BlockSpec((d, 2), lambda i: (0, 0)),
                  pl.BlockSpec((d, tn), lambda i: (0, i)),
                  pl.BlockSpec((1, tn), lambda i: (0, i))],
        out_specs=pl.BlockSpec((2, tn), lambda i: (0, i)),
        compiler_params=_cparams("arbitrary"),
        name="modulation",
    )(ct, w_mod, b_mod)


def _ab_kernel(cf_ref, sf_ref, wf_ref, ab_ref):
    wf = wf_ref[0]
    a = jnp.dot(cf_ref[...], wf, preferred_element_type=F32, precision=lax.Precision.HIGHEST)
    b = jnp.dot(sf_ref[...], wf, preferred_element_type=F32, precision=lax.Precision.HIGHEST)
    ab_ref[0] = jnp.concatenate([a, b], axis=1).astype(BF16)


def _fold_channel_dft(w_fourier):
    g, fg, _ = w_fourier.shape
    idx = jnp.arange(fg, dtype=I32)
    th = ((idx[:, None] * idx[None, :]) % fg).astype(F32) * (2.0 * math.pi / fg)
    scale = 1.0 / math.sqrt(fg)
    cf = jnp.cos(th) * scale
    sf = jnp.sin(th) * scale
    return pl.pallas_call(
        _ab_kernel,
        out_shape=jax.ShapeDtypeStruct((g, fg, 2 * fg), BF16),
        grid=(g,),
        in_specs=[pl.BlockSpec((fg, fg), lambda i: (0, 0)),
                  pl.BlockSpec((fg, fg), lambda i: (0, 0)),
                  pl.BlockSpec((1, fg, fg), lambda i: (i, 0, 0))],
        out_specs=pl.BlockSpec((1, fg, 2 * fg), lambda i: (i, 0, 0)),
        compiler_params=_cparams("arbitrary"),
        name="fold_channel_dft",
    )(cf, sf, w_fourier)


def _rope_tables(n):
    rows = n // GRID_W
    row = jnp.repeat(jnp.arange(rows, dtype=F32), GRID_W)
    col = jnp.tile(jnp.arange(GRID_W, dtype=F32), rows)
    quarter = HEAD_DIM // 4
    freqs = ROPE_THETA ** (-jnp.arange(quarter, dtype=F32) / quarter)
    ar = row[:, None] * freqs[None, :]
    ac = col[:, None] * freqs[None, :]
    ang = jnp.concatenate([ar, ar, ac, ac], axis=-1)
    cos, sin = jnp.cos(ang), jnp.sin(ang)
    first_half = (jnp.arange(HEAD_DIM) % (2 * quarter)) < quarter
    sin_up = jnp.where(first_half[None, :], -sin, 0.0)
    sin_dn = jnp.where(first_half[None, :], 0.0, sin)
    return cos, sin_up, sin_dn


def _store_dft_rows(z_ref, plane, val, a0):
    groups = z_ref.shape[1]
    nb = groups * SUBLANES_F32
    for al in range(val.shape[0] // nb):
        z_ref[plane, :, (a0 + al) * SUBLANES_F32:(a0 + al + 1) * SUBLANES_F32, :] = (
            val[al * nb:(al + 1) * nb].reshape(groups, SUBLANES_F32, LANES))


def _proj_kernel(x_ref, nw_ref, sh_ref, sc_ref, w_ref, ab_ref, cos_ref, su_ref, sd_ref,
                 q_ref, k_ref, v_ref, z_ref):
    quarter = HEAD_DIM // 4
    qscale = 1.0 / math.sqrt(HEAD_DIM)
    tm = x_ref.shape[0]
    chunk = tm // PROJ_CHUNKS
    for r0 in range(0, tm, chunk):
        rs = slice(r0, r0 + chunk)
        hb = _rms_mod(x_ref[rs, :], nw_ref[...], sh_ref[...], sc_ref[...]).astype(BF16)
        cos, s_up, s_dn = cos_ref[rs, :], su_ref[rs, :], sd_ref[rs, :]

        def rope(t):
            return (t * cos + pltpu.roll(t, HEAD_DIM - quarter, 1) * s_up
                    + pltpu.roll(t, quarter, 1) * s_dn)

        q = jnp.dot(hb, w_ref[:, :Q_DIM], preferred_element_type=F32)
        for j in range(N_HEADS):
            sl = slice(j * HEAD_DIM, (j + 1) * HEAD_DIM)
            q_ref[rs, sl] = (rope(q[:, sl]) * qscale).astype(BF16)
        k = jnp.dot(hb, w_ref[:, Q_DIM:Q_DIM + KV_DIM], preferred_element_type=F32)
        for j in range(N_KV_HEADS):
            sl = slice(j * HEAD_DIM, (j + 1) * HEAD_DIM)
            k_ref[rs, sl] = rope(k[:, sl]).astype(BF16)
        v = jnp.dot(hb, w_ref[:, Q_DIM + KV_DIM:Q_DIM + 2 * KV_DIM], preferred_element_type=F32)
        v_ref[rs, :] = v.astype(BF16)
        u = jnp.dot(hb, w_ref[:, Q_DIM + 2 * KV_DIM:], preferred_element_type=F32).astype(BF16)
        for g in range(N_FGROUPS):
            pq = jnp.dot(u[:, g * FG:(g + 1) * FG], ab_ref[g], preferred_element_type=F32)
            per_group = FG // LANES
            for t in range(2 * per_group):
                plane = (t // per_group) * (F_DIM // LANES) + g * per_group + t % per_group
                _store_dft_rows(z_ref, plane, pq[:, t * LANES:(t + 1) * LANES], r0 // DFT_NB)


def _project_latent(x, nw, shift, scale, w_in_b, ab, tm=512):
    n, d = x.shape
    cos, s_up, s_dn = _rope_tables(n)
    row = lambda i: (i, 0)
    const2 = lambda i: (0, 0)
    return pl.pallas_call(
        _proj_kernel,
        out_shape=(jax.ShapeDtypeStruct((n, Q_DIM), BF16),
                   jax.ShapeDtypeStruct((n, KV_DIM), BF16),
                   jax.ShapeDtypeStruct((n, KV_DIM), BF16),
                   jax.ShapeDtypeStruct((2 * F_DIM // LANES, DFT_NB // SUBLANES_F32,
                                         (n // DFT_NB) * SUBLANES_F32, LANES), F32)),
        grid=(n // tm,),
        in_specs=[pl.BlockSpec((tm, d), row),
                  pl.BlockSpec((1, d), const2), pl.BlockSpec((1, d), const2),
                  pl.BlockSpec((1, d), const2),
                  pl.BlockSpec((d, D_IN), const2, pipeline_mode=pl.Buffered(1)),
                  pl.BlockSpec((N_FGROUPS, FG, 2 * FG), lambda i: (0, 0, 0),
                               pipeline_mode=pl.Buffered(1)),
                  pl.BlockSpec((tm, HEAD_DIM), row), pl.BlockSpec((tm, HEAD_DIM), row),
                  pl.BlockSpec((tm, HEAD_DIM), row)],
        out_specs=(pl.BlockSpec((tm, Q_DIM), row), pl.BlockSpec((tm, KV_DIM), row),
                   pl.BlockSpec((tm, KV_DIM), row),
                   pl.BlockSpec((2 * F_DIM // LANES, DFT_NB // SUBLANES_F32,
                                 (tm // DFT_NB) * SUBLANES_F32, LANES), lambda i: (0, 0, i, 0))),
        compiler_params=_cparams("arbitrary"),
        name="project_latent",
    )(x, nw, shift, scale, w_in_b, ab, cos, s_up, s_dn)


def _ctx_kernel(x_ref, nw_ref, sh_ref, sc_ref, w_ref, k_ref, v_ref):
    hb = _rms_mod(x_ref[...], nw_ref[...], sh_ref[...], sc_ref[...]).astype(BF16)
    kv = jnp.dot(hb, w_ref[...], preferred_element_type=F32)
    k_ref[...] = kv[:, :KV_DIM].astype(BF16)
    v_ref[...] = kv[:, KV_DIM:].astype(BF16)


def _project_context(ctx, nw, shift, scale, w_kv_b):
    m, d = ctx.shape
    full = lambda shp: pl.BlockSpec(shp, lambda i: (0, 0))
    return pl.pallas_call(
        _ctx_kernel,
        out_shape=(jax.ShapeDtypeStruct((m, KV_DIM), BF16),
                   jax.ShapeDtypeStruct((m, KV_DIM), BF16)),
        grid=(1,),
        in_specs=[full((m, d)), full((1, d)), full((1, d)), full((1, d)), full((d, 2 * KV_DIM))],
        out_specs=(full((m, KV_DIM)), full((m, KV_DIM))),
        compiler_params=_cparams("arbitrary"),
        name="project_context",
    )(ctx, nw, shift, scale, w_kv_b)


def _attn_kernel(sink_ref, q_ref, kp_ref, km_ref, kn_ref, vp_ref, vm_ref, vn_ref,
                 kc_ref, vc_ref, o_ref, *, n_total, tq):
    i = pl.program_id(0)
    nsub = tq // WINDOW
    kwin = jnp.concatenate([kp_ref[...], km_ref[...], kn_ref[...]], axis=0)
    vwin = jnp.concatenate([vp_ref[...], vm_ref[...], vn_ref[...]], axis=0)
    rows = GROUP * WINDOW
    span = 3 * WINDOW
    r = lax.broadcasted_iota(I32, (rows, span), 0) % WINDOW
    c = lax.broadcasted_iota(I32, (rows, span), 1)
    head_of_row = lax.broadcasted_iota(I32, (rows, 1), 0) // WINDOW
    ones_loc = jnp.ones((span, HEAD_DIM), BF16)
    ones_ctx = jnp.ones((kc_ref.shape[0], HEAD_DIM), BF16)
    nt = (((1,), (1,)), ((), ()))
    for b in range(nsub):
        base = (i * nsub + b - 1) * WINDOW
        lo = jnp.maximum(r, -base)
        hi = jnp.minimum(r + 2 * WINDOW, n_total - 1 - base)
        valid = (c >= lo) & (c <= hi)
        for h in range(N_KV_HEADS):
            hs = slice(h * HEAD_DIM, (h + 1) * HEAD_DIM)
            qs = jnp.concatenate(
                [q_ref[b * WINDOW:(b + 1) * WINDOW,
                       (h * GROUP + g) * HEAD_DIM:(h * GROUP + g + 1) * HEAD_DIM]
                 for g in range(GROUP)], axis=0)
            kw = kwin[b * WINDOW:b * WINDOW + span, hs]
            vw = vwin[b * WINDOW:b * WINDOW + span, hs]
            s_loc = lax.dot_general(qs, kw, nt, preferred_element_type=F32)
            s_loc = jnp.where(valid, s_loc, NEG_INF)
            s_ctx = lax.dot_general(qs, kc_ref[:, hs], nt, preferred_element_type=F32)
            sink_col = jnp.zeros((rows, 1), F32)
            for g in range(GROUP):
                sink_col = jnp.where(head_of_row == g, sink_ref[h * GROUP + g], sink_col)
            m = jnp.maximum(jnp.maximum(jnp.max(s_loc, axis=1, keepdims=True),
                                        jnp.max(s_ctx, axis=1, keepdims=True)), sink_col)
            p_loc = jnp.exp(s_loc - m).astype(BF16)
            p_ctx = jnp.exp(s_ctx - m).astype(BF16)
            ov = (jnp.dot(p_loc, jnp.concatenate([vw, ones_loc], axis=1),
                          preferred_element_type=F32)
                  + jnp.dot(p_ctx, jnp.concatenate([vc_ref[:, hs], ones_ctx], axis=1),
                            preferred_element_type=F32))
            o = ov[:, :HEAD_DIM] / (ov[:, HEAD_DIM:] + jnp.exp(sink_col - m))
            for g in range(GROUP):
                o_ref[b * WINDOW:(b + 1) * WINDOW,
                      (h * GROUP + g) * HEAD_DIM:(h * GROUP + g + 1) * HEAD_DIM] = (
                    o[g * WINDOW:(g + 1) * WINDOW].astype(BF16))


def _attention(q, k, v, kc, vc, sink, tq=512):
    n = q.shape[0]
    m = kc.shape[0]
    nsub = tq // WINDOW
    nblk = n // WINDOW
    prev = lambda i, s: (jnp.maximum(i * nsub - 1, 0), 0)
    main = lambda i, s: (i, 0)
    nxt = lambda i, s: (jnp.minimum(i * nsub + nsub, nblk - 1), 0)
    const = lambda i, s: (0, 0)
    kv_specs = [pl.BlockSpec((WINDOW, KV_DIM), prev), pl.BlockSpec((tq, KV_DIM), main),
                pl.BlockSpec((WINDOW, KV_DIM), nxt)]
    return pl.pallas_call(
        functools.partial(_attn_kernel, n_total=n, tq=tq),
        out_shape=jax.ShapeDtypeStruct((n, Q_DIM), BF16),
        grid_spec=pltpu.PrefetchScalarGridSpec(
            num_scalar_prefetch=1,
            grid=(n // tq,),
            in_specs=[pl.BlockSpec((tq, Q_DIM), main)] + kv_specs + kv_specs
                     + [pl.BlockSpec((m, KV_DIM), const), pl.BlockSpec((m, KV_DIM), const)],
            out_specs=pl.BlockSpec((tq, Q_DIM), main)),
        compiler_params=_cparams("arbitrary"),
        name="banded_attention",
    )(sink, q, k, k, k, v, v, v, kc, vc)


def _dft_tables(na, nb):
    n = na * nb
    s1 = 2.0 ** (-(int(math.log2(na)) // 2))
    s2 = (1.0 / math.sqrt(n)) / s1
    ka = jnp.arange(na, dtype=I32)[None, :, None]
    a = jnp.arange(na, dtype=I32)[None, None, :]
    b = jnp.arange(nb, dtype=I32)[:, None, None]
    th = ((ka * (b + nb * a)) % n).astype(F32) * (2.0 * math.pi / n)
    tr, ti = jnp.cos(th) * s1, -jnp.sin(th) * s1
    t1 = jnp.concatenate([jnp.concatenate([tr, ti], axis=-1),
                          jnp.concatenate([ti, -tr], axis=-1)], axis=-2).astype(BF16)
    kb = jnp.arange(nb, dtype=I32)
    th2 = ((kb[:, None] * kb[None, :]) % nb).astype(F32) * (2.0 * math.pi / nb)
    t2 = (jnp.concatenate([jnp.cos(th2), jnp.sin(th2)], axis=-1) * s2).astype(BF16)
    return t1, t2


def _dft1_kernel(z_ref, t_ref, y_ref):
    planes, _, rows, _ = z_ref.shape
    step = SUBLANES_F32
    half, na = planes // 2, rows // step
    for j in range(step):
        sel = pl.ds(j, na, stride=step)
        p = jnp.concatenate([z_ref[cc, 0, sel, :] for cc in range(half)], axis=1)
        q = jnp.concatenate([z_ref[cc, 0, sel, :] for cc in range(half, planes)], axis=1)
        xs = jnp.concatenate([p, q], axis=0).astype(BF16)
        y = jnp.dot(t_ref[j], xs, preferred_element_type=F32)
        for cc in range(half):
            y_ref[cc, 0, sel, :] = y[:na, cc * LANES:(cc + 1) * LANES]
            y_ref[half + cc, 0, sel, :] = y[na:, cc * LANES:(cc + 1) * LANES]


def _dft2_kernel(y_ref, t_ref, o_ref):
    planes, groups, rows, _ = y_ref.shape
    step = SUBLANES_F32
    half, nb = planes // 2, groups * step

    def tokens_b(cc, j):
        return y_ref[cc, :, j * step:(j + 1) * step, :].reshape(nb, LANES)

    for j in range(rows // step):
        yr = jnp.concatenate([tokens_b(cc, j) for cc in range(half)], axis=1)
        yi = jnp.concatenate([tokens_b(cc, j) for cc in range(half, planes)], axis=1)
        xs = jnp.concatenate([yr, yi], axis=0).astype(BF16)
        out = jnp.dot(t_ref[...], xs, preferred_element_type=F32)
        for cc in range(half):
            o_ref[cc, 0, pl.ds(j, nb, stride=step), :] = out[:, cc * LANES:(cc + 1) * LANES]


def _position_dft(z):
    planes, groups, rows, _ = z.shape
    step = SUBLANES_F32
    nb, na = groups * step, rows // step
    t1, t2 = _dft_tables(na, nb)
    y = pl.pallas_call(
        _dft1_kernel,
        out_shape=jax.ShapeDtypeStruct(z.shape, F32),
        grid=(groups,),
        in_specs=[pl.BlockSpec((planes, 1, rows, LANES), lambda i: (0, i, 0, 0)),
                  pl.BlockSpec((step, 2 * na, 2 * na), lambda i: (i, 0, 0))],
        out_specs=pl.BlockSpec((planes, 1, rows, LANES), lambda i: (0, i, 0, 0)),
        compiler_params=_cparams("arbitrary"),
        name="position_dft_stage1",
    )(z, t1)
    return pl.pallas_call(
        _dft2_kernel,
        out_shape=jax.ShapeDtypeStruct((planes // 2, na // step, nb * step, LANES), F32),
        grid=(na // step,),
        in_specs=[pl.BlockSpec((planes, groups, step * step, LANES), lambda i: (0, 0, i, 0)),
                  pl.BlockSpec((nb, 2 * nb), lambda i: (0, 0))],
        out_specs=pl.BlockSpec((planes // 2, 1, nb * step, LANES), lambda i: (0, i, 0, 0)),
        compiler_params=_cparams("arbitrary"),
        name="position_dft_stage2",
    )(y, t2)


def _outproj_kernel(ax_ref, fx_ref, x_ref, wo_ref, gate_ref, nw_ref, sh_ref, sc_ref, wr_ref,
                    x1_ref, g_ref, aff_ref):
    planes, groups, rows, _ = fx_ref.shape
    na = groups * SUBLANES_F32
    n_kb = rows // SUBLANES_F32
    kb_per_chunk = max(1, n_kb // OUTPROJ_CHUNKS)
    for first_kb in range(0, n_kb, kb_per_chunk):
        rs = slice(first_kb * na, (first_kb + kb_per_chunk) * na)
        fx = jnp.concatenate(
            [jnp.concatenate(
                [fx_ref[cc, :, kb * SUBLANES_F32:(kb + 1) * SUBLANES_F32, :].reshape(na, LANES)
                 for kb in range(first_kb, first_kb + kb_per_chunk)], axis=0)
             for cc in range(planes)], axis=1).astype(BF16)
        acc = (jnp.dot(ax_ref[rs, :], wo_ref[:Q_DIM, :], preferred_element_type=F32)
               + jnp.dot(fx, wo_ref[Q_DIM:, :], preferred_element_type=F32))
        x1 = x_ref[rs, :] + gate_ref[...] * acc
        x1_ref[rs, :] = x1
        gb = _rms_mod(x1, nw_ref[...], sh_ref[...], sc_ref[...]).astype(BF16)
        g_ref[rs, :] = gb
        logits = jnp.dot(gb, wr_ref[...], preferred_element_type=F32)
        e = jnp.exp(logits - jnp.max(logits, axis=1, keepdims=True))
        aff_ref[rs, :] = e / jnp.sum(e, axis=1, keepdims=True)


def _output_projection(ax, fx, x, wo_b, gate, nw, shift, scale, wr_b, tm=512):
    n, d = x.shape
    ne = wr_b.shape[1]
    row = lambda i: (i, 0)
    const = lambda i: (0, 0)
    vec = pl.BlockSpec((1, d), const)
    return pl.pallas_call(
        _outproj_kernel,
        out_shape=(jax.ShapeDtypeStruct((n, d), F32), jax.ShapeDtypeStruct((n, d), BF16),
                   jax.ShapeDtypeStruct((n, ne), F32)),
        grid=(n // tm,),
        in_specs=[pl.BlockSpec((tm, Q_DIM), row),
                  pl.BlockSpec((fx.shape[0], fx.shape[1], tm // (fx.shape[1] * SUBLANES_F32)
                                * SUBLANES_F32, LANES), lambda i: (0, 0, i, 0)),
                  pl.BlockSpec((tm, d), row),
                  pl.BlockSpec((Q_DIM + F_DIM, d), const, pipeline_mode=pl.Buffered(1)),
                  vec, vec, vec, vec, pl.BlockSpec((d, ne), const)],
        out_specs=(pl.BlockSpec((tm, d), row), pl.BlockSpec((tm, d), row),
                   pl.BlockSpec((tm, ne), row)),
        compiler_params=_cparams("arbitrary"),
        name="output_projection",
    )(ax, fx, x, wo_b, gate, nw, shift, scale, wr_b)


def _route_kernel(aff_ref, posm_ref, cnt_ref, off_ref, *, cap, n_exp, n_chunk):
    aff_all = aff_ref[...]
    blocks = [aff_all[e * n_chunk:(e + 1) * n_chunk] for e in range(n_exp)]

    def total(mask):
        s = jnp.sum(jnp.where(mask, 1.0, 0.0), axis=0, keepdims=True)
        return jnp.sum(s, axis=1, keepdims=True)

    def search(it, thr_bits):
        bit = jnp.left_shift(jnp.int32(1), 30 - it)
        out = []
        for e in range(n_exp):
            cand = thr_bits[e] | bit
            enough = total(blocks[e] >= lax.bitcast_convert_type(cand, F32)) >= cap
            out.append(jnp.where(enough, cand, thr_bits[e]))
        return tuple(out)

    thr_bits = lax.fori_loop(0, 31, search, tuple(jnp.zeros((1, 1), I32) for _ in range(n_exp)))
    thr = [lax.bitcast_convert_type(t, F32) for t in thr_bits]

    rl = lax.broadcasted_iota(I32, (LANES, LANES), 0)
    cl = lax.broadcasted_iota(I32, (LANES, LANES), 1)
    before = jnp.where(rl < cl, 1.0, 0.0).astype(BF16)
    ones = jnp.ones((LANES, LANES), BF16)
    rc = lax.broadcasted_iota(I32, (n_chunk, n_chunk), 0)
    cc = lax.broadcasted_iota(I32, (n_chunk, n_chunk), 1)
    earlier = jnp.where(cc < rc, 1.0, 0.0).astype(BF16)

    def excl_cumsum(mask):
        mb = jnp.where(mask, 1.0, 0.0).astype(BF16)
        within = jnp.dot(mb, before, preferred_element_type=F32)
        rowtot = jnp.dot(mb, ones, preferred_element_type=F32)
        choff = jnp.dot(earlier, rowtot.astype(BF16), preferred_element_type=F32)
        return within + choff, rowtot, choff

    for e in range(n_exp):
        gt = blocks[e] > thr[e]
        eq = blocks[e] == thr[e]
        need = cap - total(gt)
        tie_rank, _, _ = excl_cumsum(eq)
        sel = gt | (eq & (tie_rank < need))
        pos, rowtot, choff = excl_cumsum(sel)
        rs = slice(e * n_chunk, (e + 1) * n_chunk)
        posm_ref[rs, :] = jnp.where(sel, pos.astype(I32), NOT_SELECTED)
        cnt_ref[rs, :] = rowtot.astype(I32)
        off_ref[rs, :] = choff.astype(I32)


def _route(aff2, cap, n_exp):
    rows = aff2.shape[0]
    n_chunk = rows // n_exp
    spec = pl.BlockSpec((rows, LANES), lambda i: (0, 0))
    shp = jax.ShapeDtypeStruct((rows, LANES), I32)
    return pl.pallas_call(
        functools.partial(_route_kernel, cap=cap, n_exp=n_exp, n_chunk=n_chunk),
        out_shape=(shp, shp, shp),
        grid=(1,),
        in_specs=[spec],
        out_specs=(spec, spec, spec),
        compiler_params=_cparams("arbitrary"),
        name="expert_choice_routing",
    )(aff2)


def _window_matches(posm, firsts, win_starts):
    r = lax.broadcasted_iota(I32, (WINDOW_ROWS, TOK_TILE), 0)
    out = []
    for e, (first, start) in enumerate(zip(firsts, win_starts)):
        pm = posm[e:e + 1, :]
        pm = jnp.where((pm >= first) & (pm < first + PIECE), pm, NOT_SELECTED)
        out.append((pm - start) == r)
    return out


def _align_down(v):
    return (v // ROW_ALIGN) * ROW_ALIGN


def _gather_kernel(off_ref, cnt_ref, g_ref, posm_ref, aff_ref, xs_hbm,
                   stg, tail, sem, npass_ref, *, n_exp, n_chunk, cap):
    c = pl.program_id(0)
    stride = cap + WINDOW_ROWS

    @pl.when(c == 0)
    def _():
        npass_ref[0] = 0
        tail[...] = jnp.zeros_like(tail)

    def window_copy(slot, e, dst):
        return pltpu.make_async_copy(stg.at[slot, pl.ds(e * WINDOW_ROWS, WINDOW_ROWS)],
                                     xs_hbm.at[pl.ds(dst, WINDOW_ROWS)], sem.at[slot])

    def wait_slot(slot):
        for e in range(n_exp):
            window_copy(slot, e, 0).wait()

    @pl.when(c == 0)
    def _():
        stg[0] = jnp.zeros(stg.shape[1:], stg.dtype)
        for e in range(n_exp):
            window_copy(0, e, e * stride + cap).start()
        wait_slot(0)

    most = cnt_ref[c]
    for e in range(1, n_exp):
        most = jnp.maximum(most, cnt_ref[e * n_chunk + c])
    passes = (most + PIECE - 1) // PIECE

    def one_pass(p, carry):
        done = npass_ref[0]
        slot = done % 2
        posm = posm_ref[0]
        aff = aff_ref[0]
        firsts, win_starts, next_shift = [], [], []
        for e in range(n_exp):
            o, n_e = off_ref[e * n_chunk + c], cnt_ref[e * n_chunk + c]
            first = o + jnp.minimum(p * PIECE, n_e)
            after = o + jnp.minimum((p + 1) * PIECE, n_e)
            firsts.append(first)
            win_starts.append(_align_down(first))
            next_shift.append(_align_down(after) - _align_down(first))
        matches = _window_matches(posm, firsts, win_starts)
        onehot = jnp.concatenate([jnp.where(mt, 1.0, 0.0).astype(BF16) for mt in matches], axis=0)
        new = jnp.dot(onehot, g_ref[...], preferred_element_type=F32)
        lane = lax.broadcasted_iota(I32, (WINDOW_ROWS, GATE_LANES), 1)
        for e in range(n_exp):
            lo = e * WINDOW_ROWS
            gate = jnp.sum(jnp.where(matches[e], aff[e:e + 1, :], 0.0), axis=1, keepdims=True)
            hi = gate.astype(BF16).astype(F32)
            mid = (gate - hi).astype(BF16).astype(F32)
            low = gate - hi - mid
            parts = jnp.where(lane == 0, hi, jnp.where(lane == 1, mid,
                                                       jnp.where(lane == 2, low, 0.0)))
            rows = jnp.concatenate([new[lo:lo + WINDOW_ROWS], parts], axis=1)
            stg[slot, lo:lo + ROW_ALIGN] = (rows[:ROW_ALIGN] + tail[e].astype(F32)).astype(BF16)
            stg[slot, lo + ROW_ALIGN:lo + WINDOW_ROWS] = rows[ROW_ALIGN:].astype(BF16)

        @pl.when(done > 0)
        def _():
            wait_slot(1 - slot)

        for e in range(n_exp):
            window_copy(slot, e, pl.multiple_of(e * stride + win_starts[e], ROW_ALIGN)).start()
        for e in range(n_exp):
            src = pl.ds(pl.multiple_of(e * WINDOW_ROWS + next_shift[e], ROW_ALIGN), ROW_ALIGN)
            tail[e] = stg[slot, src, :]
        npass_ref[0] = done + 1
        return carry

    lax.fori_loop(0, passes, one_pass, 0)

    @pl.when(c == pl.num_programs(0) - 1)
    def _():
        done = npass_ref[0]

        @pl.when(done > 0)
        def _():
            wait_slot((done - 1) % 2)


def _gather(off, cnt, g, posm_t, aff_t, cap):
    n, d = g.shape
    n_chunk, n_exp, _ = posm_t.shape
    rows = n_exp * (cap + WINDOW_ROWS)
    tile3 = lambda i, o, c: (i, 0, 0)
    return pl.pallas_call(
        functools.partial(_gather_kernel, n_exp=n_exp, n_chunk=n_chunk, cap=cap),
        out_shape=jax.ShapeDtypeStruct((rows, d + GATE_LANES), BF16),
        grid_spec=pltpu.PrefetchScalarGridSpec(
            num_scalar_prefetch=2,
            grid=(n_chunk,),
            in_specs=[pl.BlockSpec((TOK_TILE, d), lambda i, o, c: (i, 0)),
                      pl.BlockSpec((1, n_exp, TOK_TILE), tile3),
                      pl.BlockSpec((1, n_exp, TOK_TILE), tile3)],
            out_specs=pl.BlockSpec(memory_space=pl.ANY),
            scratch_shapes=[pltpu.VMEM((2, n_exp * WINDOW_ROWS, d + GATE_LANES), BF16),
                            pltpu.VMEM((n_exp, ROW_ALIGN, d + GATE_LANES), BF16),
                            pltpu.SemaphoreType.DMA((2,)),
                            pltpu.SMEM((1,), I32)]),
        compiler_params=_cparams("arbitrary"),
        name="gather_expert_tokens",
    )(off, cnt, g, posm_t, aff_t)


def _expert_kernel(xs_ref, wg_ref, wu_ref, wd_ref, y_ref, h_ref, *, nf, tf, tn):
    j = pl.program_id(1)
    d = wg_ref.shape[1]

    @pl.when(j < nf)
    def _():
        xs = xs_ref[0, :, :d]
        gp = jnp.dot(xs, wg_ref[0].astype(BF16), preferred_element_type=F32)
        up = jnp.dot(xs, wu_ref[0].astype(BF16), preferred_element_type=F32)
        h = ((gp * jax.nn.sigmoid(gp)) * up).astype(BF16)
        for jj in range(nf):
            @pl.when(j == jj)
            def _():
                h_ref[:, jj * tf:(jj + 1) * tf] = h

    @pl.when(j >= nf)
    def _():
        y = jnp.dot(h_ref[...], wd_ref[0].astype(BF16), preferred_element_type=F32)
        parts = xs_ref[0, :, d:].astype(F32)
        gate = parts[:, 0:1] + parts[:, 1:2] + parts[:, 2:3]
        y_ref[0] = (y * gate).astype(BF16)


def _experts(xs3, w_gate, w_up, w_down, cap, tf=256, tn=512):
    n_exp, d, ff = w_gate.shape
    tf, tn = min(tf, ff), min(tn, d)
    nf, nn = ff // tf, d // tn
    return pl.pallas_call(
        functools.partial(_expert_kernel, nf=nf, tf=tf, tn=tn),
        out_shape=jax.ShapeDtypeStruct((n_exp, cap, d), BF16),
        grid=(n_exp, nf + nn),
        in_specs=[pl.BlockSpec((1, cap, d + GATE_LANES), lambda e, j: (e, 0, 0)),
                  pl.BlockSpec((1, d, tf), lambda e, j: (e, 0, jnp.minimum(j, nf - 1))),
                  pl.BlockSpec((1, d, tf), lambda e, j: (e, 0, jnp.minimum(j, nf - 1))),
                  pl.BlockSpec((1, ff, tn), lambda e, j: (e, 0, jnp.maximum(j - nf, 0)))],
        out_specs=pl.BlockSpec((1, cap, tn), lambda e, j: (e, 0, jnp.maximum(j - nf, 0))),
        scratch_shapes=[pltpu.VMEM((cap, ff), BF16)],
        compiler_params=_cparams("arbitrary", "arbitrary"),
        name="swiglu_experts",
    )(xs3, w_gate, w_up, w_down)


def _combine_kernel(off_ref, cnt_ref, y_hbm, posm_ref, x1_ref, gate_ref, nw_ref, o_ref,
                    stg, acc_ref, sem, *, n_exp, n_chunk, cap):
    c = pl.program_id(0)
    last_start = n_exp * cap - WINDOW_ROWS

    def firsts_of(tile, p):
        return [off_ref[e * n_chunk + tile] + p * PIECE for e in range(n_exp)]

    def starts_of(tile, p):
        return [pl.multiple_of(jnp.minimum(e * cap + _align_down(first), last_start), ROW_ALIGN)
                for e, first in enumerate(firsts_of(tile, p))]

    def piece_copy(slot, e, src):
        return pltpu.make_async_copy(y_hbm.at[pl.ds(src, WINDOW_ROWS)],
                                     stg.at[slot, pl.ds(e * WINDOW_ROWS, WINDOW_ROWS)],
                                     sem.at[slot])

    def fetch(tile, p, slot):
        for e, src in enumerate(starts_of(tile, p)):
            piece_copy(slot, e, src).start()

    def wait_slot(slot):
        for e in range(n_exp):
            piece_copy(slot, e, 0).wait()

    def contribution(p, slot):
        starts = [s - e * cap for e, s in enumerate(starts_of(c, p))]
        matches = _window_matches(posm_ref[0], firsts_of(c, p), starts)
        onehot = jnp.concatenate([jnp.where(mt, 1.0, 0.0).astype(BF16) for mt in matches], axis=0)
        return lax.dot_general(onehot, stg[slot], (((0,), (0,)), ((), ())),
                               preferred_element_type=F32)

    slot = c % 2

    @pl.when(c == 0)
    def _():
        fetch(0, 0, 0)

    @pl.when(c + 1 < pl.num_programs(0))
    def _():
        fetch(c + 1, 0, 1 - slot)

    wait_slot(slot)
    acc_ref[...] = contribution(0, slot)

    most = cnt_ref[c]
    for e in range(1, n_exp):
        most = jnp.maximum(most, cnt_ref[e * n_chunk + c])
    passes = (most + PIECE - 1) // PIECE

    def extra_pass(p, carry):
        fetch(c, p, slot)
        wait_slot(slot)
        acc_ref[...] += contribution(p, slot)
        return carry

    lax.fori_loop(1, passes, extra_pass, 0)
    x2 = x1_ref[...] + gate_ref[...] * acc_ref[...]
    ms = jnp.mean(x2 * x2, axis=-1, keepdims=True)
    o_ref[...] = x2 * lax.rsqrt(ms + RMS_EPS) * nw_ref[...]


def _combine(off, cnt, y, posm_t, x1, gate, nw, cap):
    n, d = x1.shape
    n_chunk, n_exp, _ = posm_t.shape
    return pl.pallas_call(
        functools.partial(_combine_kernel, n_exp=n_exp, n_chunk=n_chunk, cap=cap),
        out_shape=jax.ShapeDtypeStruct((n, d), F32),
        grid_spec=pltpu.PrefetchScalarGridSpec(
            num_scalar_prefetch=2,
            grid=(n_chunk,),
            in_specs=[pl.BlockSpec(memory_space=pl.ANY),
                      pl.BlockSpec((1, n_exp, TOK_TILE), lambda i, o, c: (i, 0, 0)),
                      pl.BlockSpec((TOK_TILE, d), lambda i, o, c: (i, 0)),
                      pl.BlockSpec((1, d), lambda i, o, c: (0, 0)),
                      pl.BlockSpec((1, d), lambda i, o, c: (0, 0))],
            out_specs=pl.BlockSpec((TOK_TILE, d), lambda i, o, c: (i, 0)),
            scratch_shapes=[pltpu.VMEM((2, n_exp * WINDOW_ROWS, d), BF16),
                            pltpu.VMEM((TOK_TILE, d), F32),
                            pltpu.SemaphoreType.DMA((2,))]),
        compiler_params=_cparams("arbitrary"),
        name="combine_experts_final_norm",
    )(off, cnt, y, posm_t, x1, gate, nw)


def kernel(x, c, ctx, c_ctx, w_mod, b_mod, norm_mix, w_in, sink, w_fourier, w_out, norm_ffn,
           w_router, w_gate, w_up, w_down, norm_final):
    assert x.shape[0] == 1 and w_mod.shape[0] == 1
    n, d = x.shape[1], x.shape[2]
    xl, cx = x[0], ctx[0]
    n_exp = w_router.shape[2]
    cap = max(1, CAP_FACTOR * n // n_exp)
    n_chunk = n // LANES
    n_tile = n // TOK_TILE
    per_tile = TOK_TILE // LANES

    mod = _modulation(jnp.stack([c[0], c_ctx], axis=1), w_mod[0], b_mod[0][None, :])
    mx = mod[0].reshape(6, 1, d)
    mc = mod[1].reshape(6, 1, d)

    w_in_b = w_in[0].astype(BF16)
    nmix = norm_mix[0][None, :]
    ab = _fold_channel_dft(w_fourier[0])
    q, k, v, z = _project_latent(xl, nmix, mx[0], mx[1], w_in_b, ab)
    kc, vc = _project_context(cx, nmix, mc[0], mc[1], w_in_b[:, Q_DIM:Q_DIM + 2 * KV_DIM])
    ax = _attention(q, k, v, kc, vc, sink[0])
    fx = _position_dft(z)

    x1, g, aff = _output_projection(ax, fx, xl, w_out[0].astype(BF16), mx[2], norm_ffn[0][None, :],
                                    mx[3], mx[4], w_router[0].astype(BF16))

    aff2 = aff.T.reshape(n_exp * n_chunk, LANES)
    posm, cnt, off = _route(aff2, cap, n_exp)
    by_tile = lambda a: a.reshape(n_exp, n_tile, TOK_TILE).transpose(1, 0, 2)
    posm_t, aff_t = by_tile(posm), by_tile(aff2)
    cnt1 = cnt[:, 0].reshape(n_exp * n_tile, per_tile).sum(axis=1)
    off1 = off[:, 0].reshape(n_exp * n_tile, per_tile)[:, 0]

    xs = _gather(off1, cnt1, g, posm_t, aff_t, cap)
    y = _experts(xs.reshape(n_exp, cap + WINDOW_ROWS, d + GATE_LANES),
                 w_gate[0], w_up[0], w_down[0], cap)
    out = _combine(off1, cnt1, y.reshape(n_exp * cap, d), posm_t, x1, mx[5], norm_final[None, :],
                   cap)
    return out[None]
```

```python
import functools
import math

import jax
import jax.numpy as jnp
from jax import lax
from jax.experimental import pallas as pl
from jax.experimental.pallas import tpu as pltpu

F32 = jnp.float32
BF16 = jnp.bfloat16
I32 = jnp.int32

HEAD_DIM = 128
N_HEADS = 8
N_KV_HEADS = 2
GROUP = N_HEADS // N_KV_HEADS
WINDOW = 128
Q_DIM = N_HEADS * HEAD_DIM
KV_DIM = N_KV_HEADS * HEAD_DIM
N_FGROUPS = 4
FG = 256
F_DIM = N_FGROUPS * FG
D_IN = Q_DIM + 2 * KV_DIM + F_DIM
N_EXPERTS = 16
CAP_FACTOR = 2
GRID_W = 64
ROPE_THETA = 10000.0
RMS_EPS = 1e-6
NEG_INF = -1e30
LOG2_E = math.log2(math.e)

LANES = 128
SUBLANES_F32 = 8
VMEM_LIMIT_BYTES = 56 * 1024 * 1024

TOK_TILE = 2 * LANES
PIECE = 48
ROW_ALIGN = 16
WINDOW_ROWS = ROW_ALIGN + PIECE
GATE_LANES = LANES
DFT_NB = LANES
OUTPROJ_CHUNKS = 2
PROJ_CHUNKS = 2
NOT_SELECTED = -(1 << 20)


def _cparams(*sem):
    return pltpu.CompilerParams(dimension_semantics=sem, vmem_limit_bytes=VMEM_LIMIT_BYTES)


def _rms_mod(x, nw, shift, scale):
    ms = jnp.mean(x * x, axis=-1, keepdims=True)
    h = x * lax.rsqrt(ms + RMS_EPS) * nw
    return h * (1.0 + scale) + shift


def _mod_kernel(ct_ref, w_ref, b_ref, o_ref):
    ct = ct_ref[...]
    s = ct * jax.nn.sigmoid(ct)
    w = w_ref[...]
    r0 = jnp.sum(s[:, 0:1] * w, axis=0, keepdims=True)
    r1 = jnp.sum(s[:, 1:2] * w, axis=0, keepdims=True)
    o_ref[...] = jnp.concatenate([r0, r1], axis=0) + b_ref[...]


def _modulation(ct, w_mod, b_mod):
    d, n = w_mod.shape
    tn = math.gcd(n, 1024)
    return pl.pallas_call(
        _mod_kernel,
        out_shape=jax.ShapeDtypeStruct((2, n), F32),
        grid=(n // tn,),
        in_specs=[pl.BlockSpec((d, 2), lambda i: (0, 0)),
                  pl.BlockSpec((d, tn), lambda i: (0, i)),
                  pl.BlockSpec((1, tn), lambda i: (0, i))],
        out_specs=pl.BlockSpec((2, tn), lambda i: (0, i)),
        compiler_params=_cparams("arbitrary"),
        name="modulation",
    )(ct, w_mod, b_mod)


def _ab_kernel(cf_ref, sf_ref, wf_ref, ab_ref):
    wf = wf_ref[0]
    a = jnp.dot(cf_ref[...], wf, preferred_element_type=F32, precision=lax.Precision.HIGHEST)
    b = jnp.dot(sf_ref[...], wf, preferred_element_type=F32, precision=lax.Precision.HIGHEST)
    ab_ref[0] = jnp.concatenate([a, b], axis=1).astype(BF16)


def _fold_channel_dft(w_fourier):
    g, fg, _ = w_fourier.shape
    idx = jnp.arange(fg, dtype=I32)
    th = ((idx[:, None] * idx[None, :]) % fg).astype(F32) * (2.0 * math.pi / fg)
    scale = 1.0 / math.sqrt(fg)
    cf = jnp.cos(th) * scale
    sf = jnp.sin(th) * scale
    return pl.pallas_call(
        _ab_kernel,
        out_shape=jax.ShapeDtypeStruct((g, fg, 2 * fg), BF16),
        grid=(g,),
        in_specs=[pl.BlockSpec((fg, fg), lambda i: (0, 0)),
                  pl.BlockSpec((fg, fg), lambda i: (0, 0)),
                  pl.BlockSpec((1, fg, fg), lambda i: (i, 0, 0))],
        out_specs=pl.BlockSpec((1, fg, 2 * fg), lambda i: (i, 0, 0)),
        compiler_params=_cparams("arbitrary"),
        name="fold_channel_dft",
    )(cf, sf, w_fourier)


def _rope_tables(n):
    quarter = HEAD_DIM // 4
    freqs = ROPE_THETA ** (-jnp.arange(quarter, dtype=F32) / quarter)
    zeros = lambda m: jnp.zeros((m, 2 * quarter), F32)

    def tables(pos, low_half):
        ang = pos[:, None] * freqs[None, :]
        ang = jnp.concatenate([ang, ang], axis=-1)
        cos, sin = jnp.cos(ang), jnp.sin(ang)
        first = (jnp.arange(2 * quarter) < quarter)[None, :]
        parts = (cos, jnp.where(first, -sin, 0.0), jnp.where(first, 0.0, sin))
        pad = zeros(pos.shape[0])
        return jnp.stack([jnp.concatenate([p, pad] if low_half else [pad, p], axis=-1)
                          for p in parts])

    row_tab = tables(jnp.arange(n // GRID_W, dtype=F32), True)
    col_tab = tables(jnp.arange(GRID_W, dtype=F32), False)
    return row_tab, col_tab


def _store_dft_rows(z_ref, plane, val, a0):
    groups = z_ref.shape[1]
    nb = groups * SUBLANES_F32
    for al in range(val.shape[0] // nb):
        z_ref[plane, :, (a0 + al) * SUBLANES_F32:(a0 + al + 1) * SUBLANES_F32, :] = (
            val[al * nb:(al + 1) * nb].reshape(groups, SUBLANES_F32, LANES))


def _proj_kernel(x_ref, nw_ref, sh_ref, sc_ref, w_ref, ab_ref, rt_ref, ct_ref,
                 q_ref, k_ref, v_ref, z_ref):
    quarter = HEAD_DIM // 4
    qscale = LOG2_E / math.sqrt(HEAD_DIM)
    tm = x_ref.shape[0]
    chunk = tm // PROJ_CHUNKS
    for r0 in range(0, tm, chunk):
        rs = slice(r0, r0 + chunk)
        hb = _rms_mod(x_ref[rs, :], nw_ref[...], sh_ref[...], sc_ref[...]).astype(BF16)
        cos, s_up, s_dn = (
            jnp.concatenate(
                [rt_ref[t, r:r + 1, :] + ct_ref[t]
                 for r in range(r0 // GRID_W, (r0 + chunk) // GRID_W)], axis=0)
            for t in range(3))

        def rope(t):
            return (t * cos + pltpu.roll(t, HEAD_DIM - quarter, 1) * s_up
                    + pltpu.roll(t, quarter, 1) * s_dn)

        q = jnp.dot(hb, w_ref[:, :Q_DIM], preferred_element_type=F32)
        for j in range(N_HEADS):
            sl = slice(j * HEAD_DIM, (j + 1) * HEAD_DIM)
            q_ref[rs, sl] = (rope(q[:, sl]) * qscale).astype(BF16)
        k = jnp.dot(hb, w_ref[:, Q_DIM:Q_DIM + KV_DIM], preferred_element_type=F32)
        for j in range(N_KV_HEADS):
            sl = slice(j * HEAD_DIM, (j + 1) * HEAD_DIM)
            k_ref[rs, sl] = rope(k[:, sl]).astype(BF16)
        v = jnp.dot(hb, w_ref[:, Q_DIM + KV_DIM:Q_DIM + 2 * KV_DIM], preferred_element_type=F32)
        v_ref[rs, :] = v.astype(BF16)
        u = jnp.dot(hb, w_ref[:, Q_DIM + 2 * KV_DIM:], preferred_element_type=F32).astype(BF16)
        for g in range(N_FGROUPS):
            pq = jnp.dot(u[:, g * FG:(g + 1) * FG], ab_ref[g], preferred_element_type=F32)
            per_group = FG // LANES
            for t in range(2 * per_group):
                plane = (t // per_group) * (F_DIM // LANES) + g * per_group + t % per_group
                _store_dft_rows(z_ref, plane, pq[:, t * LANES:(t + 1) * LANES], r0 // DFT_NB)


def _project_latent(x, nw, shift, scale, w_in_b, ab, tm=512):
    n, d = x.shape
    row_tab, col_tab = _rope_tables(n)
    row = lambda i: (i, 0)
    const2 = lambda i: (0, 0)
    return pl.pallas_call(
        _proj_kernel,
        out_shape=(jax.ShapeDtypeStruct((n, Q_DIM), BF16),
                   jax.ShapeDtypeStruct((n, KV_DIM), BF16),
                   jax.ShapeDtypeStruct((n, KV_DIM), BF16),
                   jax.ShapeDtypeStruct((2 * F_DIM // LANES, DFT_NB // SUBLANES_F32,
                                         (n // DFT_NB) * SUBLANES_F32, LANES), F32)),
        grid=(n // tm,),
        in_specs=[pl.BlockSpec((tm, d), row),
                  pl.BlockSpec((1, d), const2), pl.BlockSpec((1, d), const2),
                  pl.BlockSpec((1, d), const2),
                  pl.BlockSpec((d, D_IN), const2, pipeline_mode=pl.Buffered(1)),
                  pl.BlockSpec((N_FGROUPS, FG, 2 * FG), lambda i: (0, 0, 0),
                               pipeline_mode=pl.Buffered(1)),
                  pl.BlockSpec((3, tm // GRID_W, HEAD_DIM), lambda i: (0, i, 0)),
                  pl.BlockSpec((3, GRID_W, HEAD_DIM), lambda i: (0, 0, 0))],
        out_specs=(pl.BlockSpec((tm, Q_DIM), row), pl.BlockSpec((tm, KV_DIM), row),
                   pl.BlockSpec((tm, KV_DIM), row),
                   pl.BlockSpec((2 * F_DIM // LANES, DFT_NB // SUBLANES_F32,
                                 (tm // DFT_NB) * SUBLANES_F32, LANES), lambda i: (0, 0, i, 0))),
        compiler_params=_cparams("arbitrary"),
        name="project_latent",
    )(x, nw, shift, scale, w_in_b, ab, row_tab, col_tab)


def _ctx_kernel(x_ref, nw_ref, sh_ref, sc_ref, w_ref, k_ref, v_ref):
    hb = _rms_mod(x_ref[...], nw_ref[...], sh_ref[...], sc_ref[...]).astype(BF16)
    kv = jnp.dot(hb, w_ref[...], preferred_element_type=F32)
    k_ref[...] = kv[:, :KV_DIM].astype(BF16)
    v_ref[...] = kv[:, KV_DIM:].astype(BF16)


def _project_context(ctx, nw, shift, scale, w_kv_b):
    m, d = ctx.shape
    full = lambda shp: pl.BlockSpec(shp, lambda i: (0, 0))
    return pl.pallas_call(
        _ctx_kernel,
        out_shape=(jax.ShapeDtypeStruct((m, KV_DIM), BF16),
                   jax.ShapeDtypeStruct((m, KV_DIM), BF16)),
        grid=(1,),
        in_specs=[full((m, d)), full((1, d)), full((1, d)), full((1, d)), full((d, 2 * KV_DIM))],
        out_specs=(full((m, KV_DIM)), full((m, KV_DIM))),
        compiler_params=_cparams("arbitrary"),
        name="project_context",
    )(ctx, nw, shift, scale, w_kv_b)


def _attn_kernel(sink_ref, q_ref, kp_ref, km_ref, kn_ref, vp_ref, vm_ref, vn_ref,
                 kc_ref, vc_ref, lo_ref, hi_ref, o_ref, *, n_total, tq):
    i = pl.program_id(0)
    nsub = tq // WINDOW
    last_blk = n_total // WINDOW - 1
    kwin = jnp.concatenate([kp_ref[...], km_ref[...], kn_ref[...]], axis=0)
    vwin = jnp.concatenate([vp_ref[...], vm_ref[...], vn_ref[...]], axis=0)
    rows = GROUP * WINDOW
    span = 3 * WINDOW
    head_of_row = lax.broadcasted_iota(I32, (rows, 1), 0) // WINDOW
    ones_loc = jnp.ones((span, HEAD_DIM), BF16)
    ones_ctx = jnp.ones((kc_ref.shape[0], HEAD_DIM), BF16)
    nt = (((1,), (1,)), ((), ()))
    for b in range(nsub):
        blk = i * nsub + b
        bias_lo = lo_ref[(blk == 0).astype(I32)]
        bias_hi = hi_ref[(blk == last_blk).astype(I32)]
        for h in range(N_KV_HEADS):
            hs = slice(h * HEAD_DIM, (h + 1) * HEAD_DIM)
            qs = jnp.concatenate(
                [q_ref[b * WINDOW:(b + 1) * WINDOW,
                       (h * GROUP + g) * HEAD_DIM:(h * GROUP + g + 1) * HEAD_DIM]
                 for g in range(GROUP)], axis=0)
            kw = kwin[b * WINDOW:b * WINDOW + span, hs]
            vw = vwin[b * WINDOW:b * WINDOW + span, hs]
            s_loc = lax.dot_general(qs, kw, nt, preferred_element_type=F32)
            s_ctx = lax.dot_general(qs, kc_ref[:, hs], nt, preferred_element_type=F32)
            parts = [s_loc[:, :WINDOW] + bias_lo, s_loc[:, WINDOW:2 * WINDOW],
                     s_loc[:, 2 * WINDOW:] + bias_hi, s_ctx]
            sink_col = jnp.zeros((rows, 1), F32)
            for g in range(GROUP):
                sink_col = jnp.where(head_of_row == g, sink_ref[h * GROUP + g] * LOG2_E, sink_col)
            blocks = parts[:3] + [s_ctx[:, t * WINDOW:(t + 1) * WINDOW]
                                  for t in range(s_ctx.shape[1] // WINDOW)]
            widest = blocks[0]
            for blk_scores in blocks[1:]:
                widest = jnp.maximum(widest, blk_scores)
            m = jnp.maximum(sink_col, jnp.max(widest, axis=1, keepdims=True))
            p = [jnp.exp2(part - m).astype(BF16) for part in parts]
            ov = (jnp.dot(jnp.concatenate(p[:3], axis=1), jnp.concatenate([vw, ones_loc], axis=1),
                          preferred_element_type=F32)
                  + jnp.dot(p[3], jnp.concatenate([vc_ref[:, hs], ones_ctx], axis=1),
                            preferred_element_type=F32))
            o = ov[:, :HEAD_DIM] / (ov[:, HEAD_DIM:] + jnp.exp2(sink_col - m))
            for g in range(GROUP):
                o_ref[b * WINDOW:(b + 1) * WINDOW,
                      (h * GROUP + g) * HEAD_DIM:(h * GROUP + g + 1) * HEAD_DIM] = (
                    o[g * WINDOW:(g + 1) * WINDOW].astype(BF16))


def _band_biases():
    r = jnp.arange(GROUP * WINDOW, dtype=I32)[:, None] % WINDOW
    c = jnp.arange(WINDOW, dtype=I32)[None, :]
    masked = jnp.full((GROUP * WINDOW, WINDOW), NEG_INF, F32)
    lo = jnp.stack([jnp.where(c >= r, 0.0, NEG_INF).astype(F32), masked])
    hi = jnp.stack([jnp.where(c <= r, 0.0, NEG_INF).astype(F32), masked])
    return lo, hi


def _attention(q, k, v, kc, vc, sink, tq=512):
    n = q.shape[0]
    bias_lo, bias_hi = _band_biases()
    bias_spec = pl.BlockSpec(bias_lo.shape, lambda i, s: (0, 0, 0))
    m = kc.shape[0]
    nsub = tq // WINDOW
    nblk = n // WINDOW
    prev = lambda i, s: (jnp.maximum(i * nsub - 1, 0), 0)
    main = lambda i, s: (i, 0)
    nxt = lambda i, s: (jnp.minimum(i * nsub + nsub, nblk - 1), 0)
    const = lambda i, s: (0, 0)
    kv_specs = [pl.BlockSpec((WINDOW, KV_DIM), prev), pl.BlockSpec((tq, KV_DIM), main),
                pl.BlockSpec((WINDOW, KV_DIM), nxt)]
    return pl.pallas_call(
        functools.partial(_attn_kernel, n_total=n, tq=tq),
        out_shape=jax.ShapeDtypeStruct((n, Q_DIM), BF16),
        grid_spec=pltpu.PrefetchScalarGridSpec(
            num_scalar_prefetch=1,
            grid=(n // tq,),
            in_specs=[pl.BlockSpec((tq, Q_DIM), main)] + kv_specs + kv_specs
                     + [pl.BlockSpec((m, KV_DIM), const), pl.BlockSpec((m, KV_DIM), const),
                        bias_spec, bias_spec],
            out_specs=pl.BlockSpec((tq, Q_DIM), main)),
        compiler_params=_cparams("arbitrary"),
        name="banded_attention",
    )(sink, q, k, k, k, v, v, v, kc, vc, bias_lo, bias_hi)


def _dft_tables(na, nb):
    n = na * nb
    s1 = 2.0 ** (-(int(math.log2(na)) // 2))
    s2 = (1.0 / math.sqrt(n)) / s1
    ka = jnp.arange(na, dtype=I32)
    th_tw = ((jnp.arange(nb, dtype=I32)[:, None] * ka[None, :]) % n).astype(F32) * (2.0 * math.pi / n)
    th_f = ((ka[:, None] * ka[None, :]) % na).astype(F32) * (2.0 * math.pi / na)
    twr, twi = (jnp.cos(th_tw) * s1)[:, :, None], (-jnp.sin(th_tw) * s1)[:, :, None]
    fr, fi = jnp.cos(th_f)[None, :, :], -jnp.sin(th_f)[None, :, :]
    tr, ti = twr * fr - twi * fi, twr * fi + twi * fr
    t1 = jnp.concatenate([jnp.concatenate([tr, ti], axis=-1),
                          jnp.concatenate([ti, -tr], axis=-1)], axis=-2).astype(BF16)
    kb = jnp.arange(nb, dtype=I32)
    th2 = ((kb[:, None] * kb[None, :]) % nb).astype(F32) * (2.0 * math.pi / nb)
    t2 = (jnp.concatenate([jnp.cos(th2), jnp.sin(th2)], axis=-1) * s2).astype(BF16)
    return t1, t2


def _dft1_kernel(z_ref, t_ref, y_ref):
    planes, _, rows, _ = z_ref.shape
    step = SUBLANES_F32
    half, na = planes // 2, rows // step
    for j in range(step):
        sel = pl.ds(j, na, stride=step)
        p = jnp.concatenate([z_ref[cc, 0, sel, :] for cc in range(half)], axis=1)
        q = jnp.concatenate([z_ref[cc, 0, sel, :] for cc in range(half, planes)], axis=1)
        xs = jnp.concatenate([p, q], axis=0).astype(BF16)
        y = jnp.dot(t_ref[j], xs, preferred_element_type=F32)
        for cc in range(half):
            y_ref[cc, 0, sel, :] = y[:na, cc * LANES:(cc + 1) * LANES]
            y_ref[half + cc, 0, sel, :] = y[na:, cc * LANES:(cc + 1) * LANES]


def _dft2_kernel(y_ref, t_ref, o_ref):
    planes, groups, rows, _ = y_ref.shape
    step = SUBLANES_F32
    half, nb = planes // 2, groups * step

    def tokens_b(cc, j):
        return y_ref[cc, :, j * step:(j + 1) * step, :].reshape(nb, LANES)

    for j in range(rows // step):
        yr = jnp.concatenate([tokens_b(cc, j) for cc in range(half)], axis=1)
        yi = jnp.concatenate([tokens_b(cc, j) for cc in range(half, planes)], axis=1)
        xs = jnp.concatenate([yr, yi], axis=0).astype(BF16)
        out = jnp.dot(t_ref[...], xs, preferred_element_type=F32)
        for cc in range(half):
            o_ref[cc, 0, pl.ds(j, nb, stride=step), :] = out[:, cc * LANES:(cc + 1) * LANES]


def _position_dft(z):
    planes, groups, rows, _ = z.shape
    step = SUBLANES_F32
    nb, na = groups * step, rows // step
    t1, t2 = _dft_tables(na, nb)
    y = pl.pallas_call(
        _dft1_kernel,
        out_shape=jax.ShapeDtypeStruct(z.shape, F32),
        grid=(groups,),
        in_specs=[pl.BlockSpec((planes, 1, rows, LANES), lambda i: (0, i, 0, 0)),
                  pl.BlockSpec((step, 2 * na, 2 * na), lambda i: (i, 0, 0))],
        out_specs=pl.BlockSpec((planes, 1, rows, LANES), lambda i: (0, i, 0, 0)),
        compiler_params=_cparams("arbitrary"),
        name="position_dft_stage1",
    )(z, t1)
    return pl.pallas_call(
        _dft2_kernel,
        out_shape=jax.ShapeDtypeStruct((planes // 2, na // step, nb * step, LANES), F32),
        grid=(na // step,),
        in_specs=[pl.BlockSpec((planes, groups, step * step, LANES), lambda i: (0, 0, i, 0)),
                  pl.BlockSpec((nb, 2 * nb), lambda i: (0, 0))],
        out_specs=pl.BlockSpec((planes // 2, 1, nb * step, LANES), lambda i: (0, i, 0, 0)),
        compiler_params=_cparams("arbitrary"),
        name="position_dft_stage2",
    )(y, t2)


def _outproj_kernel(ax_ref, fx_ref, x_ref, wo_ref, gate_ref, nw_ref, sh_ref, sc_ref, wr_ref,
                    x1_ref, g_ref, aff_ref):
    planes, groups, rows, _ = fx_ref.shape
    na = groups * SUBLANES_F32
    n_kb = rows // SUBLANES_F32
    kb_per_chunk = max(1, n_kb // OUTPROJ_CHUNKS)
    for first_kb in range(0, n_kb, kb_per_chunk):
        rs = slice(first_kb * na, (first_kb + kb_per_chunk) * na)
        fx = jnp.concatenate(
            [jnp.concatenate(
                [fx_ref[cc, :, kb * SUBLANES_F32:(kb + 1) * SUBLANES_F32, :].reshape(na, LANES)
                 for kb in range(first_kb, first_kb + kb_per_chunk)], axis=0)
             for cc in range(planes)], axis=1).astype(BF16)
        acc = (jnp.dot(ax_ref[rs, :], wo_ref[:Q_DIM, :], preferred_element_type=F32)
               + jnp.dot(fx, wo_ref[Q_DIM:, :], preferred_element_type=F32))
        x1 = x_ref[rs, :] + gate_ref[...] * acc
        x1_ref[rs, :] = x1
        gb = _rms_mod(x1, nw_ref[...], sh_ref[...], sc_ref[...]).astype(BF16)
        g_ref[rs, :] = gb
        logits = jnp.dot(gb, wr_ref[...], preferred_element_type=F32)
        e = jnp.exp(logits - jnp.max(logits, axis=1, keepdims=True))
        aff_ref[rs, :] = e / jnp.sum(e, axis=1, keepdims=True)


def _output_projection(ax, fx, x, wo_b, gate, nw, shift, scale, wr_b, tm=512):
    n, d = x.shape
    ne = wr_b.shape[1]
    row = lambda i: (i, 0)
    const = lambda i: (0, 0)
    vec = pl.BlockSpec((1, d), const)
    return pl.pallas_call(
        _outproj_kernel,
        out_shape=(jax.ShapeDtypeStruct((n, d), F32), jax.ShapeDtypeStruct((n, d), BF16),
                   jax.ShapeDtypeStruct((n, ne), F32)),
        grid=(n // tm,),
        in_specs=[pl.BlockSpec((tm, Q_DIM), row),
                  pl.BlockSpec((fx.shape[0], fx.shape[1], tm // (fx.shape[1] * SUBLANES_F32)
                                * SUBLANES_F32, LANES), lambda i: (0, 0, i, 0)),
                  pl.BlockSpec((tm, d), row),
                  pl.BlockSpec((Q_DIM + F_DIM, d), const, pipeline_mode=pl.Buffered(1)),
                  vec, vec, vec, vec, pl.BlockSpec((d, ne), const)],
        out_specs=(pl.BlockSpec((tm, d), row), pl.BlockSpec((tm, d), row),
                   pl.BlockSpec((tm, ne), row)),
        compiler_params=_cparams("arbitrary"),
        name="output_projection",
    )(ax, fx, x, wo_b, gate, nw, shift, scale, wr_b)


def _route_kernel(aff_ref, posm_ref, cnt_ref, off_ref, *, cap, n_exp, n_chunk):
    aff_all = aff_ref[...]
    blocks = [aff_all[e * n_chunk:(e + 1) * n_chunk] for e in range(n_exp)]

    def total(mask):
        s = jnp.sum(jnp.where(mask, 1.0, 0.0), axis=0, keepdims=True)
        return jnp.sum(s, axis=1, keepdims=True)

    def search(it, thr_bits):
        bit = jnp.left_shift(jnp.int32(1), 30 - it)
        out = []
        for e in range(n_exp):
            cand = thr_bits[e] | bit
            enough = total(blocks[e] >= lax.bitcast_convert_type(cand, F32)) >= cap
            out.append(jnp.where(enough, cand, thr_bits[e]))
        return tuple(out)

    thr_bits = lax.fori_loop(0, 31, search, tuple(jnp.zeros((1, 1), I32) for _ in range(n_exp)))
    thr = [lax.bitcast_convert_type(t, F32) for t in thr_bits]

    rl = lax.broadcasted_iota(I32, (LANES, LANES), 0)
    cl = lax.broadcasted_iota(I32, (LANES, LANES), 1)
    before = jnp.where(rl < cl, 1.0, 0.0).astype(BF16)
    ones = jnp.ones((LANES, LANES), BF16)
    rc = lax.broadcasted_iota(I32, (n_chunk, n_chunk), 0)
    cc = lax.broadcasted_iota(I32, (n_chunk, n_chunk), 1)
    earlier = jnp.where(cc < rc, 1.0, 0.0).astype(BF16)

    def excl_cumsum(mask):
        mb = jnp.where(mask, 1.0, 0.0).astype(BF16)
        within = jnp.dot(mb, before, preferred_element_type=F32)
        rowtot = jnp.dot(mb, ones, preferred_element_type=F32)
        choff = jnp.dot(earlier, rowtot.astype(BF16), preferred_element_type=F32)
        return within + choff, rowtot, choff

    for e in range(n_exp):
        gt = blocks[e] > thr[e]
        eq = blocks[e] == thr[e]
        need = cap - total(gt)
        tie_rank, _, _ = excl_cumsum(eq)
        sel = gt | (eq & (tie_rank < need))
        pos, rowtot, choff = excl_cumsum(sel)
        rs = slice(e * n_chunk, (e + 1) * n_chunk)
        posm_ref[rs, :] = jnp.where(sel, pos.astype(I32), NOT_SELECTED)
        cnt_ref[rs, :] = rowtot.astype(I32)
        off_ref[rs, :] = choff.astype(I32)


def _route(aff2, cap, n_exp):
    rows = aff2.shape[0]
    n_chunk = rows // n_exp
    spec = pl.BlockSpec((rows, LANES), lambda i: (0, 0))
    shp = jax.ShapeDtypeStruct((rows, LANES), I32)
    return pl.pallas_call(
        functools.partial(_route_kernel, cap=cap, n_exp=n_exp, n_chunk=n_chunk),
        out_shape=(shp, shp, shp),
        grid=(1,),
        in_specs=[spec],
        out_specs=(spec, spec, spec),
        compiler_params=_cparams("arbitrary"),
        name="expert_choice_routing",
    )(aff2)


def _window_matches(posm, firsts, win_starts):
    r = lax.broadcasted_iota(I32, (WINDOW_ROWS, TOK_TILE), 0)
    out = []
    for e, (first, start) in enumerate(zip(firsts, win_starts)):
        pm = posm[e:e + 1, :]
        pm = jnp.where((pm >= first) & (pm < first + PIECE), pm, NOT_SELECTED)
        out.append((pm - start) == r)
    return out


def _align_down(v):
    return (v // ROW_ALIGN) * ROW_ALIGN


def _gather_kernel(off_ref, cnt_ref, g_ref, posm_ref, aff_ref, xs_hbm,
                   stg, tail, sem, npass_ref, *, n_exp, n_chunk, cap):
    c = pl.program_id(0)
    stride = cap + WINDOW_ROWS

    @pl.when(c == 0)
    def _():
        npass_ref[0] = 0
        tail[...] = jnp.zeros_like(tail)

    def window_copy(slot, e, dst):
        return pltpu.make_async_copy(stg.at[slot, pl.ds(e * WINDOW_ROWS, WINDOW_ROWS)],
                                     xs_hbm.at[pl.ds(dst, WINDOW_ROWS)], sem.at[slot])

    def wait_slot(slot):
        for e in range(n_exp):
            window_copy(slot, e, 0).wait()

    @pl.when(c == 0)
    def _():
        stg[0] = jnp.zeros(stg.shape[1:], stg.dtype)
        for e in range(n_exp):
            window_copy(0, e, e * stride + cap).start()
        wait_slot(0)

    most = cnt_ref[c]
    for e in range(1, n_exp):
        most = jnp.maximum(most, cnt_ref[e * n_chunk + c])
    passes = (most + PIECE - 1) // PIECE

    def one_pass(p, carry):
        done = npass_ref[0]
        slot = done % 2
        posm = posm_ref[0]
        aff = aff_ref[0]
        firsts, win_starts, next_shift = [], [], []
        for e in range(n_exp):
            o, n_e = off_ref[e * n_chunk + c], cnt_ref[e * n_chunk + c]
            first = o + jnp.minimum(p * PIECE, n_e)
            after = o + jnp.minimum((p + 1) * PIECE, n_e)
            firsts.append(first)
            win_starts.append(_align_down(first))
            next_shift.append(_align_down(after) - _align_down(first))
        matches = _window_matches(posm, firsts, win_starts)
        onehot = jnp.concatenate([jnp.where(mt, 1.0, 0.0).astype(BF16) for mt in matches], axis=0)
        new = jnp.dot(onehot, g_ref[...], preferred_element_type=F32)
        lane = lax.broadcasted_iota(I32, (WINDOW_ROWS, GATE_LANES), 1)
        for e in range(n_exp):
            lo = e * WINDOW_ROWS
            gate = jnp.sum(jnp.where(matches[e], aff[e:e + 1, :], 0.0), axis=1, keepdims=True)
            hi = gate.astype(BF16).astype(F32)
            mid = (gate - hi).astype(BF16).astype(F32)
            low = gate - hi - mid
            parts = jnp.where(lane == 0, hi, jnp.where(lane == 1, mid,
                                                       jnp.where(lane == 2, low, 0.0)))
            rows = jnp.concatenate([new[lo:lo + WINDOW_ROWS], parts], axis=1)
            stg[slot, lo:lo + ROW_ALIGN] = (rows[:ROW_ALIGN] + tail[e].astype(F32)).astype(BF16)
            stg[slot, lo + ROW_ALIGN:lo + WINDOW_ROWS] = rows[ROW_ALIGN:].astype(BF16)

        @pl.when(done > 0)
        def _():
            wait_slot(1 - slot)

        for e in range(n_exp):
            window_copy(slot, e, pl.multiple_of(e * stride + win_starts[e], ROW_ALIGN)).start()
        for e in range(n_exp):
            src = pl.ds(pl.multiple_of(e * WINDOW_ROWS + next_shift[e], ROW_ALIGN), ROW_ALIGN)
            tail[e] = stg[slot, src, :]
        npass_ref[0] = done + 1
        return carry

    lax.fori_loop(0, passes, one_pass, 0)

    @pl.when(c == pl.num_programs(0) - 1)
    def _():
        done = npass_ref[0]

        @pl.when(done > 0)
        def _():
            wait_slot((done - 1) % 2)


def _gather(off, cnt, g, posm_t, aff_t, cap):
    n, d = g.shape
    n_chunk, n_exp, _ = posm_t.shape
    rows = n_exp * (cap + WINDOW_ROWS)
    tile3 = lambda i, o, c: (i, 0, 0)
    return pl.pallas_call(
        functools.partial(_gather_kernel, n_exp=n_exp, n_chunk=n_chunk, cap=cap),
        out_shape=jax.ShapeDtypeStruct((rows, d + GATE_LANES), BF16),
        grid_spec=pltpu.PrefetchScalarGridSpec(
            num_scalar_prefetch=2,
            grid=(n_chunk,),
            in_specs=[pl.BlockSpec((TOK_TILE, d), lambda i, o, c: (i, 0)),
                      pl.BlockSpec((1, n_exp, TOK_TILE), tile3),
                      pl.BlockSpec((1, n_exp, TOK_TILE), tile3)],
            out_specs=pl.BlockSpec(memory_space=pl.ANY),
            scratch_shapes=[pltpu.VMEM((2, n_exp * WINDOW_ROWS, d + GATE_LANES), BF16),
                            pltpu.VMEM((n_exp, ROW_ALIGN, d + GATE_LANES), BF16),
                            pltpu.SemaphoreType.DMA((2,)),
                            pltpu.SMEM((1,), I32)]),
        compiler_params=_cparams("arbitrary"),
        name="gather_expert_tokens",
    )(off, cnt, g, posm_t, aff_t)


def _expert_kernel(xs_ref, wg_ref, wu_ref, wd_ref, y_ref, h_ref, *, nf, tf, tn):
    j = pl.program_id(1)
    d = wg_ref.shape[1]

    @pl.when(j < nf)
    def _():
        xs = xs_ref[0, :, :d]
        gp = jnp.dot(xs, wg_ref[0].astype(BF16), preferred_element_type=F32)
        up = jnp.dot(xs, wu_ref[0].astype(BF16), preferred_element_type=F32)
        h = ((gp * jax.nn.sigmoid(gp)) * up).astype(BF16)
        for jj in range(nf):
            @pl.when(j == jj)
            def _():
                h_ref[:, jj * tf:(jj + 1) * tf] = h

    @pl.when(j >= nf)
    def _():
        y = jnp.dot(h_ref[...], wd_ref[0].astype(BF16), preferred_element_type=F32)
        parts = xs_ref[0, :, d:].astype(F32)
        gate = parts[:, 0:1] + parts[:, 1:2] + parts[:, 2:3]
        y_ref[0] = (y * gate).astype(BF16)


def _experts(xs3, w_gate, w_up, w_down, cap, tf=256, tn=512):
    n_exp, d, ff = w_gate.shape
    tf, tn = min(tf, ff), min(tn, d)
    nf, nn = ff // tf, d // tn
    return pl.pallas_call(
        functools.partial(_expert_kernel, nf=nf, tf=tf, tn=tn),
        out_shape=jax.ShapeDtypeStruct((n_exp, cap, d), BF16),
        grid=(n_exp, nf + nn),
        in_specs=[pl.BlockSpec((1, cap, d + GATE_LANES), lambda e, j: (e, 0, 0)),
                  pl.BlockSpec((1, d, tf), lambda e, j: (e, 0, jnp.minimum(j, nf - 1))),
                  pl.BlockSpec((1, d, tf), lambda e, j: (e, 0, jnp.minimum(j, nf - 1))),
                  pl.BlockSpec((1, ff, tn), lambda e, j: (e, 0, jnp.maximum(j - nf, 0)))],
        out_specs=pl.BlockSpec((1, cap, tn), lambda e, j: (e, 0, jnp.maximum(j - nf, 0))),
        scratch_shapes=[pltpu.VMEM((cap, ff), BF16)],
        compiler_params=_cparams("arbitrary", "arbitrary"),
        name="swiglu_experts",
    )(xs3, w_gate, w_up, w_down)


def _combine_kernel(off_ref, cnt_ref, y_hbm, posm_ref, x1_ref, gate_ref, nw_ref, o_ref,
                    stg, acc_ref, sem, *, n_exp, n_chunk, cap):
    c = pl.program_id(0)
    last_start = n_exp * cap - WINDOW_ROWS

    def firsts_of(tile, p):
        return [off_ref[e * n_chunk + tile] + p * PIECE for e in range(n_exp)]

    def starts_of(tile, p):
        return [pl.multiple_of(jnp.minimum(e * cap + _align_down(first), last_start), ROW_ALIGN)
                for e, first in enumerate(firsts_of(tile, p))]

    def piece_copy(slot, e, src):
        return pltpu.make_async_copy(y_hbm.at[pl.ds(src, WINDOW_ROWS)],
                                     stg.at[slot, pl.ds(e * WINDOW_ROWS, WINDOW_ROWS)],
                                     sem.at[slot])

    def fetch(tile, p, slot):
        for e, src in enumerate(starts_of(tile, p)):
            piece_copy(slot, e, src).start()

    def wait_slot(slot):
        for e in range(n_exp):
            piece_copy(slot, e, 0).wait()

    def contribution(p, slot):
        starts = [s - e * cap for e, s in enumerate(starts_of(c, p))]
        matches = _window_matches(posm_ref[0], firsts_of(c, p), starts)
        onehot = jnp.concatenate([jnp.where(mt, 1.0, 0.0).astype(BF16) for mt in matches], axis=0)
        return lax.dot_general(onehot, stg[slot], (((0,), (0,)), ((), ())),
                               preferred_element_type=F32)

    slot = c % 2

    @pl.when(c == 0)
    def _():
        fetch(0, 0, 0)

    @pl.when(c + 1 < pl.num_programs(0))
    def _():
        fetch(c + 1, 0, 1 - slot)

    wait_slot(slot)
    acc_ref[...] = contribution(0, slot)

    most = cnt_ref[c]
    for e in range(1, n_exp):
        most = jnp.maximum(most, cnt_ref[e * n_chunk + c])
    passes = (most + PIECE - 1) // PIECE

    def extra_pass(p, carry):
        fetch(c, p, slot)
        wait_slot(slot)
        acc_ref[...] += contribution(p, slot)
        return carry

    lax.fori_loop(1, passes, extra_pass, 0)
    x2 = x1_ref[...] + gate_ref[...] * acc_ref[...]
    ms = jnp.mean(x2 * x2, axis=-1, keepdims=True)
    o_ref[...] = x2 * lax.rsqrt(ms + RMS_EPS) * nw_ref[...]


def _combine(off, cnt, y, posm_t, x1, gate, nw, cap):
    n, d = x1.shape
    n_chunk, n_exp, _ = posm_t.shape
    return pl.pallas_call(
        functools.partial(_combine_kernel, n_exp=n_exp, n_chunk=n_chunk, cap=cap),
        out_shape=jax.ShapeDtypeStruct((n, d), F32),
        grid_spec=pltpu.PrefetchScalarGridSpec(
            num_scalar_prefetch=2,
            grid=(n_chunk,),
            in_specs=[pl.BlockSpec(memory_space=pl.ANY),
                      pl.BlockSpec((1, n_exp, TOK_TILE), lambda i, o, c: (i, 0, 0)),
                      pl.BlockSpec((TOK_TILE, d), lambda i, o, c: (i, 0)),
                      pl.BlockSpec((1, d), lambda i, o, c: (0, 0)),
                      pl.BlockSpec((1, d), lambda i, o, c: (0, 0))],
            out_specs=pl.BlockSpec((TOK_TILE, d), lambda i, o, c: (i, 0)),
            scratch_shapes=[pltpu.VMEM((2, n_exp * WINDOW_ROWS, d), BF16),
                            pltpu.VMEM((TOK_TILE, d), F32),
                            pltpu.SemaphoreType.DMA((2,))]),
        compiler_params=_cparams("arbitrary"),
        name="combine_experts_final_norm",
    )(off, cnt, y, posm_t, x1, gate, nw)


def kernel(x, c, ctx, c_ctx, w_mod, b_mod, norm_mix, w_in, sink, w_fourier, w_out, norm_ffn,
           w_router, w_gate, w_up, w_down, norm_final):
    assert x.shape[0] == 1 and w_mod.shape[0] == 1
    n, d = x.shape[1], x.shape[2]
    xl, cx = x[0], ctx[0]
    n_exp = w_router.shape[2]
    cap = max(1, CAP_FACTOR * n // n_exp)
    n_chunk = n // LANES
    n_tile = n // TOK_TILE
    per_tile = TOK_TILE // LANES

    mod = _modulation(jnp.stack([c[0], c_ctx], axis=1), w_mod[0], b_mod[0][None, :])
    mx = mod[0].reshape(6, 1, d)
    mc = mod[1].reshape(6, 1, d)

    w_in_b = w_in[0].astype(BF16)
    nmix = norm_mix[0][None, :]
    ab = _fold_channel_dft(w_fourier[0])
    q, k, v, z = _project_latent(xl, nmix, mx[0], mx[1], w_in_b, ab)
    kc, vc = _project_context(cx, nmix, mc[0], mc[1], w_in_b[:, Q_DIM:Q_DIM + 2 * KV_DIM])
    ax = _attention(q, k, v, kc, vc, sink[0])
    fx = _position_dft(z)

    x1, g, aff = _output_projection(ax, fx, xl, w_out[0].astype(BF16), mx[2], norm_ffn[0][None, :],
                                    mx[3], mx[4], w_router[0].astype(BF16))

    aff2 = aff.T.reshape(n_exp * n_chunk, LANES)
    posm, cnt, off = _route(aff2, cap, n_exp)
    by_tile = lambda a: a.reshape(n_exp, n_tile, TOK_TILE).transpose(1, 0, 2)
    posm_t, aff_t = by_tile(posm), by_tile(aff2)
    cnt1 = cnt[:, 0].reshape(n_exp * n_tile, per_tile).sum(axis=1)
    off1 = off[:, 0].reshape(n_exp * n_tile, per_tile)[:, 0]

    xs = _gather(off1, cnt1, g, posm_t, aff_t, cap)
    y = _experts(xs.reshape(n_exp, cap + WINDOW_ROWS, d + GATE_LANES),
                 w_gate[0], w_up[0], w_down[0], cap)
    out = _combine(off1, cnt1, y.reshape(n_exp * cap, d), posm_t, x1, mx[5], norm_final[None, :],
                   cap)
    return out[None]
```

```python
import functools
import math

import jax
import jax.numpy as jnp
from jax import lax
from jax.experimental import pallas as pl
from jax.experimental.pallas import tpu as pltpu

F32 = jnp.float32
BF16 = jnp.bfloat16
I32 = jnp.int32

HEAD_DIM = 128
N_HEADS = 8
N_KV_HEADS = 2
GROUP = N_HEADS // N_KV_HEADS
WINDOW = 128
Q_DIM = N_HEADS * HEAD_DIM
KV_DIM = N_KV_HEADS * HEAD_DIM
N_FGROUPS = 4
FG = 256
F_DIM = N_FGROUPS * FG
D_IN = Q_DIM + 2 * KV_DIM + F_DIM
N_EXPERTS = 16
CAP_FACTOR = 2
GRID_W = 64
ROPE_THETA = 10000.0
RMS_EPS = 1e-6
NEG_INF = -1e30
LOG2_E = math.log2(math.e)

LANES = 128
SUBLANES_F32 = 8
VMEM_LIMIT_BYTES = 56 * 1024 * 1024

TOK_TILE = 2 * LANES
PIECE = 48
ROW_ALIGN = 16
WINDOW_ROWS = ROW_ALIGN + PIECE
GATE_LANES = LANES
DFT_NB = LANES
OUTPROJ_CHUNKS = 2
PROJ_CHUNKS = 2
EXPERT_ROW_SPLIT = 2
NOT_SELECTED = -(1 << 20)


def _cparams(*sem):
    return pltpu.CompilerParams(dimension_semantics=sem, vmem_limit_bytes=VMEM_LIMIT_BYTES)


def _rms_mod(x, nw, shift, scale):
    ms = jnp.mean(x * x, axis=-1, keepdims=True)
    h = x * lax.rsqrt(ms + RMS_EPS) * nw
    return h * (1.0 + scale) + shift


def _mod_kernel(ct_ref, w_ref, b_ref, o_ref):
    ct = ct_ref[...]
    s = ct * jax.nn.sigmoid(ct)
    w = w_ref[...]
    r0 = jnp.sum(s[:, 0:1] * w, axis=0, keepdims=True)
    r1 = jnp.sum(s[:, 1:2] * w, axis=0, keepdims=True)
    o_ref[...] = jnp.concatenate([r0, r1], axis=0) + b_ref[...]


def _modulation(ct, w_mod, b_mod):
    d, n = w_mod.shape
    tn = math.gcd(n, 1024)
    return pl.pallas_call(
        _mod_kernel,
        out_shape=jax.ShapeDtypeStruct((2, n), F32),
        grid=(n // tn,),
        in_specs=[pl.BlockSpec((d, 2), lambda i: (0, 0)),
                  pl.BlockSpec((d, tn), lambda i: (0, i)),
                  pl.BlockSpec((1, tn), lambda i: (0, i))],
        out_specs=pl.BlockSpec((2, tn), lambda i: (0, i)),
        compiler_params=_cparams("arbitrary"),
        name="modulation",
    )(ct, w_mod, b_mod)


def _ab_kernel(cf_ref, sf_ref, wf_ref, ab_ref):
    wf = wf_ref[0]
    a = jnp.dot(cf_ref[...], wf, preferred_element_type=F32, precision=lax.Precision.HIGHEST)
    b = jnp.dot(sf_ref[...], wf, preferred_element_type=F32, precision=lax.Precision.HIGHEST)
    ab_ref[0] = jnp.concatenate([a, b], axis=1).astype(BF16)


def _fold_channel_dft(w_fourier):
    g, fg, _ = w_fourier.shape
    idx = jnp.arange(fg, dtype=I32)
    th = ((idx[:, None] * idx[None, :]) % fg).astype(F32) * (2.0 * math.pi / fg)
    scale = 1.0 / math.sqrt(fg)
    cf = jnp.cos(th) * scale
    sf = jnp.sin(th) * scale
    return pl.pallas_call(
        _ab_kernel,
        out_shape=jax.ShapeDtypeStruct((g, fg, 2 * fg), BF16),
        grid=(g,),
        in_specs=[pl.BlockSpec((fg, fg), lambda i: (0, 0)),
                  pl.BlockSpec((fg, fg), lambda i: (0, 0)),
                  pl.BlockSpec((1, fg, fg), lambda i: (i, 0, 0))],
        out_specs=pl.BlockSpec((1, fg, 2 * fg), lambda i: (i, 0, 0)),
        compiler_params=_cparams("arbitrary"),
        name="fold_channel_dft",
    )(cf, sf, w_fourier)


def _rope_tables(n):
    quarter = HEAD_DIM // 4
    freqs = ROPE_THETA ** (-jnp.arange(quarter, dtype=F32) / quarter)
    zeros = lambda m: jnp.zeros((m, 2 * quarter), F32)

    def tables(pos, low_half):
        ang = pos[:, None] * freqs[None, :]
        ang = jnp.concatenate([ang, ang], axis=-1)
        cos, sin = jnp.cos(ang), jnp.sin(ang)
        first = (jnp.arange(2 * quarter) < quarter)[None, :]
        parts = (cos, jnp.where(first, -sin, 0.0), jnp.where(first, 0.0, sin))
        pad = zeros(pos.shape[0])
        return jnp.stack([jnp.concatenate([p, pad] if low_half else [pad, p], axis=-1)
                          for p in parts])

    row_tab = tables(jnp.arange(n // GRID_W, dtype=F32), True)
    col_tab = tables(jnp.arange(GRID_W, dtype=F32), False)
    return row_tab, col_tab


def _store_dft_rows(z_ref, plane, val, a0):
    groups = z_ref.shape[1]
    nb = groups * SUBLANES_F32
    for al in range(val.shape[0] // nb):
        z_ref[plane, :, (a0 + al) * SUBLANES_F32:(a0 + al + 1) * SUBLANES_F32, :] = (
            val[al * nb:(al + 1) * nb].reshape(groups, SUBLANES_F32, LANES))


def _proj_kernel(x_ref, nw_ref, sh_ref, sc_ref, w_ref, ab_ref, rt_ref, ct_ref,
                 q_ref, k_ref, v_ref, z_ref):
    quarter = HEAD_DIM // 4
    qscale = LOG2_E / math.sqrt(HEAD_DIM)
    tm = x_ref.shape[0]
    chunk = tm // PROJ_CHUNKS
    for r0 in range(0, tm, chunk):
        rs = slice(r0, r0 + chunk)
        hb = _rms_mod(x_ref[rs, :], nw_ref[...], sh_ref[...], sc_ref[...]).astype(BF16)
        cos, s_up, s_dn = (
            jnp.concatenate(
                [rt_ref[t, r:r + 1, :] + ct_ref[t]
                 for r in range(r0 // GRID_W, (r0 + chunk) // GRID_W)], axis=0)
            for t in range(3))

        def rope(t):
            return (t * cos + pltpu.roll(t, HEAD_DIM - quarter, 1) * s_up
                    + pltpu.roll(t, quarter, 1) * s_dn)

        q = jnp.dot(hb, w_ref[:, :Q_DIM], preferred_element_type=F32)
        for j in range(N_HEADS):
            sl = slice(j * HEAD_DIM, (j + 1) * HEAD_DIM)
            q_ref[rs, sl] = (rope(q[:, sl]) * qscale).astype(BF16)
        k = jnp.dot(hb, w_ref[:, Q_DIM:Q_DIM + KV_DIM], preferred_element_type=F32)
        for j in range(N_KV_HEADS):
            sl = slice(j * HEAD_DIM, (j + 1) * HEAD_DIM)
            k_ref[rs, sl] = rope(k[:, sl]).astype(BF16)
        v = jnp.dot(hb, w_ref[:, Q_DIM + KV_DIM:Q_DIM + 2 * KV_DIM], preferred_element_type=F32)
        v_ref[rs, :] = v.astype(BF16)
        u = jnp.dot(hb, w_ref[:, Q_DIM + 2 * KV_DIM:], preferred_element_type=F32).astype(BF16)
        for g in range(N_FGROUPS):
            pq = jnp.dot(u[:, g * FG:(g + 1) * FG], ab_ref[g], preferred_element_type=F32)
            per_group = FG // LANES
            for t in range(2 * per_group):
                plane = (t // per_group) * (F_DIM // LANES) + g * per_group + t % per_group
                _store_dft_rows(z_ref, plane, pq[:, t * LANES:(t + 1) * LANES], r0 // DFT_NB)


def _project_latent(x, nw, shift, scale, w_in_b, ab, tm=512):
    n, d = x.shape
    row_tab, col_tab = _rope_tables(n)
    row = lambda i: (i, 0)
    const2 = lambda i: (0, 0)
    return pl.pallas_call(
        _proj_kernel,
        out_shape=(jax.ShapeDtypeStruct((n, Q_DIM), BF16),
                   jax.ShapeDtypeStruct((n, KV_DIM), BF16),
                   jax.ShapeDtypeStruct((n, KV_DIM), BF16),
                   jax.ShapeDtypeStruct((2 * F_DIM // LANES, DFT_NB // SUBLANES_F32,
                                         (n // DFT_NB) * SUBLANES_F32, LANES), F32)),
        grid=(n // tm,),
        in_specs=[pl.BlockSpec((tm, d), row),
                  pl.BlockSpec((1, d), const2), pl.BlockSpec((1, d), const2),
                  pl.BlockSpec((1, d), const2),
                  pl.BlockSpec((d, D_IN), const2, pipeline_mode=pl.Buffered(1)),
                  pl.BlockSpec((N_FGROUPS, FG, 2 * FG), lambda i: (0, 0, 0),
                               pipeline_mode=pl.Buffered(1)),
                  pl.BlockSpec((3, tm // GRID_W, HEAD_DIM), lambda i: (0, i, 0)),
                  pl.BlockSpec((3, GRID_W, HEAD_DIM), lambda i: (0, 0, 0))],
        out_specs=(pl.BlockSpec((tm, Q_DIM), row), pl.BlockSpec((tm, KV_DIM), row),
                   pl.BlockSpec((tm, KV_DIM), row),
                   pl.BlockSpec((2 * F_DIM // LANES, DFT_NB // SUBLANES_F32,
                                 (tm // DFT_NB) * SUBLANES_F32, LANES), lambda i: (0, 0, i, 0))),
        compiler_params=_cparams("arbitrary"),
        name="project_latent",
    )(x, nw, shift, scale, w_in_b, ab, row_tab, col_tab)


def _ctx_kernel(x_ref, nw_ref, sh_ref, sc_ref, w_ref, k_ref, v_ref):
    hb = _rms_mod(x_ref[...], nw_ref[...], sh_ref[...], sc_ref[...]).astype(BF16)
    kv = jnp.dot(hb, w_ref[...], preferred_element_type=F32)
    k_ref[...] = kv[:, :KV_DIM].astype(BF16)
    v_ref[...] = kv[:, KV_DIM:].astype(BF16)


def _project_context(ctx, nw, shift, scale, w_kv_b):
    m, d = ctx.shape
    full = lambda shp: pl.BlockSpec(shp, lambda i: (0, 0))
    return pl.pallas_call(
        _ctx_kernel,
        out_shape=(jax.ShapeDtypeStruct((m, KV_DIM), BF16),
                   jax.ShapeDtypeStruct((m, KV_DIM), BF16)),
        grid=(1,),
        in_specs=[full((m, d)), full((1, d)), full((1, d)), full((1, d)), full((d, 2 * KV_DIM))],
        out_specs=(full((m, KV_DIM)), full((m, KV_DIM))),
        compiler_params=_cparams("arbitrary"),
        name="project_context",
    )(ctx, nw, shift, scale, w_kv_b)


def _attn_kernel(sink_ref, q_ref, kp_ref, km_ref, kn_ref, vp_ref, vm_ref, vn_ref,
                 kc_ref, vc_ref, lo_ref, hi_ref, o_ref, *, n_total, tq):
    i = pl.program_id(0)
    nsub = tq // WINDOW
    last_blk = n_total // WINDOW - 1
    kwin = jnp.concatenate([kp_ref[...], km_ref[...], kn_ref[...]], axis=0)
    vwin = jnp.concatenate([vp_ref[...], vm_ref[...], vn_ref[...]], axis=0)
    rows = GROUP * WINDOW
    span = 3 * WINDOW
    head_of_row = lax.broadcasted_iota(I32, (rows, 1), 0) // WINDOW
    ones_loc = jnp.ones((span, HEAD_DIM), BF16)
    ones_ctx = jnp.ones((kc_ref.shape[0], HEAD_DIM), BF16)
    nt = (((1,), (1,)), ((), ()))
    for b in range(nsub):
        blk = i * nsub + b
        bias_lo = lo_ref[(blk == 0).astype(I32)]
        bias_hi = hi_ref[(blk == last_blk).astype(I32)]
        for h in range(N_KV_HEADS):
            hs = slice(h * HEAD_DIM, (h + 1) * HEAD_DIM)
            qs = jnp.concatenate(
                [q_ref[b * WINDOW:(b + 1) * WINDOW,
                       (h * GROUP + g) * HEAD_DIM:(h * GROUP + g + 1) * HEAD_DIM]
                 for g in range(GROUP)], axis=0)
            kw = kwin[b * WINDOW:b * WINDOW + span, hs]
            vw = vwin[b * WINDOW:b * WINDOW + span, hs]
            s_loc = lax.dot_general(qs, kw, nt, preferred_element_type=F32)
            s_ctx = lax.dot_general(qs, kc_ref[:, hs], nt, preferred_element_type=F32)
            parts = [s_loc[:, :WINDOW] + bias_lo, s_loc[:, WINDOW:2 * WINDOW],
                     s_loc[:, 2 * WINDOW:] + bias_hi, s_ctx]
            sink_col = jnp.zeros((rows, 1), F32)
            for g in range(GROUP):
                sink_col = jnp.where(head_of_row == g, sink_ref[h * GROUP + g] * LOG2_E, sink_col)
            blocks = parts[:3] + [s_ctx[:, t * WINDOW:(t + 1) * WINDOW]
                                  for t in range(s_ctx.shape[1] // WINDOW)]
            widest = blocks[0]
            for blk_scores in blocks[1:]:
                widest = jnp.maximum(widest, blk_scores)
            m = jnp.maximum(sink_col, jnp.max(widest, axis=1, keepdims=True))
            p = [jnp.exp2(part - m).astype(BF16) for part in parts]
            ov = (jnp.dot(jnp.concatenate(p[:3], axis=1), jnp.concatenate([vw, ones_loc], axis=1),
                          preferred_element_type=F32)
                  + jnp.dot(p[3], jnp.concatenate([vc_ref[:, hs], ones_ctx], axis=1),
                            preferred_element_type=F32))
            o = ov[:, :HEAD_DIM] / (ov[:, HEAD_DIM:] + jnp.exp2(sink_col - m))
            for g in range(GROUP):
                o_ref[b * WINDOW:(b + 1) * WINDOW,
                      (h * GROUP + g) * HEAD_DIM:(h * GROUP + g + 1) * HEAD_DIM] = (
                    o[g * WINDOW:(g + 1) * WINDOW].astype(BF16))


def _band_biases():
    r = jnp.arange(GROUP * WINDOW, dtype=I32)[:, None] % WINDOW
    c = jnp.arange(WINDOW, dtype=I32)[None, :]
    masked = jnp.full((GROUP * WINDOW, WINDOW), NEG_INF, F32)
    lo = jnp.stack([jnp.where(c >= r, 0.0, NEG_INF).astype(F32), masked])
    hi = jnp.stack([jnp.where(c <= r, 0.0, NEG_INF).astype(F32), masked])
    return lo, hi


def _attention(q, k, v, kc, vc, sink, tq=512):
    n = q.shape[0]
    bias_lo, bias_hi = _band_biases()
    bias_spec = pl.BlockSpec(bias_lo.shape, lambda i, s: (0, 0, 0))
    m = kc.shape[0]
    nsub = tq // WINDOW
    nblk = n // WINDOW
    prev = lambda i, s: (jnp.maximum(i * nsub - 1, 0), 0)
    main = lambda i, s: (i, 0)
    nxt = lambda i, s: (jnp.minimum(i * nsub + nsub, nblk - 1), 0)
    const = lambda i, s: (0, 0)
    kv_specs = [pl.BlockSpec((WINDOW, KV_DIM), prev), pl.BlockSpec((tq, KV_DIM), main),
                pl.BlockSpec((WINDOW, KV_DIM), nxt)]
    return pl.pallas_call(
        functools.partial(_attn_kernel, n_total=n, tq=tq),
        out_shape=jax.ShapeDtypeStruct((n, Q_DIM), BF16),
        grid_spec=pltpu.PrefetchScalarGridSpec(
            num_scalar_prefetch=1,
            grid=(n // tq,),
            in_specs=[pl.BlockSpec((tq, Q_DIM), main)] + kv_specs + kv_specs
                     + [pl.BlockSpec((m, KV_DIM), const), pl.BlockSpec((m, KV_DIM), const),
                        bias_spec, bias_spec],
            out_specs=pl.BlockSpec((tq, Q_DIM), main)),
        compiler_params=_cparams("arbitrary"),
        name="banded_attention",
    )(sink, q, k, k, k, v, v, v, kc, vc, bias_lo, bias_hi)


def _dft_tables(na, nb):
    n = na * nb
    s1 = 2.0 ** (-(int(math.log2(na)) // 2))
    s2 = (1.0 / math.sqrt(n)) / s1
    ka = jnp.arange(na, dtype=I32)
    th_tw = ((jnp.arange(nb, dtype=I32)[:, None] * ka[None, :]) % n).astype(F32) * (2.0 * math.pi / n)
    th_f = ((ka[:, None] * ka[None, :]) % na).astype(F32) * (2.0 * math.pi / na)
    twr, twi = (jnp.cos(th_tw) * s1)[:, :, None], (-jnp.sin(th_tw) * s1)[:, :, None]
    fr, fi = jnp.cos(th_f)[None, :, :], -jnp.sin(th_f)[None, :, :]
    tr, ti = twr * fr - twi * fi, twr * fi + twi * fr
    t1 = jnp.concatenate([jnp.concatenate([tr, ti], axis=-1),
                          jnp.concatenate([ti, -tr], axis=-1)], axis=-2).astype(BF16)
    kb = jnp.arange(nb, dtype=I32)
    th2 = ((kb[:, None] * kb[None, :]) % nb).astype(F32) * (2.0 * math.pi / nb)
    t2 = (jnp.concatenate([jnp.cos(th2), jnp.sin(th2)], axis=-1) * s2).astype(BF16)
    return t1, t2


def _dft1_kernel(z_ref, t_ref, y_ref):
    planes, _, rows, _ = z_ref.shape
    step = SUBLANES_F32
    half, na = planes // 2, rows // step
    for j in range(step):
        sel = pl.ds(j, na, stride=step)
        p = jnp.concatenate([z_ref[cc, 0, sel, :] for cc in range(half)], axis=1)
        q = jnp.concatenate([z_ref[cc, 0, sel, :] for cc in range(half, planes)], axis=1)
        xs = jnp.concatenate([p, q], axis=0).astype(BF16)
        y = jnp.dot(t_ref[j], xs, preferred_element_type=F32)
        for cc in range(half):
            y_ref[cc, 0, sel, :] = y[:na, cc * LANES:(cc + 1) * LANES]
            y_ref[half + cc, 0, sel, :] = y[na:, cc * LANES:(cc + 1) * LANES]


def _dft2_kernel(y_ref, t_ref, o_ref):
    planes, groups, rows, _ = y_ref.shape
    step = SUBLANES_F32
    half, nb = planes // 2, groups * step

    def tokens_b(cc, j):
        return y_ref[cc, :, j * step:(j + 1) * step, :].reshape(nb, LANES)

    for j in range(rows // step):
        yr = jnp.concatenate([tokens_b(cc, j) for cc in range(half)], axis=1)
        yi = jnp.concatenate([tokens_b(cc, j) for cc in range(half, planes)], axis=1)
        xs = jnp.concatenate([yr, yi], axis=0).astype(BF16)
        out = jnp.dot(t_ref[...], xs, preferred_element_type=F32)
        for cc in range(half):
            o_ref[cc, 0, pl.ds(j, nb, stride=step), :] = out[:, cc * LANES:(cc + 1) * LANES]


def _position_dft(z):
    planes, groups, rows, _ = z.shape
    step = SUBLANES_F32
    nb, na = groups * step, rows // step
    t1, t2 = _dft_tables(na, nb)
    y = pl.pallas_call(
        _dft1_kernel,
        out_shape=jax.ShapeDtypeStruct(z.shape, F32),
        grid=(groups,),
        in_specs=[pl.BlockSpec((planes, 1, rows, LANES), lambda i: (0, i, 0, 0)),
                  pl.BlockSpec((step, 2 * na, 2 * na), lambda i: (i, 0, 0))],
        out_specs=pl.BlockSpec((planes, 1, rows, LANES), lambda i: (0, i, 0, 0)),
        compiler_params=_cparams("arbitrary"),
        name="position_dft_stage1",
    )(z, t1)
    return pl.pallas_call(
        _dft2_kernel,
        out_shape=jax.ShapeDtypeStruct((planes // 2, na // step, nb * step, LANES), F32),
        grid=(na // step,),
        in_specs=[pl.BlockSpec((planes, groups, step * step, LANES), lambda i: (0, 0, i, 0)),
                  pl.BlockSpec((nb, 2 * nb), lambda i: (0, 0))],
        out_specs=pl.BlockSpec((planes // 2, 1, nb * step, LANES), lambda i: (0, i, 0, 0)),
        compiler_params=_cparams("arbitrary"),
        name="position_dft_stage2",
    )(y, t2)


def _outproj_kernel(ax_ref, fx_ref, x_ref, wo_ref, gate_ref, nw_ref, sh_ref, sc_ref, wr_ref,
                    x1_ref, g_ref, aff_ref):
    planes, groups, rows, _ = fx_ref.shape
    na = groups * SUBLANES_F32
    n_kb = rows // SUBLANES_F32
    kb_per_chunk = max(1, n_kb // OUTPROJ_CHUNKS)
    for first_kb in range(0, n_kb, kb_per_chunk):
        rs = slice(first_kb * na, (first_kb + kb_per_chunk) * na)
        fx = jnp.concatenate(
            [jnp.concatenate(
                [fx_ref[cc, :, kb * SUBLANES_F32:(kb + 1) * SUBLANES_F32, :].reshape(na, LANES)
                 for kb in range(first_kb, first_kb + kb_per_chunk)], axis=0)
             for cc in range(planes)], axis=1).astype(BF16)
        acc = (jnp.dot(ax_ref[rs, :], wo_ref[:Q_DIM, :], preferred_element_type=F32)
               + jnp.dot(fx, wo_ref[Q_DIM:, :], preferred_element_type=F32))
        x1 = x_ref[rs, :] + gate_ref[...] * acc
        x1_ref[rs, :] = x1
        gb = _rms_mod(x1, nw_ref[...], sh_ref[...], sc_ref[...]).astype(BF16)
        g_ref[rs, :] = gb
        logits = jnp.dot(gb, wr_ref[...], preferred_element_type=F32)
        e = jnp.exp(logits - jnp.max(logits, axis=1, keepdims=True))
        aff_ref[rs, :] = e / jnp.sum(e, axis=1, keepdims=True)


def _output_projection(ax, fx, x, wo_b, gate, nw, shift, scale, wr_b, tm=512):
    n, d = x.shape
    ne = wr_b.shape[1]
    row = lambda i: (i, 0)
    const = lambda i: (0, 0)
    vec = pl.BlockSpec((1, d), const)
    return pl.pallas_call(
        _outproj_kernel,
        out_shape=(jax.ShapeDtypeStruct((n, d), F32), jax.ShapeDtypeStruct((n, d), BF16),
                   jax.ShapeDtypeStruct((n, ne), F32)),
        grid=(n // tm,),
        in_specs=[pl.BlockSpec((tm, Q_DIM), row),
                  pl.BlockSpec((fx.shape[0], fx.shape[1], tm // (fx.shape[1] * SUBLANES_F32)
                                * SUBLANES_F32, LANES), lambda i: (0, 0, i, 0)),
                  pl.BlockSpec((tm, d), row),
                  pl.BlockSpec((Q_DIM + F_DIM, d), const, pipeline_mode=pl.Buffered(1)),
                  vec, vec, vec, vec, pl.BlockSpec((d, ne), const)],
        out_specs=(pl.BlockSpec((tm, d), row), pl.BlockSpec((tm, d), row),
                   pl.BlockSpec((tm, ne), row)),
        compiler_params=_cparams("arbitrary"),
        name="output_projection",
    )(ax, fx, x, wo_b, gate, nw, shift, scale, wr_b)


def _route_kernel(aff_ref, posm_ref, cnt_ref, off_ref, *, cap, n_exp, n_chunk):
    aff_all = aff_ref[...]
    blocks = [aff_all[e * n_chunk:(e + 1) * n_chunk] for e in range(n_exp)]

    def total(mask):
        s = jnp.sum(jnp.where(mask, 1.0, 0.0), axis=0, keepdims=True)
        return jnp.sum(s, axis=1, keepdims=True)

    def search(it, thr_bits):
        bit = jnp.left_shift(jnp.int32(1), 30 - it)
        out = []
        for e in range(n_exp):
            cand = thr_bits[e] | bit
            enough = total(blocks[e] >= lax.bitcast_convert_type(cand, F32)) >= cap
            out.append(jnp.where(enough, cand, thr_bits[e]))
        return tuple(out)

    thr_bits = lax.fori_loop(0, 31, search, tuple(jnp.zeros((1, 1), I32) for _ in range(n_exp)))
    thr = [lax.bitcast_convert_type(t, F32) for t in thr_bits]

    rl = lax.broadcasted_iota(I32, (LANES, LANES), 0)
    cl = lax.broadcasted_iota(I32, (LANES, LANES), 1)
    before = jnp.where(rl < cl, 1.0, 0.0).astype(BF16)
    ones = jnp.ones((LANES, LANES), BF16)
    rc = lax.broadcasted_iota(I32, (n_chunk, n_chunk), 0)
    cc = lax.broadcasted_iota(I32, (n_chunk, n_chunk), 1)
    earlier = jnp.where(cc < rc, 1.0, 0.0).astype(BF16)

    def excl_cumsum(mask):
        mb = jnp.where(mask, 1.0, 0.0).astype(BF16)
        within = jnp.dot(mb, before, preferred_element_type=F32)
        rowtot = jnp.dot(mb, ones, preferred_element_type=F32)
        choff = jnp.dot(earlier, rowtot.astype(BF16), preferred_element_type=F32)
        return within + choff, rowtot, choff

    for e in range(n_exp):
        gt = blocks[e] > thr[e]
        eq = blocks[e] == thr[e]
        need = cap - total(gt)
        tie_rank, _, _ = excl_cumsum(eq)
        sel = gt | (eq & (tie_rank < need))
        pos, rowtot, choff = excl_cumsum(sel)
        rs = slice(e * n_chunk, (e + 1) * n_chunk)
        posm_ref[rs, :] = jnp.where(sel, pos.astype(I32), NOT_SELECTED)
        cnt_ref[rs, :] = rowtot.astype(I32)
        off_ref[rs, :] = choff.astype(I32)


def _route(aff2, cap, n_exp):
    rows = aff2.shape[0]
    n_chunk = rows // n_exp
    spec = pl.BlockSpec((rows, LANES), lambda i: (0, 0))
    shp = jax.ShapeDtypeStruct((rows, LANES), I32)
    return pl.pallas_call(
        functools.partial(_route_kernel, cap=cap, n_exp=n_exp, n_chunk=n_chunk),
        out_shape=(shp, shp, shp),
        grid=(1,),
        in_specs=[spec],
        out_specs=(spec, spec, spec),
        compiler_params=_cparams("arbitrary"),
        name="expert_choice_routing",
    )(aff2)


def _window_matches(posm, firsts, win_starts):
    r = lax.broadcasted_iota(I32, (WINDOW_ROWS, TOK_TILE), 0)
    out = []
    for e, (first, start) in enumerate(zip(firsts, win_starts)):
        pm = posm[e:e + 1, :]
        pm = jnp.where((pm >= first) & (pm < first + PIECE), pm, NOT_SELECTED)
        out.append((pm - start) == r)
    return out


def _align_down(v):
    return (v // ROW_ALIGN) * ROW_ALIGN


def _gather_kernel(off_ref, cnt_ref, g_ref, posm_ref, aff_ref, xs_hbm,
                   stg, tail, sem, npass_ref, *, n_exp, n_chunk, cap):
    c = pl.program_id(0)
    stride = cap + WINDOW_ROWS

    @pl.when(c == 0)
    def _():
        npass_ref[0] = 0
        tail[...] = jnp.zeros_like(tail)

    def window_copy(slot, e, dst):
        return pltpu.make_async_copy(stg.at[slot, pl.ds(e * WINDOW_ROWS, WINDOW_ROWS)],
                                     xs_hbm.at[pl.ds(dst, WINDOW_ROWS)], sem.at[slot])

    def wait_slot(slot):
        for e in range(n_exp):
            window_copy(slot, e, 0).wait()

    @pl.when(c == 0)
    def _():
        stg[0] = jnp.zeros(stg.shape[1:], stg.dtype)
        for e in range(n_exp):
            window_copy(0, e, e * stride + cap).start()
        wait_slot(0)

    most = cnt_ref[c]
    for e in range(1, n_exp):
        most = jnp.maximum(most, cnt_ref[e * n_chunk + c])
    passes = (most + PIECE - 1) // PIECE

    def one_pass(p, carry):
        done = npass_ref[0]
        slot = done % 2
        posm = posm_ref[0]
        aff = aff_ref[0]
        firsts, win_starts, next_shift = [], [], []
        for e in range(n_exp):
            o, n_e = off_ref[e * n_chunk + c], cnt_ref[e * n_chunk + c]
            first = o + jnp.minimum(p * PIECE, n_e)
            after = o + jnp.minimum((p + 1) * PIECE, n_e)
            firsts.append(first)
            win_starts.append(_align_down(first))
            next_shift.append(_align_down(after) - _align_down(first))
        matches = _window_matches(posm, firsts, win_starts)
        onehot = jnp.concatenate([jnp.where(mt, 1.0, 0.0).astype(BF16) for mt in matches], axis=0)
        new = jnp.dot(onehot, g_ref[...], preferred_element_type=F32)
        lane = lax.broadcasted_iota(I32, (WINDOW_ROWS, GATE_LANES), 1)
        for e in range(n_exp):
            lo = e * WINDOW_ROWS
            gate = jnp.sum(jnp.where(matches[e], aff[e:e + 1, :], 0.0), axis=1, keepdims=True)
            hi = gate.astype(BF16).astype(F32)
            mid = (gate - hi).astype(BF16).astype(F32)
            low = gate - hi - mid
            parts = jnp.where(lane == 0, hi, jnp.where(lane == 1, mid,
                                                       jnp.where(lane == 2, low, 0.0)))
            rows = jnp.concatenate([new[lo:lo + WINDOW_ROWS], parts], axis=1)
            stg[slot, lo:lo + ROW_ALIGN] = (rows[:ROW_ALIGN] + tail[e].astype(F32)).astype(BF16)
            stg[slot, lo + ROW_ALIGN:lo + WINDOW_ROWS] = rows[ROW_ALIGN:].astype(BF16)

        @pl.when(done > 0)
        def _():
            wait_slot(1 - slot)

        for e in range(n_exp):
            window_copy(slot, e, pl.multiple_of(e * stride + win_starts[e], ROW_ALIGN)).start()
        for e in range(n_exp):
            src = pl.ds(pl.multiple_of(e * WINDOW_ROWS + next_shift[e], ROW_ALIGN), ROW_ALIGN)
            tail[e] = stg[slot, src, :]
        npass_ref[0] = done + 1
        return carry

    lax.fori_loop(0, passes, one_pass, 0)

    @pl.when(c == pl.num_programs(0) - 1)
    def _():
        done = npass_ref[0]

        @pl.when(done > 0)
        def _():
            wait_slot((done - 1) % 2)


def _gather(off, cnt, g, posm_t, aff_t, cap):
    n, d = g.shape
    n_chunk, n_exp, _ = posm_t.shape
    rows = n_exp * (cap + WINDOW_ROWS)
    tile3 = lambda i, o, c: (i, 0, 0)
    return pl.pallas_call(
        functools.partial(_gather_kernel, n_exp=n_exp, n_chunk=n_chunk, cap=cap),
        out_shape=jax.ShapeDtypeStruct((rows, d + GATE_LANES), BF16),
        grid_spec=pltpu.PrefetchScalarGridSpec(
            num_scalar_prefetch=2,
            grid=(n_chunk,),
            in_specs=[pl.BlockSpec((TOK_TILE, d), lambda i, o, c: (i, 0)),
                      pl.BlockSpec((1, n_exp, TOK_TILE), tile3),
                      pl.BlockSpec((1, n_exp, TOK_TILE), tile3)],
            out_specs=pl.BlockSpec(memory_space=pl.ANY),
            scratch_shapes=[pltpu.VMEM((2, n_exp * WINDOW_ROWS, d + GATE_LANES), BF16),
                            pltpu.VMEM((n_exp, ROW_ALIGN, d + GATE_LANES), BF16),
                            pltpu.SemaphoreType.DMA((2,)),
                            pltpu.SMEM((1,), I32)]),
        compiler_params=_cparams("arbitrary"),
        name="gather_expert_tokens",
    )(off, cnt, g, posm_t, aff_t)


def _expert_kernel(xs_ref, wg_ref, wu_ref, wd_ref, y_ref, h_ref, gate_ref, *, nf, tf, tn):
    j = pl.program_id(1)
    d = wg_ref.shape[1]
    cap = xs_ref.shape[1]
    halves = [slice(r * (cap // EXPERT_ROW_SPLIT), (r + 1) * (cap // EXPERT_ROW_SPLIT))
              for r in range(EXPERT_ROW_SPLIT)]

    @pl.when(j == 0)
    def _():
        parts = xs_ref[0, :, d:].astype(F32)
        gate = parts[:, 0:1] + parts[:, 1:2] + parts[:, 2:3]
        gate_ref[...] = jnp.broadcast_to(gate, gate_ref.shape)

    @pl.when(j < nf)
    def _():
        wg = wg_ref[0].astype(BF16)
        wu = wu_ref[0].astype(BF16)
        for rs in halves:
            xs = xs_ref[0, rs, :d]
            gp = jnp.dot(xs, wg, preferred_element_type=F32)
            up = jnp.dot(xs, wu, preferred_element_type=F32)
            h = ((gp * jax.nn.sigmoid(gp)) * up).astype(BF16)
            for jj in range(nf):
                @pl.when(j == jj)
                def _():
                    h_ref[rs, jj * tf:(jj + 1) * tf] = h

    @pl.when(j >= nf)
    def _():
        wd = wd_ref[0].astype(BF16)
        for rs in halves:
            y = jnp.dot(h_ref[rs, :], wd, preferred_element_type=F32)
            y_ref[0, rs, :] = (y * gate_ref[rs, 0:1]).astype(BF16)


def _experts(xs3, w_gate, w_up, w_down, cap, tf=512, tn=512):
    n_exp, d, ff = w_gate.shape
    tf, tn = min(tf, ff), min(tn, d)
    nf, nn = ff // tf, d // tn

    def ahead(e, j):
        return jnp.minimum(e + (j >= nf).astype(I32), n_exp - 1)

    def ff_tile(e, j):
        return jnp.where(j >= nf, jnp.where(e == n_exp - 1, nf - 1, 0), j)

    return pl.pallas_call(
        functools.partial(_expert_kernel, nf=nf, tf=tf, tn=tn),
        out_shape=jax.ShapeDtypeStruct((n_exp, cap, d), BF16),
        grid=(n_exp, nf + nn),
        in_specs=[pl.BlockSpec((1, cap, d + GATE_LANES), lambda e, j: (ahead(e, j), 0, 0)),
                  pl.BlockSpec((1, d, tf), lambda e, j: (ahead(e, j), 0, ff_tile(e, j))),
                  pl.BlockSpec((1, d, tf), lambda e, j: (ahead(e, j), 0, ff_tile(e, j))),
                  pl.BlockSpec((1, ff, tn), lambda e, j: (e, 0, jnp.maximum(j - nf, 0)))],
        out_specs=pl.BlockSpec((1, cap, tn), lambda e, j: (e, 0, jnp.maximum(j - nf, 0))),
        scratch_shapes=[pltpu.VMEM((cap, ff), BF16), pltpu.VMEM((cap, LANES), F32)],
        compiler_params=_cparams("arbitrary", "arbitrary"),
        name="swiglu_experts",
    )(xs3, w_gate, w_up, w_down)


def _combine_kernel(off_ref, cnt_ref, y_hbm, posm_ref, x1_ref, gate_ref, nw_ref, o_ref,
                    stg, acc_ref, sem, *, n_exp, n_chunk, cap):
    c = pl.program_id(0)
    last_start = n_exp * cap - WINDOW_ROWS

    def firsts_of(tile, p):
        return [off_ref[e * n_chunk + tile] + p * PIECE for e in range(n_exp)]

    def starts_of(tile, p):
        return [pl.multiple_of(jnp.minimum(e * cap + _align_down(first), last_start), ROW_ALIGN)
                for e, first in enumerate(firsts_of(tile, p))]

    def piece_copy(slot, e, src):
        return pltpu.make_async_copy(y_hbm.at[pl.ds(src, WINDOW_ROWS)],
                                     stg.at[slot, pl.ds(e * WINDOW_ROWS, WINDOW_ROWS)],
                                     sem.at[slot])

    def fetch(tile, p, slot):
        for e, src in enumerate(starts_of(tile, p)):
            piece_copy(slot, e, src).start()

    def wait_slot(slot):
        for e in range(n_exp):
            piece_copy(slot, e, 0).wait()

    def contribution(p, slot):
        starts = [s - e * cap for e, s in enumerate(starts_of(c, p))]
        matches = _window_matches(posm_ref[0], firsts_of(c, p), starts)
        onehot = jnp.concatenate([jnp.where(mt, 1.0, 0.0).astype(BF16) for mt in matches], axis=0)
        return lax.dot_general(onehot, stg[slot], (((0,), (0,)), ((), ())),
                               preferred_element_type=F32)

    slot = c % 2

    @pl.when(c == 0)
    def _():
        fetch(0, 0, 0)

    @pl.when(c + 1 < pl.num_programs(0))
    def _():
        fetch(c + 1, 0, 1 - slot)

    wait_slot(slot)
    acc_ref[...] = contribution(0, slot)

    most = cnt_ref[c]
    for e in range(1, n_exp):
        most = jnp.maximum(most, cnt_ref[e * n_chunk + c])
    passes = (most + PIECE - 1) // PIECE

    def extra_pass(p, carry):
        fetch(c, p, slot)
        wait_slot(slot)
        acc_ref[...] += contribution(p, slot)
        return carry

    lax.fori_loop(1, passes, extra_pass, 0)
    x2 = x1_ref[...] + gate_ref[...] * acc_ref[...]
    ms = jnp.mean(x2 * x2, axis=-1, keepdims=True)
    o_ref[...] = x2 * lax.rsqrt(ms + RMS_EPS) * nw_ref[...]


def _combine(off, cnt, y, posm_t, x1, gate, nw, cap):
    n, d = x1.shape
    n_chunk, n_exp, _ = posm_t.shape
    return pl.pallas_call(
        functools.partial(_combine_kernel, n_exp=n_exp, n_chunk=n_chunk, cap=cap),
        out_shape=jax.ShapeDtypeStruct((n, d), F32),
        grid_spec=pltpu.PrefetchScalarGridSpec(
            num_scalar_prefetch=2,
            grid=(n_chunk,),
            in_specs=[pl.BlockSpec(memory_space=pl.ANY),
                      pl.BlockSpec((1, n_exp, TOK_TILE), lambda i, o, c: (i, 0, 0)),
                      pl.BlockSpec((TOK_TILE, d), lambda i, o, c: (i, 0)),
                      pl.BlockSpec((1, d), lambda i, o, c: (0, 0)),
                      pl.BlockSpec((1, d), lambda i, o, c: (0, 0))],
            out_specs=pl.BlockSpec((TOK_TILE, d), lambda i, o, c: (i, 0)),
            scratch_shapes=[pltpu.VMEM((2, n_exp * WINDOW_ROWS, d), BF16),
                            pltpu.VMEM((TOK_TILE, d), F32),
                            pltpu.SemaphoreType.DMA((2,))]),
        compiler_params=_cparams("arbitrary"),
        name="combine_experts_final_norm",
    )(off, cnt, y, posm_t, x1, gate, nw)


def kernel(x, c, ctx, c_ctx, w_mod, b_mod, norm_mix, w_in, sink, w_fourier, w_out, norm_ffn,
           w_router, w_gate, w_up, w_down, norm_final):
    assert x.shape[0] == 1 and w_mod.shape[0] == 1
    n, d = x.shape[1], x.shape[2]
    xl, cx = x[0], ctx[0]
    n_exp = w_router.shape[2]
    cap = max(1, CAP_FACTOR * n // n_exp)
    n_chunk = n // LANES
    n_tile = n // TOK_TILE
    per_tile = TOK_TILE // LANES

    mod = _modulation(jnp.stack([c[0], c_ctx], axis=1), w_mod[0], b_mod[0][None, :])
    mx = mod[0].reshape(6, 1, d)
    mc = mod[1].reshape(6, 1, d)

    w_in_b = w_in[0].astype(BF16)
    nmix = norm_mix[0][None, :]
    ab = _fold_channel_dft(w_fourier[0])
    q, k, v, z = _project_latent(xl, nmix, mx[0], mx[1], w_in_b, ab)
    kc, vc = _project_context(cx, nmix, mc[0], mc[1], w_in_b[:, Q_DIM:Q_DIM + 2 * KV_DIM])
    ax = _attention(q, k, v, kc, vc, sink[0])
    fx = _position_dft(z)

    x1, g, aff = _output_projection(ax, fx, xl, w_out[0].astype(BF16), mx[2], norm_ffn[0][None, :],
                                    mx[3], mx[4], w_router[0].astype(BF16))

    aff2 = aff.T.reshape(n_exp * n_chunk, LANES)
    posm, cnt, off = _route(aff2, cap, n_exp)
    by_tile = lambda a: a.reshape(n_exp, n_tile, TOK_TILE).transpose(1, 0, 2)
    posm_t, aff_t = by_tile(posm), by_tile(aff2)
    cnt1 = cnt[:, 0].reshape(n_exp * n_tile, per_tile).sum(axis=1)
    off1 = off[:, 0].reshape(n_exp * n_tile, per_tile)[:, 0]

    xs = _gather(off1, cnt1, g, posm_t, aff_t, cap)
    y = _experts(xs.reshape(n_exp, cap + WINDOW_ROWS, d + GATE_LANES),
                 w_gate[0], w_up[0], w_down[0], cap)
    out = _combine(off1, cnt1, y.reshape(n_exp * cap, d), posm_t, x1, mx[5], norm_final[None, :],
                   cap)
    return out[None]
```

```python
import functools
import math

import jax
import jax.numpy as jnp
from jax import lax
from jax.experimental import pallas as pl
from jax.experimental.pallas import tpu as pltpu

F32 = jnp.float32
BF16 = jnp.bfloat16
I32 = jnp.int32

HEAD_DIM = 128
N_HEADS = 8
N_KV_HEADS = 2
GROUP = N_HEADS // N_KV_HEADS
WINDOW = 128
Q_DIM = N_HEADS * HEAD_DIM
KV_DIM = N_KV_HEADS * HEAD_DIM
N_FGROUPS = 4
FG = 256
F_DIM = N_FGROUPS * FG
D_IN = Q_DIM + 2 * KV_DIM + F_DIM
N_EXPERTS = 16
CAP_FACTOR = 2
GRID_W = 64
ROPE_THETA = 10000.0
RMS_EPS = 1e-6
NEG_INF = -1e30
LOG2_E = math.log2(math.e)

LANES = 128
SUBLANES_F32 = 8
VMEM_LIMIT_BYTES = 56 * 1024 * 1024

TOK_TILE = 2 * LANES
PIECE = 48
ROW_ALIGN = 16
WINDOW_ROWS = ROW_ALIGN + PIECE
GATE_LANES = LANES
DFT_NB = LANES
OUTPROJ_CHUNKS = 2
PROJ_CHUNKS = 2
EXPERT_ROW_SPLIT = 2
NOT_SELECTED = -(1 << 20)


def _cparams(*sem):
    return pltpu.CompilerParams(dimension_semantics=sem, vmem_limit_bytes=VMEM_LIMIT_BYTES)


def _rms_mod(x, nw, shift, scale):
    ms = jnp.mean(x * x, axis=-1, keepdims=True)
    h = x * lax.rsqrt(ms + RMS_EPS) * nw
    return h * (1.0 + scale) + shift


def _mod_kernel(ct_ref, w_ref, b_ref, o_ref):
    ct = ct_ref[...]
    s = ct * jax.nn.sigmoid(ct)
    w = w_ref[...]
    r0 = jnp.sum(s[:, 0:1] * w, axis=0, keepdims=True)
    r1 = jnp.sum(s[:, 1:2] * w, axis=0, keepdims=True)
    o_ref[...] = jnp.concatenate([r0, r1], axis=0) + b_ref[...]


def _modulation(ct, w_mod, b_mod):
    d, n = w_mod.shape
    tn = math.gcd(n, 1024)
    return pl.pallas_call(
        _mod_kernel,
        out_shape=jax.ShapeDtypeStruct((2, n), F32),
        grid=(n // tn,),
        in_specs=[pl.BlockSpec((d, 2), lambda i: (0, 0)),
                  pl.BlockSpec((d, tn), lambda i: (0, i)),
                  pl.BlockSpec((1, tn), lambda i: (0, i))],
        out_specs=pl.BlockSpec((2, tn), lambda i: (0, i)),
        compiler_params=_cparams("arbitrary"),
        name="modulation",
    )(ct, w_mod, b_mod)


def _ab_kernel(cf_ref, sf_ref, wf_ref, ab_ref):
    wf = wf_ref[0]
    a = jnp.dot(cf_ref[...], wf, preferred_element_type=F32, precision=lax.Precision.HIGHEST)
    b = jnp.dot(sf_ref[...], wf, preferred_element_type=F32, precision=lax.Precision.HIGHEST)
    ab_ref[0] = jnp.concatenate([a, b], axis=1).astype(BF16)


def _fold_channel_dft(w_fourier):
    g, fg, _ = w_fourier.shape
    idx = jnp.arange(fg, dtype=I32)
    th = ((idx[:, None] * idx[None, :]) % fg).astype(F32) * (2.0 * math.pi / fg)
    scale = 1.0 / math.sqrt(fg)
    cf = jnp.cos(th) * scale
    sf = jnp.sin(th) * scale
    return pl.pallas_call(
        _ab_kernel,
        out_shape=jax.ShapeDtypeStruct((g, fg, 2 * fg), BF16),
        grid=(g,),
        in_specs=[pl.BlockSpec((fg, fg), lambda i: (0, 0)),
                  pl.BlockSpec((fg, fg), lambda i: (0, 0)),
                  pl.BlockSpec((1, fg, fg), lambda i: (i, 0, 0))],
        out_specs=pl.BlockSpec((1, fg, 2 * fg), lambda i: (i, 0, 0)),
        compiler_params=_cparams("arbitrary"),
        name="fold_channel_dft",
    )(cf, sf, w_fourier)


def _rope_tables(n):
    quarter = HEAD_DIM // 4
    freqs = ROPE_THETA ** (-jnp.arange(quarter, dtype=F32) / quarter)
    zeros = lambda m: jnp.zeros((m, 2 * quarter), F32)

    def tables(pos, low_half):
        ang = pos[:, None] * freqs[None, :]
        ang = jnp.concatenate([ang, ang], axis=-1)
        cos, sin = jnp.cos(ang), jnp.sin(ang)
        first = (jnp.arange(2 * quarter) < quarter)[None, :]
        parts = (cos, jnp.where(first, -sin, 0.0), jnp.where(first, 0.0, sin))
        pad = zeros(pos.shape[0])
        return jnp.stack([jnp.concatenate([p, pad] if low_half else [pad, p], axis=-1)
                          for p in parts])

    row_tab = tables(jnp.arange(n // GRID_W, dtype=F32), True)
    col_tab = tables(jnp.arange(GRID_W, dtype=F32), False)
    return row_tab, col_tab


def _store_dft_rows(z_ref, plane, val, a0):
    groups = z_ref.shape[1]
    nb = groups * SUBLANES_F32
    for al in range(val.shape[0] // nb):
        z_ref[plane, :, (a0 + al) * SUBLANES_F32:(a0 + al + 1) * SUBLANES_F32, :] = (
            val[al * nb:(al + 1) * nb].reshape(groups, SUBLANES_F32, LANES))


def _proj_kernel(x_ref, nw_ref, sh_ref, sc_ref, w_ref, ab_ref, rt_ref, ct_ref,
                 q_ref, k_ref, v_ref, z_ref):
    quarter = HEAD_DIM // 4
    qscale = LOG2_E / math.sqrt(HEAD_DIM)
    tm = x_ref.shape[0]
    chunk = tm // PROJ_CHUNKS
    for r0 in range(0, tm, chunk):
        rs = slice(r0, r0 + chunk)
        hb = _rms_mod(x_ref[rs, :], nw_ref[...], sh_ref[...], sc_ref[...]).astype(BF16)
        cos, s_up, s_dn = (
            jnp.concatenate(
                [rt_ref[t, r:r + 1, :] + ct_ref[t]
                 for r in range(r0 // GRID_W, (r0 + chunk) // GRID_W)], axis=0)
            for t in range(3))

        def rope(t):
            return (t * cos + pltpu.roll(t, HEAD_DIM - quarter, 1) * s_up
                    + pltpu.roll(t, quarter, 1) * s_dn)

        q = jnp.dot(hb, w_ref[:, :Q_DIM], preferred_element_type=F32)
        for j in range(N_HEADS):
            sl = slice(j * HEAD_DIM, (j + 1) * HEAD_DIM)
            q_ref[rs, sl] = (rope(q[:, sl]) * qscale).astype(BF16)
        k = jnp.dot(hb, w_ref[:, Q_DIM:Q_DIM + KV_DIM], preferred_element_type=F32)
        for j in range(N_KV_HEADS):
            sl = slice(j * HEAD_DIM, (j + 1) * HEAD_DIM)
            k_ref[rs, sl] = rope(k[:, sl]).astype(BF16)
        v = jnp.dot(hb, w_ref[:, Q_DIM + KV_DIM:Q_DIM + 2 * KV_DIM], preferred_element_type=F32)
        v_ref[rs, :] = v.astype(BF16)
        u = jnp.dot(hb, w_ref[:, Q_DIM + 2 * KV_DIM:], preferred_element_type=F32).astype(BF16)
        for g in range(N_FGROUPS):
            pq = jnp.dot(u[:, g * FG:(g + 1) * FG], ab_ref[g], preferred_element_type=F32)
            per_group = FG // LANES
            for t in range(2 * per_group):
                plane = (t // per_group) * (F_DIM // LANES) + g * per_group + t % per_group
                _store_dft_rows(z_ref, plane, pq[:, t * LANES:(t + 1) * LANES], r0 // DFT_NB)


def _project_latent(x, nw, shift, scale, w_in_b, ab, tm=512):
    n, d = x.shape
    row_tab, col_tab = _rope_tables(n)
    row = lambda i: (i, 0)
    const2 = lambda i: (0, 0)
    return pl.pallas_call(
        _proj_kernel,
        out_shape=(jax.ShapeDtypeStruct((n, Q_DIM), BF16),
                   jax.ShapeDtypeStruct((n, KV_DIM), BF16),
                   jax.ShapeDtypeStruct((n, KV_DIM), BF16),
                   jax.ShapeDtypeStruct((2 * F_DIM // LANES, DFT_NB // SUBLANES_F32,
                                         (n // DFT_NB) * SUBLANES_F32, LANES), F32)),
        grid=(n // tm,),
        in_specs=[pl.BlockSpec((tm, d), row),
                  pl.BlockSpec((1, d), const2), pl.BlockSpec((1, d), const2),
                  pl.BlockSpec((1, d), const2),
                  pl.BlockSpec((d, D_IN), const2, pipeline_mode=pl.Buffered(1)),
                  pl.BlockSpec((N_FGROUPS, FG, 2 * FG), lambda i: (0, 0, 0),
                               pipeline_mode=pl.Buffered(1)),
                  pl.BlockSpec((3, tm // GRID_W, HEAD_DIM), lambda i: (0, i, 0)),
                  pl.BlockSpec((3, GRID_W, HEAD_DIM), lambda i: (0, 0, 0))],
        out_specs=(pl.BlockSpec((tm, Q_DIM), row), pl.BlockSpec((tm, KV_DIM), row),
                   pl.BlockSpec((tm, KV_DIM), row),
                   pl.BlockSpec((2 * F_DIM // LANES, DFT_NB // SUBLANES_F32,
                                 (tm // DFT_NB) * SUBLANES_F32, LANES), lambda i: (0, 0, i, 0))),
        compiler_params=_cparams("arbitrary"),
        name="project_latent",
    )(x, nw, shift, scale, w_in_b, ab, row_tab, col_tab)


def _ctx_kernel(x_ref, nw_ref, sh_ref, sc_ref, w_ref, k_ref, v_ref):
    hb = _rms_mod(x_ref[...], nw_ref[...], sh_ref[...], sc_ref[...]).astype(BF16)
    kv = jnp.dot(hb, w_ref[...], preferred_element_type=F32)
    k_ref[...] = kv[:, :KV_DIM].astype(BF16)
    v_ref[...] = kv[:, KV_DIM:].astype(BF16)


def _project_context(ctx, nw, shift, scale, w_kv_b):
    m, d = ctx.shape
    full = lambda shp: pl.BlockSpec(shp, lambda i: (0, 0))
    return pl.pallas_call(
        _ctx_kernel,
        out_shape=(jax.ShapeDtypeStruct((m, KV_DIM), BF16),
                   jax.ShapeDtypeStruct((m, KV_DIM), BF16)),
        grid=(1,),
        in_specs=[full((m, d)), full((1, d)), full((1, d)), full((1, d)), full((d, 2 * KV_DIM))],
        out_specs=(full((m, KV_DIM)), full((m, KV_DIM))),
        compiler_params=_cparams("arbitrary"),
        name="project_context",
    )(ctx, nw, shift, scale, w_kv_b)


def _attn_kernel(sink_ref, q_ref, kp_ref, km_ref, kn_ref, vp_ref, vm_ref, vn_ref,
                 kc_ref, vc_ref, lo_ref, hi_ref, o_ref, *, n_total, tq):
    i = pl.program_id(0)
    nsub = tq // WINDOW
    last_blk = n_total // WINDOW - 1
    kwin = jnp.concatenate([kp_ref[...], km_ref[...], kn_ref[...]], axis=0)
    vwin = jnp.concatenate([vp_ref[...], vm_ref[...], vn_ref[...]], axis=0)
    rows = GROUP * WINDOW
    span = 3 * WINDOW
    head_of_row = lax.broadcasted_iota(I32, (rows, 1), 0) // WINDOW
    ones_loc = jnp.ones((span, HEAD_DIM), BF16)
    ones_ctx = jnp.ones((kc_ref.shape[0], HEAD_DIM), BF16)
    nt = (((1,), (1,)), ((), ()))
    for b in range(nsub):
        blk = i * nsub + b
        bias_lo = lo_ref[(blk == 0).astype(I32)]
        bias_hi = hi_ref[(blk == last_blk).astype(I32)]
        for h in range(N_KV_HEADS):
            hs = slice(h * HEAD_DIM, (h + 1) * HEAD_DIM)
            qs = jnp.concatenate(
                [q_ref[b * WINDOW:(b + 1) * WINDOW,
                       (h * GROUP + g) * HEAD_DIM:(h * GROUP + g + 1) * HEAD_DIM]
                 for g in range(GROUP)], axis=0)
            kw = kwin[b * WINDOW:b * WINDOW + span, hs]
            vw = vwin[b * WINDOW:b * WINDOW + span, hs]
            s_loc = lax.dot_general(qs, kw, nt, preferred_element_type=F32)
            s_ctx = lax.dot_general(qs, kc_ref[:, hs], nt, preferred_element_type=F32)
            parts = [s_loc[:, :WINDOW] + bias_lo, s_loc[:, WINDOW:2 * WINDOW],
                     s_loc[:, 2 * WINDOW:] + bias_hi, s_ctx]
            sink_col = jnp.zeros((rows, 1), F32)
            for g in range(GROUP):
                sink_col = jnp.where(head_of_row == g, sink_ref[h * GROUP + g] * LOG2_E, sink_col)
            blocks = parts[:3] + [s_ctx[:, t * WINDOW:(t + 1) * WINDOW]
                                  for t in range(s_ctx.shape[1] // WINDOW)]
            widest = blocks[0]
            for blk_scores in blocks[1:]:
                widest = jnp.maximum(widest, blk_scores)
            m = jnp.maximum(sink_col, jnp.max(widest, axis=1, keepdims=True))
            p = [jnp.exp2(part - m).astype(BF16) for part in parts]
            ov = (jnp.dot(jnp.concatenate(p[:3], axis=1), jnp.concatenate([vw, ones_loc], axis=1),
                          preferred_element_type=F32)
                  + jnp.dot(p[3], jnp.concatenate([vc_ref[:, hs], ones_ctx], axis=1),
                            preferred_element_type=F32))
            o = ov[:, :HEAD_DIM] / (ov[:, HEAD_DIM:] + jnp.exp2(sink_col - m))
            for g in range(GROUP):
                o_ref[b * WINDOW:(b + 1) * WINDOW,
                      (h * GROUP + g) * HEAD_DIM:(h * GROUP + g + 1) * HEAD_DIM] = (
                    o[g * WINDOW:(g + 1) * WINDOW].astype(BF16))


def _band_biases():
    r = jnp.arange(GROUP * WINDOW, dtype=I32)[:, None] % WINDOW
    c = jnp.arange(WINDOW, dtype=I32)[None, :]
    masked = jnp.full((GROUP * WINDOW, WINDOW), NEG_INF, F32)
    lo = jnp.stack([jnp.where(c >= r, 0.0, NEG_INF).astype(F32), masked])
    hi = jnp.stack([jnp.where(c <= r, 0.0, NEG_INF).astype(F32), masked])
    return lo, hi


def _attention(q, k, v, kc, vc, sink, tq=512):
    n = q.shape[0]
    bias_lo, bias_hi = _band_biases()
    bias_spec = pl.BlockSpec(bias_lo.shape, lambda i, s: (0, 0, 0))
    m = kc.shape[0]
    nsub = tq // WINDOW
    nblk = n // WINDOW
    prev = lambda i, s: (jnp.maximum(i * nsub - 1, 0), 0)
    main = lambda i, s: (i, 0)
    nxt = lambda i, s: (jnp.minimum(i * nsub + nsub, nblk - 1), 0)
    const = lambda i, s: (0, 0)
    kv_specs = [pl.BlockSpec((WINDOW, KV_DIM), prev), pl.BlockSpec((tq, KV_DIM), main),
                pl.BlockSpec((WINDOW, KV_DIM), nxt)]
    return pl.pallas_call(
        functools.partial(_attn_kernel, n_total=n, tq=tq),
        out_shape=jax.ShapeDtypeStruct((n, Q_DIM), BF16),
        grid_spec=pltpu.PrefetchScalarGridSpec(
            num_scalar_prefetch=1,
            grid=(n // tq,),
            in_specs=[pl.BlockSpec((tq, Q_DIM), main)] + kv_specs + kv_specs
                     + [pl.BlockSpec((m, KV_DIM), const), pl.BlockSpec((m, KV_DIM), const),
                        bias_spec, bias_spec],
            out_specs=pl.BlockSpec((tq, Q_DIM), main)),
        compiler_params=_cparams("arbitrary"),
        name="banded_attention",
    )(sink, q, k, k, k, v, v, v, kc, vc, bias_lo, bias_hi)


def _dft_tables(na, nb):
    n = na * nb
    s1 = 2.0 ** (-(int(math.log2(na)) // 2))
    s2 = (1.0 / math.sqrt(n)) / s1
    ka = jnp.arange(na, dtype=I32)
    th_tw = ((jnp.arange(nb, dtype=I32)[:, None] * ka[None, :]) % n).astype(F32) * (2.0 * math.pi / n)
    th_f = ((ka[:, None] * ka[None, :]) % na).astype(F32) * (2.0 * math.pi / na)
    twr, twi = (jnp.cos(th_tw) * s1)[:, :, None], (-jnp.sin(th_tw) * s1)[:, :, None]
    fr, fi = jnp.cos(th_f)[None, :, :], -jnp.sin(th_f)[None, :, :]
    tr, ti = twr * fr - twi * fi, twr * fi + twi * fr
    t1 = jnp.concatenate([jnp.concatenate([tr, ti], axis=-1),
                          jnp.concatenate([ti, -tr], axis=-1)], axis=-2).astype(BF16)
    kb = jnp.arange(nb, dtype=I32)
    th2 = ((kb[:, None] * kb[None, :]) % nb).astype(F32) * (2.0 * math.pi / nb)
    t2 = (jnp.concatenate([jnp.cos(th2), jnp.sin(th2)], axis=-1) * s2).astype(BF16)
    return t1, t2


def _dft1_kernel(z_ref, t_ref, y_ref):
    planes, _, rows, _ = z_ref.shape
    step = SUBLANES_F32
    half, na = planes // 2, rows // step
    for j in range(step):
        sel = pl.ds(j, na, stride=step)
        p = jnp.concatenate([z_ref[cc, 0, sel, :] for cc in range(half)], axis=1)
        q = jnp.concatenate([z_ref[cc, 0, sel, :] for cc in range(half, planes)], axis=1)
        xs = jnp.concatenate([p, q], axis=0).astype(BF16)
        y = jnp.dot(t_ref[j], xs, preferred_element_type=F32)
        for cc in range(half):
            y_ref[cc, 0, sel, :] = y[:na, cc * LANES:(cc + 1) * LANES]
            y_ref[half + cc, 0, sel, :] = y[na:, cc * LANES:(cc + 1) * LANES]


def _dft2_kernel(y_ref, t_ref, o_ref):
    planes, groups, rows, _ = y_ref.shape
    step = SUBLANES_F32
    half, nb = planes // 2, groups * step

    def tokens_b(cc, j):
        return y_ref[cc, :, j * step:(j + 1) * step, :].reshape(nb, LANES)

    for j in range(rows // step):
        yr = jnp.concatenate([tokens_b(cc, j) for cc in range(half)], axis=1)
        yi = jnp.concatenate([tokens_b(cc, j) for cc in range(half, planes)], axis=1)
        xs = jnp.concatenate([yr, yi], axis=0).astype(BF16)
        out = jnp.dot(t_ref[...], xs, preferred_element_type=F32)
        for cc in range(half):
            o_ref[cc, 0, pl.ds(j, nb, stride=step), :] = out[:, cc * LANES:(cc + 1) * LANES]


def _position_dft(z):
    planes, groups, rows, _ = z.shape
    step = SUBLANES_F32
    nb, na = groups * step, rows // step
    t1, t2 = _dft_tables(na, nb)
    y = pl.pallas_call(
        _dft1_kernel,
        out_shape=jax.ShapeDtypeStruct(z.shape, F32),
        grid=(groups,),
        in_specs=[pl.BlockSpec((planes, 1, rows, LANES), lambda i: (0, i, 0, 0)),
                  pl.BlockSpec((step, 2 * na, 2 * na), lambda i: (i, 0, 0))],
        out_specs=pl.BlockSpec((planes, 1, rows, LANES), lambda i: (0, i, 0, 0)),
        compiler_params=_cparams("arbitrary"),
        name="position_dft_stage1",
    )(z, t1)
    return pl.pallas_call(
        _dft2_kernel,
        out_shape=jax.ShapeDtypeStruct((planes // 2, na // step, nb * step, LANES), F32),
        grid=(na // step,),
        in_specs=[pl.BlockSpec((planes, groups, step * step, LANES), lambda i: (0, 0, i, 0)),
                  pl.BlockSpec((nb, 2 * nb), lambda i: (0, 0))],
        out_specs=pl.BlockSpec((planes // 2, 1, nb * step, LANES), lambda i: (0, i, 0, 0)),
        compiler_params=_cparams("arbitrary"),
        name="position_dft_stage2",
    )(y, t2)


def _outproj_kernel(ax_ref, fx_ref, x_ref, wo_ref, gate_ref, nw_ref, sh_ref, sc_ref, wr_ref,
                    x1_ref, g_ref, aff_ref):
    planes, groups, rows, _ = fx_ref.shape
    na = groups * SUBLANES_F32
    n_kb = rows // SUBLANES_F32
    kb_per_chunk = max(1, n_kb // OUTPROJ_CHUNKS)
    for first_kb in range(0, n_kb, kb_per_chunk):
        rs = slice(first_kb * na, (first_kb + kb_per_chunk) * na)
        fx = jnp.concatenate(
            [jnp.concatenate(
                [fx_ref[cc, :, kb * SUBLANES_F32:(kb + 1) * SUBLANES_F32, :].reshape(na, LANES)
                 for kb in range(first_kb, first_kb + kb_per_chunk)], axis=0)
             for cc in range(planes)], axis=1).astype(BF16)
        acc = (jnp.dot(ax_ref[rs, :], wo_ref[:Q_DIM, :], preferred_element_type=F32)
               + jnp.dot(fx, wo_ref[Q_DIM:, :], preferred_element_type=F32))
        x1 = x_ref[rs, :] + gate_ref[...] * acc
        x1_ref[rs, :] = x1
        gb = _rms_mod(x1, nw_ref[...], sh_ref[...], sc_ref[...]).astype(BF16)
        g_ref[rs, :] = gb
        logits = jnp.dot(gb, wr_ref[...], preferred_element_type=F32)
        e = jnp.exp(logits - jnp.max(logits, axis=1, keepdims=True))
        aff_ref[rs, :] = e / jnp.sum(e, axis=1, keepdims=True)


def _output_projection(ax, fx, x, wo_b, gate, nw, shift, scale, wr_b, tm=512):
    n, d = x.shape
    ne = wr_b.shape[1]
    row = lambda i: (i, 0)
    const = lambda i: (0, 0)
    vec = pl.BlockSpec((1, d), const)
    return pl.pallas_call(
        _outproj_kernel,
        out_shape=(jax.ShapeDtypeStruct((n, d), F32), jax.ShapeDtypeStruct((n, d), BF16),
                   jax.ShapeDtypeStruct((n, ne), F32)),
        grid=(n // tm,),
        in_specs=[pl.BlockSpec((tm, Q_DIM), row),
                  pl.BlockSpec((fx.shape[0], fx.shape[1], tm // (fx.shape[1] * SUBLANES_F32)
                                * SUBLANES_F32, LANES), lambda i: (0, 0, i, 0)),
                  pl.BlockSpec((tm, d), row),
                  pl.BlockSpec((Q_DIM + F_DIM, d), const, pipeline_mode=pl.Buffered(1)),
                  vec, vec, vec, vec, pl.BlockSpec((d, ne), const)],
        out_specs=(pl.BlockSpec((tm, d), row), pl.BlockSpec((tm, d), row),
                   pl.BlockSpec((tm, ne), row)),
        compiler_params=_cparams("arbitrary"),
        name="output_projection",
    )(ax, fx, x, wo_b, gate, nw, shift, scale, wr_b)


def _route_kernel(aff_ref, posm_ref, cnt_ref, off_ref, *, cap, n_exp, n_chunk):
    aff_all = aff_ref[...]
    blocks = [aff_all[e * n_chunk:(e + 1) * n_chunk] for e in range(n_exp)]

    def total(mask):
        s = jnp.sum(jnp.where(mask, 1.0, 0.0), axis=0, keepdims=True)
        return jnp.sum(s, axis=1, keepdims=True)

    def search(it, thr_bits):
        bit = jnp.left_shift(jnp.int32(1), 30 - it)
        out = []
        for e in range(n_exp):
            cand = thr_bits[e] | bit
            enough = total(blocks[e] >= lax.bitcast_convert_type(cand, F32)) >= cap
            out.append(jnp.where(enough, cand, thr_bits[e]))
        return tuple(out)

    thr_bits = lax.fori_loop(0, 31, search, tuple(jnp.zeros((1, 1), I32) for _ in range(n_exp)))
    thr = [lax.bitcast_convert_type(t, F32) for t in thr_bits]

    rl = lax.broadcasted_iota(I32, (LANES, LANES), 0)
    cl = lax.broadcasted_iota(I32, (LANES, LANES), 1)
    before = jnp.where(rl < cl, 1.0, 0.0).astype(BF16)
    ones = jnp.ones((LANES, LANES), BF16)
    rc = lax.broadcasted_iota(I32, (n_chunk, n_chunk), 0)
    cc = lax.broadcasted_iota(I32, (n_chunk, n_chunk), 1)
    earlier = jnp.where(cc < rc, 1.0, 0.0).astype(BF16)

    def excl_cumsum(mask):
        mb = jnp.where(mask, 1.0, 0.0).astype(BF16)
        within = jnp.dot(mb, before, preferred_element_type=F32)
        rowtot = jnp.dot(mb, ones, preferred_element_type=F32)
        choff = jnp.dot(earlier, rowtot.astype(BF16), preferred_element_type=F32)
        return within + choff, rowtot, choff

    for e in range(n_exp):
        gt = blocks[e] > thr[e]
        eq = blocks[e] == thr[e]
        need = cap - total(gt)
        tie_rank, _, _ = excl_cumsum(eq)
        sel = gt | (eq & (tie_rank < need))
        pos, rowtot, choff = excl_cumsum(sel)
        rs = slice(e * n_chunk, (e + 1) * n_chunk)
        posm_ref[rs, :] = jnp.where(sel, pos.astype(I32), NOT_SELECTED)
        cnt_ref[rs, :] = rowtot.astype(I32)
        off_ref[rs, :] = choff.astype(I32)


def _route(aff2, cap, n_exp):
    rows = aff2.shape[0]
    n_chunk = rows // n_exp
    spec = pl.BlockSpec((rows, LANES), lambda i: (0, 0))
    shp = jax.ShapeDtypeStruct((rows, LANES), I32)
    return pl.pallas_call(
        functools.partial(_route_kernel, cap=cap, n_exp=n_exp, n_chunk=n_chunk),
        out_shape=(shp, shp, shp),
        grid=(1,),
        in_specs=[spec],
        out_specs=(spec, spec, spec),
        compiler_params=_cparams("arbitrary"),
        name="expert_choice_routing",
    )(aff2)


def _window_matches(posm, firsts, win_starts):
    r = lax.broadcasted_iota(I32, (WINDOW_ROWS, TOK_TILE), 0)
    out = []
    for e, (first, start) in enumerate(zip(firsts, win_starts)):
        pm = posm[e:e + 1, :]
        pm = jnp.where((pm >= first) & (pm < first + PIECE), pm, NOT_SELECTED)
        out.append((pm - start) == r)
    return out


def _align_down(v):
    return (v // ROW_ALIGN) * ROW_ALIGN


def _gather_kernel(off_ref, cnt_ref, g_ref, posm_ref, aff_ref, xs_hbm,
                   stg, tail, sem, npass_ref, *, n_exp, n_chunk, cap):
    c = pl.program_id(0)
    stride = cap + WINDOW_ROWS

    @pl.when(c == 0)
    def _():
        npass_ref[0] = 0
        tail[...] = jnp.zeros_like(tail)

    def window_copy(slot, e, dst):
        return pltpu.make_async_copy(stg.at[slot, pl.ds(e * WINDOW_ROWS, WINDOW_ROWS)],
                                     xs_hbm.at[pl.ds(dst, WINDOW_ROWS)], sem.at[slot, e])

    def wait_slot(slot):
        for e in range(n_exp):
            window_copy(slot, e, 0).wait()

    @pl.when(c == 0)
    def _():
        stg[0] = jnp.zeros(stg.shape[1:], stg.dtype)
        for e in range(n_exp):
            window_copy(0, e, e * stride + cap).start()
        wait_slot(0)

    most = cnt_ref[c]
    for e in range(1, n_exp):
        most = jnp.maximum(most, cnt_ref[e * n_chunk + c])
    passes = (most + PIECE - 1) // PIECE

    def one_pass(p, carry):
        done = npass_ref[0]
        slot = done % 2
        posm = posm_ref[0]
        aff = aff_ref[0]
        firsts, win_starts, next_shift = [], [], []
        for e in range(n_exp):
            o, n_e = off_ref[e * n_chunk + c], cnt_ref[e * n_chunk + c]
            first = o + jnp.minimum(p * PIECE, n_e)
            after = o + jnp.minimum((p + 1) * PIECE, n_e)
            firsts.append(first)
            win_starts.append(_align_down(first))
            next_shift.append(_align_down(after) - _align_down(first))
        matches = _window_matches(posm, firsts, win_starts)
        onehot = jnp.concatenate([jnp.where(mt, 1.0, 0.0).astype(BF16) for mt in matches], axis=0)
        new = jnp.dot(onehot, g_ref[...], preferred_element_type=F32)
        lane = lax.broadcasted_iota(I32, (WINDOW_ROWS, GATE_LANES), 1)
        for e in range(n_exp):
            lo = e * WINDOW_ROWS
            gate = jnp.sum(jnp.where(matches[e], aff[e:e + 1, :], 0.0), axis=1, keepdims=True)
            hi = gate.astype(BF16).astype(F32)
            mid = (gate - hi).astype(BF16).astype(F32)
            low = gate - hi - mid
            parts = jnp.where(lane == 0, hi, jnp.where(lane == 1, mid,
                                                       jnp.where(lane == 2, low, 0.0)))
            rows = jnp.concatenate([new[lo:lo + WINDOW_ROWS], parts], axis=1)
            stg[slot, lo:lo + ROW_ALIGN] = (rows[:ROW_ALIGN] + tail[e].astype(F32)).astype(BF16)
            stg[slot, lo + ROW_ALIGN:lo + WINDOW_ROWS] = rows[ROW_ALIGN:].astype(BF16)

        dsts = [pl.multiple_of(e * stride + win_starts[e], ROW_ALIGN) for e in range(n_exp)]

        @pl.when(done > 0)
        def _():
            for e in range(n_exp):
                window_copy(1 - slot, e, 0).wait()
                window_copy(slot, e, dsts[e]).start()

        @pl.when(done == 0)
        def _():
            for e in range(n_exp):
                window_copy(slot, e, dsts[e]).start()

        for e in range(n_exp):
            src = pl.ds(pl.multiple_of(e * WINDOW_ROWS + next_shift[e], ROW_ALIGN), ROW_ALIGN)
            tail[e] = stg[slot, src, :]
        npass_ref[0] = done + 1
        return carry

    lax.fori_loop(0, passes, one_pass, 0)

    @pl.when(c == pl.num_programs(0) - 1)
    def _():
        done = npass_ref[0]

        @pl.when(done > 0)
        def _():
            wait_slot((done - 1) % 2)


def _gather(off, cnt, g, posm_t, aff_t, cap):
    n, d = g.shape
    n_chunk, n_exp, _ = posm_t.shape
    rows = n_exp * (cap + WINDOW_ROWS)
    tile3 = lambda i, o, c: (i, 0, 0)
    return pl.pallas_call(
        functools.partial(_gather_kernel, n_exp=n_exp, n_chunk=n_chunk, cap=cap),
        out_shape=jax.ShapeDtypeStruct((rows, d + GATE_LANES), BF16),
        grid_spec=pltpu.PrefetchScalarGridSpec(
            num_scalar_prefetch=2,
            grid=(n_chunk,),
            in_specs=[pl.BlockSpec((TOK_TILE, d), lambda i, o, c: (i, 0)),
                      pl.BlockSpec((1, n_exp, TOK_TILE), tile3),
                      pl.BlockSpec((1, n_exp, TOK_TILE), tile3)],
            out_specs=pl.BlockSpec(memory_space=pl.ANY),
            scratch_shapes=[pltpu.VMEM((2, n_exp * WINDOW_ROWS, d + GATE_LANES), BF16),
                            pltpu.VMEM((n_exp, ROW_ALIGN, d + GATE_LANES), BF16),
                            pltpu.SemaphoreType.DMA((2, n_exp)),
                            pltpu.SMEM((1,), I32)]),
        compiler_params=_cparams("arbitrary"),
        name="gather_expert_tokens",
    )(off, cnt, g, posm_t, aff_t)


def _expert_kernel(xs_ref, wg_ref, wu_ref, wd_ref, y_ref, h_ref, gate_ref, *, nf, tf, tn):
    j = pl.program_id(1)
    d = wg_ref.shape[1]
    cap = xs_ref.shape[1]
    halves = [slice(r * (cap // EXPERT_ROW_SPLIT), (r + 1) * (cap // EXPERT_ROW_SPLIT))
              for r in range(EXPERT_ROW_SPLIT)]

    @pl.when(j == 0)
    def _():
        parts = xs_ref[0, :, d:].astype(F32)
        gate = parts[:, 0:1] + parts[:, 1:2] + parts[:, 2:3]
        gate_ref[...] = jnp.broadcast_to(gate, gate_ref.shape)

    @pl.when(j < nf)
    def _():
        wg = wg_ref[0].astype(BF16)
        wu = wu_ref[0].astype(BF16)
        for rs in halves:
            xs = xs_ref[0, rs, :d]
            gp = jnp.dot(xs, wg, preferred_element_type=F32)
            up = jnp.dot(xs, wu, preferred_element_type=F32)
            h = ((gp * jax.nn.sigmoid(gp)) * up).astype(BF16)
            for jj in range(nf):
                @pl.when(j == jj)
                def _():
                    h_ref[rs, jj * tf:(jj + 1) * tf] = h

    @pl.when(j >= nf)
    def _():
        wd = wd_ref[0].astype(BF16)
        for rs in halves:
            y = jnp.dot(h_ref[rs, :], wd, preferred_element_type=F32)
            y_ref[0, rs, :] = (y * gate_ref[rs, 0:1]).astype(BF16)


def _experts(xs3, w_gate, w_up, w_down, cap, tf=512, tn=512):
    n_exp, d, ff = w_gate.shape
    tf, tn = min(tf, ff), min(tn, d)
    nf, nn = ff // tf, d // tn

    def ahead(e, j):
        return jnp.minimum(e + (j >= nf).astype(I32), n_exp - 1)

    def ff_tile(e, j):
        return jnp.where(j >= nf, jnp.where(e == n_exp - 1, nf - 1, 0), j)

    return pl.pallas_call(
        functools.partial(_expert_kernel, nf=nf, tf=tf, tn=tn),
        out_shape=jax.ShapeDtypeStruct((n_exp, cap, d), BF16),
        grid=(n_exp, nf + nn),
        in_specs=[pl.BlockSpec((1, cap, d + GATE_LANES), lambda e, j: (ahead(e, j), 0, 0)),
                  pl.BlockSpec((1, d, tf), lambda e, j: (ahead(e, j), 0, ff_tile(e, j))),
                  pl.BlockSpec((1, d, tf), lambda e, j: (ahead(e, j), 0, ff_tile(e, j))),
                  pl.BlockSpec((1, ff, tn), lambda e, j: (e, 0, jnp.maximum(j - nf, 0)))],
        out_specs=pl.BlockSpec((1, cap, tn), lambda e, j: (e, 0, jnp.maximum(j - nf, 0))),
        scratch_shapes=[pltpu.VMEM((cap, ff), BF16), pltpu.VMEM((cap, LANES), F32)],
        compiler_params=_cparams("arbitrary", "arbitrary"),
        name="swiglu_experts",
    )(xs3, w_gate, w_up, w_down)


def _combine_kernel(off_ref, cnt_ref, y_hbm, posm_ref, x1_ref, gate_ref, nw_ref, o_ref,
                    stg, acc_ref, sem, *, n_exp, n_chunk, cap):
    c = pl.program_id(0)
    last_start = n_exp * cap - WINDOW_ROWS

    def firsts_of(tile, p):
        return [off_ref[e * n_chunk + tile] + p * PIECE for e in range(n_exp)]

    def starts_of(tile, p):
        return [pl.multiple_of(jnp.minimum(e * cap + _align_down(first), last_start), ROW_ALIGN)
                for e, first in enumerate(firsts_of(tile, p))]

    def piece_copy(slot, e, src):
        return pltpu.make_async_copy(y_hbm.at[pl.ds(src, WINDOW_ROWS)],
                                     stg.at[slot, pl.ds(e * WINDOW_ROWS, WINDOW_ROWS)],
                                     sem.at[slot])

    def fetch(tile, p, slot):
        for e, src in enumerate(starts_of(tile, p)):
            piece_copy(slot, e, src).start()

    def wait_slot(slot):
        for e in range(n_exp):
            piece_copy(slot, e, 0).wait()

    def contribution(p, slot, tokens):
        starts = [s - e * cap for e, s in enumerate(starts_of(c, p))]
        matches = _window_matches(posm_ref[0], firsts_of(c, p), starts)
        onehot = jnp.concatenate([jnp.where(mt[:, tokens], 1.0, 0.0).astype(BF16)
                                  for mt in matches], axis=0)
        return lax.dot_general(onehot, stg[slot], (((0,), (0,)), ((), ())),
                               preferred_element_type=F32)

    def finish(tokens, moe):
        x2 = x1_ref[tokens, :] + gate_ref[...] * moe
        ms = jnp.mean(x2 * x2, axis=-1, keepdims=True)
        o_ref[tokens, :] = x2 * lax.rsqrt(ms + RMS_EPS) * nw_ref[...]

    slot = c % 2

    @pl.when(c == 0)
    def _():
        fetch(0, 0, 0)

    @pl.when(c + 1 < pl.num_programs(0))
    def _():
        fetch(c + 1, 0, 1 - slot)

    most = cnt_ref[c]
    for e in range(1, n_exp):
        most = jnp.maximum(most, cnt_ref[e * n_chunk + c])
    passes = (most + PIECE - 1) // PIECE
    wait_slot(slot)

    @pl.when(passes <= 1)
    def _():
        for t0 in range(0, TOK_TILE, LANES):
            tokens = slice(t0, t0 + LANES)
            finish(tokens, contribution(0, slot, tokens))

    @pl.when(passes > 1)
    def _():
        everyone = slice(0, TOK_TILE)
        acc_ref[...] = contribution(0, slot, everyone)

        def extra_pass(p, carry):
            fetch(c, p, slot)
            wait_slot(slot)
            acc_ref[...] += contribution(p, slot, everyone)
            return carry

        lax.fori_loop(1, passes, extra_pass, 0)
        finish(everyone, acc_ref[...])


def _combine(off, cnt, y, posm_t, x1, gate, nw, cap):
    n, d = x1.shape
    n_chunk, n_exp, _ = posm_t.shape
    return pl.pallas_call(
        functools.partial(_combine_kernel, n_exp=n_exp, n_chunk=n_chunk, cap=cap),
        out_shape=jax.ShapeDtypeStruct((n, d), F32),
        grid_spec=pltpu.PrefetchScalarGridSpec(
            num_scalar_prefetch=2,
            grid=(n_chunk,),
            in_specs=[pl.BlockSpec(memory_space=pl.ANY),
                      pl.BlockSpec((1, n_exp, TOK_TILE), lambda i, o, c: (i, 0, 0)),
                      pl.BlockSpec((TOK_TILE, d), lambda i, o, c: (i, 0)),
                      pl.BlockSpec((1, d), lambda i, o, c: (0, 0)),
                      pl.BlockSpec((1, d), lambda i, o, c: (0, 0))],
            out_specs=pl.BlockSpec((TOK_TILE, d), lambda i, o, c: (i, 0)),
            scratch_shapes=[pltpu.VMEM((2, n_exp * WINDOW_ROWS, d), BF16),
                            pltpu.VMEM((TOK_TILE, d), F32),
                            pltpu.SemaphoreType.DMA((2,))]),
        compiler_params=_cparams("arbitrary"),
        name="combine_experts_final_norm",
    )(off, cnt, y, posm_t, x1, gate, nw)


def kernel(x, c, ctx, c_ctx, w_mod, b_mod, norm_mix, w_in, sink, w_fourier, w_out, norm_ffn,
           w_router, w_gate, w_up, w_down, norm_final):
    assert x.shape[0] == 1 and w_mod.shape[0] == 1
    n, d = x.shape[1], x.shape[2]
    xl, cx = x[0], ctx[0]
    n_exp = w_router.shape[2]
    cap = max(1, CAP_FACTOR * n // n_exp)
    n_chunk = n // LANES
    n_tile = n // TOK_TILE
    per_tile = TOK_TILE // LANES

    mod = _modulation(jnp.stack([c[0], c_ctx], axis=1), w_mod[0], b_mod[0][None, :])
    mx = mod[0].reshape(6, 1, d)
    mc = mod[1].reshape(6, 1, d)

    w_in_b = w_in[0].astype(BF16)
    nmix = norm_mix[0][None, :]
    ab = _fold_channel_dft(w_fourier[0])
    q, k, v, z = _project_latent(xl, nmix, mx[0], mx[1], w_in_b, ab)
    kc, vc = _project_context(cx, nmix, mc[0], mc[1], w_in_b[:, Q_DIM:Q_DIM + 2 * KV_DIM])
    ax = _attention(q, k, v, kc, vc, sink[0])
    fx = _position_dft(z)

    x1, g, aff = _output_projection(ax, fx, xl, w_out[0].astype(BF16), mx[2], norm_ffn[0][None, :],
                                    mx[3], mx[4], w_router[0].astype(BF16))

    aff2 = aff.T.reshape(n_exp * n_chunk, LANES)
    posm, cnt, off = _route(aff2, cap, n_exp)
    by_tile = lambda a: a.reshape(n_exp, n_tile, TOK_TILE).transpose(1, 0, 2)
    posm_t, aff_t = by_tile(posm), by_tile(aff2)
    cnt1 = cnt[:, 0].reshape(n_exp * n_tile, per_tile).sum(axis=1)
    off1 = off[:, 0].reshape(n_exp * n_tile, per_tile)[:, 0]

    xs = _gather(off1, cnt1, g, posm_t, aff_t, cap)
    y = _experts(xs.reshape(n_exp, cap + WINDOW_ROWS, d + GATE_LANES),
                 w_gate[0], w_up[0], w_down[0], cap)
    out = _combine(off1, cnt1, y.reshape(n_exp * cap, d), posm_t, x1, mx[5], norm_final[None, :],
                   cap)
    return out[None]
```

```python
import functools
import math

import jax
import jax.numpy as jnp
from jax import lax
from jax.experimental import pallas as pl
from jax.experimental.pallas import tpu as pltpu

F32 = jnp.float32
BF16 = jnp.bfloat16
I32 = jnp.int32

HEAD_DIM = 128
N_HEADS = 8
N_KV_HEADS = 2
GROUP = N_HEADS // N_KV_HEADS
WINDOW = 128
Q_DIM = N_HEADS * HEAD_DIM
KV_DIM = N_KV_HEADS * HEAD_DIM
N_FGROUPS = 4
FG = 256
F_DIM = N_FGROUPS * FG
D_IN = Q_DIM + 2 * KV_DIM + F_DIM
N_EXPERTS = 16
CAP_FACTOR = 2
GRID_W = 64
ROPE_THETA = 10000.0
RMS_EPS = 1e-6
NEG_INF = -1e30
LOG2_E = math.log2(math.e)

LANES = 128
SUBLANES_F32 = 8
VMEM_LIMIT_BYTES = 56 * 1024 * 1024

TOK_TILE = 2 * LANES
PIECE = 48
ROW_ALIGN = 16
WINDOW_ROWS = ROW_ALIGN + PIECE
WINDOW_SIZES = tuple(range(2 * ROW_ALIGN, WINDOW_ROWS + 1, ROW_ALIGN))
GATE_LANES = LANES
DFT_NB = LANES
OUTPROJ_CHUNKS = 2
PROJ_CHUNKS = 2
EXPERT_ROW_SPLIT = 2
NOT_SELECTED = -(1 << 20)


def _cparams(*sem):
    return pltpu.CompilerParams(dimension_semantics=sem, vmem_limit_bytes=VMEM_LIMIT_BYTES)


def _rms_mod(x, nw, shift, scale):
    ms = jnp.mean(x * x, axis=-1, keepdims=True)
    h = x * lax.rsqrt(ms + RMS_EPS) * nw
    return h * (1.0 + scale) + shift


def _mod_kernel(ct_ref, w_ref, b_ref, o_ref):
    ct = ct_ref[...]
    s = ct * jax.nn.sigmoid(ct)
    w = w_ref[...]
    r0 = jnp.sum(s[:, 0:1] * w, axis=0, keepdims=True)
    r1 = jnp.sum(s[:, 1:2] * w, axis=0, keepdims=True)
    o_ref[...] = jnp.concatenate([r0, r1], axis=0) + b_ref[...]


def _modulation(ct, w_mod, b_mod):
    d, n = w_mod.shape
    tn = math.gcd(n, 1024)
    return pl.pallas_call(
        _mod_kernel,
        out_shape=jax.ShapeDtypeStruct((2, n), F32),
        grid=(n // tn,),
        in_specs=[pl.BlockSpec((d, 2), lambda i: (0, 0)),
                  pl.BlockSpec((d, tn), lambda i: (0, i)),
                  pl.BlockSpec((1, tn), lambda i: (0, i))],
        out_specs=pl.BlockSpec((2, tn), lambda i: (0, i)),
        compiler_params=_cparams("arbitrary"),
        name="modulation",
    )(ct, w_mod, b_mod)


def _ab_kernel(cf_ref, sf_ref, wf_ref, ab_ref):
    wf = wf_ref[0]
    a = jnp.dot(cf_ref[...], wf, preferred_element_type=F32, precision=lax.Precision.HIGHEST)
    b = jnp.dot(sf_ref[...], wf, preferred_element_type=F32, precision=lax.Precision.HIGHEST)
    ab_ref[0] = jnp.concatenate([a, b], axis=1).astype(BF16)


def _fold_channel_dft(w_fourier):
    g, fg, _ = w_fourier.shape
    idx = jnp.arange(fg, dtype=I32)
    th = ((idx[:, None] * idx[None, :]) % fg).astype(F32) * (2.0 * math.pi / fg)
    scale = 1.0 / math.sqrt(fg)
    cf = jnp.cos(th) * scale
    sf = jnp.sin(th) * scale
    return pl.pallas_call(
        _ab_kernel,
        out_shape=jax.ShapeDtypeStruct((g, fg, 2 * fg), BF16),
        grid=(g,),
        in_specs=[pl.BlockSpec((fg, fg), lambda i: (0, 0)),
                  pl.BlockSpec((fg, fg), lambda i: (0, 0)),
                  pl.BlockSpec((1, fg, fg), lambda i: (i, 0, 0))],
        out_specs=pl.BlockSpec((1, fg, 2 * fg), lambda i: (i, 0, 0)),
        compiler_params=_cparams("arbitrary"),
        name="fold_channel_dft",
    )(cf, sf, w_fourier)


def _rope_tables(n):
    quarter = HEAD_DIM // 4
    freqs = ROPE_THETA ** (-jnp.arange(quarter, dtype=F32) / quarter)
    zeros = lambda m: jnp.zeros((m, 2 * quarter), F32)

    def tables(pos, low_half):
        ang = pos[:, None] * freqs[None, :]
        ang = jnp.concatenate([ang, ang], axis=-1)
        cos, sin = jnp.cos(ang), jnp.sin(ang)
        first = (jnp.arange(2 * quarter) < quarter)[None, :]
        parts = (cos, jnp.where(first, -sin, 0.0), jnp.where(first, 0.0, sin))
        pad = zeros(pos.shape[0])
        return jnp.stack([jnp.concatenate([p, pad] if low_half else [pad, p], axis=-1)
                          for p in parts])

    row_tab = tables(jnp.arange(n // GRID_W, dtype=F32), True)
    col_tab = tables(jnp.arange(GRID_W, dtype=F32), False)
    return row_tab, col_tab


def _store_dft_rows(z_ref, plane, val, a0):
    groups = z_ref.shape[1]
    nb = groups * SUBLANES_F32
    for al in range(val.shape[0] // nb):
        z_ref[plane, :, (a0 + al) * SUBLANES_F32:(a0 + al + 1) * SUBLANES_F32, :] = (
            val[al * nb:(al + 1) * nb].reshape(groups, SUBLANES_F32, LANES))


def _proj_kernel(x_ref, nw_ref, sh_ref, sc_ref, w_ref, ab_ref, rt_ref, ct_ref,
                 q_ref, k_ref, v_ref, z_ref):
    quarter = HEAD_DIM // 4
    qscale = LOG2_E / math.sqrt(HEAD_DIM)
    tm = x_ref.shape[0]
    chunk = tm // PROJ_CHUNKS
    for r0 in range(0, tm, chunk):
        rs = slice(r0, r0 + chunk)
        hb = _rms_mod(x_ref[rs, :], nw_ref[...], sh_ref[...], sc_ref[...]).astype(BF16)
        cos, s_up, s_dn = (
            jnp.concatenate(
                [rt_ref[t, r:r + 1, :] + ct_ref[t]
                 for r in range(r0 // GRID_W, (r0 + chunk) // GRID_W)], axis=0)
            for t in range(3))

        def rope(t):
            return (t * cos + pltpu.roll(t, HEAD_DIM - quarter, 1) * s_up
                    + pltpu.roll(t, quarter, 1) * s_dn)

        q = jnp.dot(hb, w_ref[:, :Q_DIM], preferred_element_type=F32)
        for j in range(N_HEADS):
            sl = slice(j * HEAD_DIM, (j + 1) * HEAD_DIM)
            q_ref[rs, sl] = (rope(q[:, sl]) * qscale).astype(BF16)
        k = jnp.dot(hb, w_ref[:, Q_DIM:Q_DIM + KV_DIM], preferred_element_type=F32)
        for j in range(N_KV_HEADS):
            sl = slice(j * HEAD_DIM, (j + 1) * HEAD_DIM)
            k_ref[rs, sl] = rope(k[:, sl]).astype(BF16)
        v = jnp.dot(hb, w_ref[:, Q_DIM + KV_DIM:Q_DIM + 2 * KV_DIM], preferred_element_type=F32)
        v_ref[rs, :] = v.astype(BF16)
        u = jnp.dot(hb, w_ref[:, Q_DIM + 2 * KV_DIM:], preferred_element_type=F32).astype(BF16)
        for g in range(N_FGROUPS):
            pq = jnp.dot(u[:, g * FG:(g + 1) * FG], ab_ref[g], preferred_element_type=F32)
            per_group = FG // LANES
            for t in range(2 * per_group):
                plane = (t // per_group) * (F_DIM // LANES) + g * per_group + t % per_group
                _store_dft_rows(z_ref, plane, pq[:, t * LANES:(t + 1) * LANES], r0 // DFT_NB)


def _project_latent(x, nw, shift, scale, w_in_b, ab, tm=512):
    n, d = x.shape
    row_tab, col_tab = _rope_tables(n)
    row = lambda i: (i, 0)
    const2 = lambda i: (0, 0)
    return pl.pallas_call(
        _proj_kernel,
        out_shape=(jax.ShapeDtypeStruct((n, Q_DIM), BF16),
                   jax.ShapeDtypeStruct((n, KV_DIM), BF16),
                   jax.ShapeDtypeStruct((n, KV_DIM), BF16),
                   jax.ShapeDtypeStruct((2 * F_DIM // LANES, DFT_NB // SUBLANES_F32,
                                         (n // DFT_NB) * SUBLANES_F32, LANES), F32)),
        grid=(n // tm,),
        in_specs=[pl.BlockSpec((tm, d), row),
                  pl.BlockSpec((1, d), const2), pl.BlockSpec((1, d), const2),
                  pl.BlockSpec((1, d), const2),
                  pl.BlockSpec((d, D_IN), const2, pipeline_mode=pl.Buffered(1)),
                  pl.BlockSpec((N_FGROUPS, FG, 2 * FG), lambda i: (0, 0, 0),
                               pipeline_mode=pl.Buffered(1)),
                  pl.BlockSpec((3, tm // GRID_W, HEAD_DIM), lambda i: (0, i, 0)),
                  pl.BlockSpec((3, GRID_W, HEAD_DIM), lambda i: (0, 0, 0))],
        out_specs=(pl.BlockSpec((tm, Q_DIM), row), pl.BlockSpec((tm, KV_DIM), row),
                   pl.BlockSpec((tm, KV_DIM), row),
                   pl.BlockSpec((2 * F_DIM // LANES, DFT_NB // SUBLANES_F32,
                                 (tm // DFT_NB) * SUBLANES_F32, LANES), lambda i: (0, 0, i, 0))),
        compiler_params=_cparams("arbitrary"),
        name="project_latent",
    )(x, nw, shift, scale, w_in_b, ab, row_tab, col_tab)


def _ctx_kernel(x_ref, nw_ref, sh_ref, sc_ref, w_ref, k_ref, v_ref):
    hb = _rms_mod(x_ref[...], nw_ref[...], sh_ref[...], sc_ref[...]).astype(BF16)
    kv = jnp.dot(hb, w_ref[...], preferred_element_type=F32)
    k_ref[...] = kv[:, :KV_DIM].astype(BF16)
    v_ref[...] = kv[:, KV_DIM:].astype(BF16)


def _project_context(ctx, nw, shift, scale, w_kv_b):
    m, d = ctx.shape
    full = lambda shp: pl.BlockSpec(shp, lambda i: (0, 0))
    return pl.pallas_call(
        _ctx_kernel,
        out_shape=(jax.ShapeDtypeStruct((m, KV_DIM), BF16),
                   jax.ShapeDtypeStruct((m, KV_DIM), BF16)),
        grid=(1,),
        in_specs=[full((m, d)), full((1, d)), full((1, d)), full((1, d)), full((d, 2 * KV_DIM))],
        out_specs=(full((m, KV_DIM)), full((m, KV_DIM))),
        compiler_params=_cparams("arbitrary"),
        name="project_context",
    )(ctx, nw, shift, scale, w_kv_b)


def _attn_kernel(sink_ref, q_ref, kp_ref, km_ref, kn_ref, vp_ref, vm_ref, vn_ref,
                 kc_ref, vc_ref, lo_ref, hi_ref, o_ref, *, n_total, tq):
    i = pl.program_id(0)
    nsub = tq // WINDOW
    last_blk = n_total // WINDOW - 1
    kwin = jnp.concatenate([kp_ref[...], km_ref[...], kn_ref[...]], axis=0)
    vwin = jnp.concatenate([vp_ref[...], vm_ref[...], vn_ref[...]], axis=0)
    rows = GROUP * WINDOW
    span = 3 * WINDOW
    head_of_row = lax.broadcasted_iota(I32, (rows, 1), 0) // WINDOW
    ones_loc = jnp.ones((span, HEAD_DIM), BF16)
    ones_ctx = jnp.ones((kc_ref.shape[0], HEAD_DIM), BF16)
    nt = (((1,), (1,)), ((), ()))
    for b in range(nsub):
        blk = i * nsub + b
        bias_lo = lo_ref[(blk == 0).astype(I32)]
        bias_hi = hi_ref[(blk == last_blk).astype(I32)]
        for h in range(N_KV_HEADS):
            hs = slice(h * HEAD_DIM, (h + 1) * HEAD_DIM)
            qs = jnp.concatenate(
                [q_ref[b * WINDOW:(b + 1) * WINDOW,
                       (h * GROUP + g) * HEAD_DIM:(h * GROUP + g + 1) * HEAD_DIM]
                 for g in range(GROUP)], axis=0)
            kw = kwin[b * WINDOW:b * WINDOW + span, hs]
            vw = vwin[b * WINDOW:b * WINDOW + span, hs]
            s_loc = lax.dot_general(qs, kw, nt, preferred_element_type=F32)
            s_ctx = lax.dot_general(qs, kc_ref[:, hs], nt, preferred_element_type=F32)
            parts = [s_loc[:, :WINDOW] + bias_lo, s_loc[:, WINDOW:2 * WINDOW],
                     s_loc[:, 2 * WINDOW:] + bias_hi, s_ctx]
            sink_col = jnp.zeros((rows, 1), F32)
            for g in range(GROUP):
                sink_col = jnp.where(head_of_row == g, sink_ref[h * GROUP + g] * LOG2_E, sink_col)
            blocks = parts[:3] + [s_ctx[:, t * WINDOW:(t + 1) * WINDOW]
                                  for t in range(s_ctx.shape[1] // WINDOW)]
            widest = blocks[0]
            for blk_scores in blocks[1:]:
                widest = jnp.maximum(widest, blk_scores)
            m = jnp.maximum(sink_col, jnp.max(widest, axis=1, keepdims=True))
            p = [jnp.exp2(part - m).astype(BF16) for part in parts]
            ov = (jnp.dot(jnp.concatenate(p[:3], axis=1), jnp.concatenate([vw, ones_loc], axis=1),
                          preferred_element_type=F32)
                  + jnp.dot(p[3], jnp.concatenate([vc_ref[:, hs], ones_ctx], axis=1),
                            preferred_element_type=F32))
            o = ov[:, :HEAD_DIM] / (ov[:, HEAD_DIM:] + jnp.exp2(sink_col - m))
            for g in range(GROUP):
                o_ref[b * WINDOW:(b + 1) * WINDOW,
                      (h * GROUP + g) * HEAD_DIM:(h * GROUP + g + 1) * HEAD_DIM] = (
                    o[g * WINDOW:(g + 1) * WINDOW].astype(BF16))


def _band_biases():
    r = jnp.arange(GROUP * WINDOW, dtype=I32)[:, None] % WINDOW
    c = jnp.arange(WINDOW, dtype=I32)[None, :]
    masked = jnp.full((GROUP * WINDOW, WINDOW), NEG_INF, F32)
    lo = jnp.stack([jnp.where(c >= r, 0.0, NEG_INF).astype(F32), masked])
    hi = jnp.stack([jnp.where(c <= r, 0.0, NEG_INF).astype(F32), masked])
    return lo, hi


def _attention(q, k, v, kc, vc, sink, tq=512):
    n = q.shape[0]
    bias_lo, bias_hi = _band_biases()
    bias_spec = pl.BlockSpec(bias_lo.shape, lambda i, s: (0, 0, 0))
    m = kc.shape[0]
    nsub = tq // WINDOW
    nblk = n // WINDOW
    prev = lambda i, s: (jnp.maximum(i * nsub - 1, 0), 0)
    main = lambda i, s: (i, 0)
    nxt = lambda i, s: (jnp.minimum(i * nsub + nsub, nblk - 1), 0)
    const = lambda i, s: (0, 0)
    kv_specs = [pl.BlockSpec((WINDOW, KV_DIM), prev), pl.BlockSpec((tq, KV_DIM), main),
                pl.BlockSpec((WINDOW, KV_DIM), nxt)]
    return pl.pallas_call(
        functools.partial(_attn_kernel, n_total=n, tq=tq),
        out_shape=jax.ShapeDtypeStruct((n, Q_DIM), BF16),
        grid_spec=pltpu.PrefetchScalarGridSpec(
            num_scalar_prefetch=1,
            grid=(n // tq,),
            in_specs=[pl.BlockSpec((tq, Q_DIM), main)] + kv_specs + kv_specs
                     + [pl.BlockSpec((m, KV_DIM), const), pl.BlockSpec((m, KV_DIM), const),
                        bias_spec, bias_spec],
            out_specs=pl.BlockSpec((tq, Q_DIM), main)),
        compiler_params=_cparams("arbitrary"),
        name="banded_attention",
    )(sink, q, k, k, k, v, v, v, kc, vc, bias_lo, bias_hi)


def _dft_tables(na, nb):
    n = na * nb
    s1 = 2.0 ** (-(int(math.log2(na)) // 2))
    s2 = (1.0 / math.sqrt(n)) / s1
    ka = jnp.arange(na, dtype=I32)
    th_tw = ((jnp.arange(nb, dtype=I32)[:, None] * ka[None, :]) % n).astype(F32) * (2.0 * math.pi / n)
    th_f = ((ka[:, None] * ka[None, :]) % na).astype(F32) * (2.0 * math.pi / na)
    twr, twi = (jnp.cos(th_tw) * s1)[:, :, None], (-jnp.sin(th_tw) * s1)[:, :, None]
    fr, fi = jnp.cos(th_f)[None, :, :], -jnp.sin(th_f)[None, :, :]
    tr, ti = twr * fr - twi * fi, twr * fi + twi * fr
    t1 = jnp.concatenate([jnp.concatenate([tr, ti], axis=-1),
                          jnp.concatenate([ti, -tr], axis=-1)], axis=-2).astype(BF16)
    kb = jnp.arange(nb, dtype=I32)
    th2 = ((kb[:, None] * kb[None, :]) % nb).astype(F32) * (2.0 * math.pi / nb)
    t2 = (jnp.concatenate([jnp.cos(th2), jnp.sin(th2)], axis=-1) * s2).astype(BF16)
    return t1, t2


def _dft1_kernel(z_ref, t_ref, y_ref):
    planes, _, rows, _ = z_ref.shape
    step = SUBLANES_F32
    half, na = planes // 2, rows // step
    for j in range(step):
        sel = pl.ds(j, na, stride=step)
        p = jnp.concatenate([z_ref[cc, 0, sel, :] for cc in range(half)], axis=1)
        q = jnp.concatenate([z_ref[cc, 0, sel, :] for cc in range(half, planes)], axis=1)
        xs = jnp.concatenate([p, q], axis=0).astype(BF16)
        y = jnp.dot(t_ref[j], xs, preferred_element_type=F32)
        for cc in range(half):
            y_ref[cc, 0, sel, :] = y[:na, cc * LANES:(cc + 1) * LANES]
            y_ref[half + cc, 0, sel, :] = y[na:, cc * LANES:(cc + 1) * LANES]


def _dft2_kernel(y_ref, t_ref, o_ref):
    planes, groups, rows, _ = y_ref.shape
    step = SUBLANES_F32
    half, nb = planes // 2, groups * step

    def tokens_b(cc, j):
        return y_ref[cc, :, j * step:(j + 1) * step, :].reshape(nb, LANES)

    for j in range(rows // step):
        yr = jnp.concatenate([tokens_b(cc, j) for cc in range(half)], axis=1)
        yi = jnp.concatenate([tokens_b(cc, j) for cc in range(half, planes)], axis=1)
        xs = jnp.concatenate([yr, yi], axis=0).astype(BF16)
        out = jnp.dot(t_ref[...], xs, preferred_element_type=F32)
        for cc in range(half):
            o_ref[cc, 0, pl.ds(j, nb, stride=step), :] = out[:, cc * LANES:(cc + 1) * LANES]


def _position_dft(z):
    planes, groups, rows, _ = z.shape
    step = SUBLANES_F32
    nb, na = groups * step, rows // step
    t1, t2 = _dft_tables(na, nb)
    y = pl.pallas_call(
        _dft1_kernel,
        out_shape=jax.ShapeDtypeStruct(z.shape, F32),
        grid=(groups,),
        in_specs=[pl.BlockSpec((planes, 1, rows, LANES), lambda i: (0, i, 0, 0)),
                  pl.BlockSpec((step, 2 * na, 2 * na), lambda i: (i, 0, 0))],
        out_specs=pl.BlockSpec((planes, 1, rows, LANES), lambda i: (0, i, 0, 0)),
        compiler_params=_cparams("arbitrary"),
        name="position_dft_stage1",
    )(z, t1)
    return pl.pallas_call(
        _dft2_kernel,
        out_shape=jax.ShapeDtypeStruct((planes // 2, na // step, nb * step, LANES), F32),
        grid=(na // step,),
        in_specs=[pl.BlockSpec((planes, groups, step * step, LANES), lambda i: (0, 0, i, 0)),
                  pl.BlockSpec((nb, 2 * nb), lambda i: (0, 0))],
        out_specs=pl.BlockSpec((planes // 2, 1, nb * step, LANES), lambda i: (0, i, 0, 0)),
        compiler_params=_cparams("arbitrary"),
        name="position_dft_stage2",
    )(y, t2)


def _outproj_kernel(ax_ref, fx_ref, x_ref, wo_ref, gate_ref, nw_ref, sh_ref, sc_ref, wr_ref,
                    x1_ref, g_ref, aff_ref):
    planes, groups, rows, _ = fx_ref.shape
    na = groups * SUBLANES_F32
    n_kb = rows // SUBLANES_F32
    kb_per_chunk = max(1, n_kb // OUTPROJ_CHUNKS)
    for first_kb in range(0, n_kb, kb_per_chunk):
        rs = slice(first_kb * na, (first_kb + kb_per_chunk) * na)
        fx = jnp.concatenate(
            [jnp.concatenate(
                [fx_ref[cc, :, kb * SUBLANES_F32:(kb + 1) * SUBLANES_F32, :].reshape(na, LANES)
                 for kb in range(first_kb, first_kb + kb_per_chunk)], axis=0)
             for cc in range(planes)], axis=1).astype(BF16)
        acc = (jnp.dot(ax_ref[rs, :], wo_ref[:Q_DIM, :], preferred_element_type=F32)
               + jnp.dot(fx, wo_ref[Q_DIM:, :], preferred_element_type=F32))
        x1 = x_ref[rs, :] + gate_ref[...] * acc
        x1_ref[rs, :] = x1
        gb = _rms_mod(x1, nw_ref[...], sh_ref[...], sc_ref[...]).astype(BF16)
        g_ref[rs, :] = gb
        logits = jnp.dot(gb, wr_ref[...], preferred_element_type=F32)
        e = jnp.exp(logits - jnp.max(logits, axis=1, keepdims=True))
        aff_ref[rs, :] = e / jnp.sum(e, axis=1, keepdims=True)


def _output_projection(ax, fx, x, wo_b, gate, nw, shift, scale, wr_b, tm=512):
    n, d = x.shape
    ne = wr_b.shape[1]
    row = lambda i: (i, 0)
    const = lambda i: (0, 0)
    vec = pl.BlockSpec((1, d), const)
    return pl.pallas_call(
        _outproj_kernel,
        out_shape=(jax.ShapeDtypeStruct((n, d), F32), jax.ShapeDtypeStruct((n, d), BF16),
                   jax.ShapeDtypeStruct((n, ne), F32)),
        grid=(n // tm,),
        in_specs=[pl.BlockSpec((tm, Q_DIM), row),
                  pl.BlockSpec((fx.shape[0], fx.shape[1], tm // (fx.shape[1] * SUBLANES_F32)
                                * SUBLANES_F32, LANES), lambda i: (0, 0, i, 0)),
                  pl.BlockSpec((tm, d), row),
                  pl.BlockSpec((Q_DIM + F_DIM, d), const, pipeline_mode=pl.Buffered(1)),
                  vec, vec, vec, vec, pl.BlockSpec((d, ne), const)],
        out_specs=(pl.BlockSpec((tm, d), row), pl.BlockSpec((tm, d), row),
                   pl.BlockSpec((tm, ne), row)),
        compiler_params=_cparams("arbitrary"),
        name="output_projection",
    )(ax, fx, x, wo_b, gate, nw, shift, scale, wr_b)


def _route_kernel(aff_ref, posm_ref, cnt_ref, off_ref, *, cap, n_exp, n_chunk):
    aff_all = aff_ref[...]
    blocks = [aff_all[e * n_chunk:(e + 1) * n_chunk] for e in range(n_exp)]

    def total(mask):
        s = jnp.sum(jnp.where(mask, 1.0, 0.0), axis=0, keepdims=True)
        return jnp.sum(s, axis=1, keepdims=True)

    def search(it, thr_bits):
        bit = jnp.left_shift(jnp.int32(1), 30 - it)
        out = []
        for e in range(n_exp):
            cand = thr_bits[e] | bit
            enough = total(blocks[e] >= lax.bitcast_convert_type(cand, F32)) >= cap
            out.append(jnp.where(enough, cand, thr_bits[e]))
        return tuple(out)

    thr_bits = lax.fori_loop(0, 31, search, tuple(jnp.zeros((1, 1), I32) for _ in range(n_exp)))
    thr = [lax.bitcast_convert_type(t, F32) for t in thr_bits]

    rl = lax.broadcasted_iota(I32, (LANES, LANES), 0)
    cl = lax.broadcasted_iota(I32, (LANES, LANES), 1)
    before = jnp.where(rl < cl, 1.0, 0.0).astype(BF16)
    ones = jnp.ones((LANES, LANES), BF16)
    rc = lax.broadcasted_iota(I32, (n_chunk, n_chunk), 0)
    cc = lax.broadcasted_iota(I32, (n_chunk, n_chunk), 1)
    earlier = jnp.where(cc < rc, 1.0, 0.0).astype(BF16)

    def excl_cumsum(mask):
        mb = jnp.where(mask, 1.0, 0.0).astype(BF16)
        within = jnp.dot(mb, before, preferred_element_type=F32)
        rowtot = jnp.dot(mb, ones, preferred_element_type=F32)
        choff = jnp.dot(earlier, rowtot.astype(BF16), preferred_element_type=F32)
        return within + choff, rowtot, choff

    for e in range(n_exp):
        gt = blocks[e] > thr[e]
        eq = blocks[e] == thr[e]
        need = cap - total(gt)
        tie_rank, _, _ = excl_cumsum(eq)
        sel = gt | (eq & (tie_rank < need))
        pos, rowtot, choff = excl_cumsum(sel)
        rs = slice(e * n_chunk, (e + 1) * n_chunk)
        posm_ref[rs, :] = jnp.where(sel, pos.astype(I32), NOT_SELECTED)
        cnt_ref[rs, :] = rowtot.astype(I32)
        off_ref[rs, :] = choff.astype(I32)


def _route(aff2, cap, n_exp):
    rows = aff2.shape[0]
    n_chunk = rows // n_exp
    spec = pl.BlockSpec((rows, LANES), lambda i: (0, 0))
    shp = jax.ShapeDtypeStruct((rows, LANES), I32)
    return pl.pallas_call(
        functools.partial(_route_kernel, cap=cap, n_exp=n_exp, n_chunk=n_chunk),
        out_shape=(shp, shp, shp),
        grid=(1,),
        in_specs=[spec],
        out_specs=(spec, spec, spec),
        compiler_params=_cparams("arbitrary"),
        name="expert_choice_routing",
    )(aff2)


def _window_matches(posm, firsts, win_starts):
    r = lax.broadcasted_iota(I32, (WINDOW_ROWS, TOK_TILE), 0)
    out = []
    for e, (first, start) in enumerate(zip(firsts, win_starts)):
        pm = posm[e:e + 1, :]
        pm = jnp.where((pm >= first) & (pm < first + PIECE), pm, NOT_SELECTED)
        out.append((pm - start) == r)
    return out


def _align_down(v):
    return (v // ROW_ALIGN) * ROW_ALIGN


def _for_window_size(rows_needed, fn):
    below = None
    for k, size in enumerate(WINDOW_SIZES):
        fits = rows_needed <= size if k + 1 < len(WINDOW_SIZES) else None
        cond = fits if below is None else (below if fits is None else jnp.logical_and(below, fits))
        pl.when(cond)(functools.partial(fn, size))
        below = rows_needed > size


def _gather_kernel(off_ref, cnt_ref, g_ref, posm_ref, aff_ref, xs_hbm,
                   stg, tail, sem, npass_ref, sent_ref, *, n_exp, n_chunk, cap):
    c = pl.program_id(0)
    stride = cap + WINDOW_ROWS

    @pl.when(c == 0)
    def _():
        npass_ref[0] = 0
        tail[...] = jnp.zeros_like(tail)

    def window_copy(slot, e, dst, size=WINDOW_ROWS):
        return pltpu.make_async_copy(stg.at[slot, pl.ds(e * WINDOW_ROWS, size)],
                                     xs_hbm.at[pl.ds(dst, size)], sem.at[slot, e])

    def start_window(slot, e, dst, rows_used):
        _for_window_size(rows_used, lambda size: window_copy(slot, e, dst, size).start())
        sent_ref[slot * n_exp + e] = rows_used

    def wait_window(slot, e):
        _for_window_size(sent_ref[slot * n_exp + e],
                         lambda size: window_copy(slot, e, 0, size).wait())

    @pl.when(c == 0)
    def _():
        stg[0] = jnp.zeros(stg.shape[1:], stg.dtype)
        for e in range(n_exp):
            window_copy(0, e, e * stride + cap).start()
        for e in range(n_exp):
            window_copy(0, e, 0).wait()

    most = cnt_ref[c]
    for e in range(1, n_exp):
        most = jnp.maximum(most, cnt_ref[e * n_chunk + c])
    passes = (most + PIECE - 1) // PIECE

    def one_pass(p, carry):
        done = npass_ref[0]
        slot = done % 2
        posm = posm_ref[0]
        aff = aff_ref[0]
        firsts, win_starts, next_shift, rows_used = [], [], [], []
        for e in range(n_exp):
            o, n_e = off_ref[e * n_chunk + c], cnt_ref[e * n_chunk + c]
            first = o + jnp.minimum(p * PIECE, n_e)
            after = o + jnp.minimum((p + 1) * PIECE, n_e)
            firsts.append(first)
            win_starts.append(_align_down(first))
            next_shift.append(_align_down(after) - _align_down(first))
            rows_used.append(after - _align_down(first))
        matches = _window_matches(posm, firsts, win_starts)
        onehot = jnp.concatenate([jnp.where(mt, 1.0, 0.0).astype(BF16) for mt in matches], axis=0)
        new = jnp.dot(onehot, g_ref[...], preferred_element_type=F32)
        lane = lax.broadcasted_iota(I32, (WINDOW_ROWS, GATE_LANES), 1)
        for e in range(n_exp):
            lo = e * WINDOW_ROWS
            gate = jnp.sum(jnp.where(matches[e], aff[e:e + 1, :], 0.0), axis=1, keepdims=True)
            hi = gate.astype(BF16).astype(F32)
            mid = (gate - hi).astype(BF16).astype(F32)
            low = gate - hi - mid
            parts = jnp.where(lane == 0, hi, jnp.where(lane == 1, mid,
                                                       jnp.where(lane == 2, low, 0.0)))
            rows = jnp.concatenate([new[lo:lo + WINDOW_ROWS], parts], axis=1)
            stg[slot, lo:lo + ROW_ALIGN] = (rows[:ROW_ALIGN] + tail[e].astype(F32)).astype(BF16)
            stg[slot, lo + ROW_ALIGN:lo + WINDOW_ROWS] = rows[ROW_ALIGN:].astype(BF16)

        dsts = [pl.multiple_of(e * stride + win_starts[e], ROW_ALIGN) for e in range(n_exp)]

        @pl.when(done > 0)
        def _():
            for e in range(n_exp):
                wait_window(1 - slot, e)
                start_window(slot, e, dsts[e], rows_used[e])

        @pl.when(done == 0)
        def _():
            for e in range(n_exp):
                start_window(slot, e, dsts[e], rows_used[e])

        for e in range(n_exp):
            src = pl.ds(pl.multiple_of(e * WINDOW_ROWS + next_shift[e], ROW_ALIGN), ROW_ALIGN)
            tail[e] = stg[slot, src, :]
        npass_ref[0] = done + 1
        return carry

    lax.fori_loop(0, passes, one_pass, 0)

    @pl.when(c == pl.num_programs(0) - 1)
    def _():
        done = npass_ref[0]

        @pl.when(done > 0)
        def _():
            for e in range(n_exp):
                wait_window((done - 1) % 2, e)


def _gather(off, cnt, g, posm_t, aff_t, cap):
    n, d = g.shape
    n_chunk, n_exp, _ = posm_t.shape
    rows = n_exp * (cap + WINDOW_ROWS)
    tile3 = lambda i, o, c: (i, 0, 0)
    return pl.pallas_call(
        functools.partial(_gather_kernel, n_exp=n_exp, n_chunk=n_chunk, cap=cap),
        out_shape=jax.ShapeDtypeStruct((rows, d + GATE_LANES), BF16),
        grid_spec=pltpu.PrefetchScalarGridSpec(
            num_scalar_prefetch=2,
            grid=(n_chunk,),
            in_specs=[pl.BlockSpec((TOK_TILE, d), lambda i, o, c: (i, 0)),
                      pl.BlockSpec((1, n_exp, TOK_TILE), tile3),
                      pl.BlockSpec((1, n_exp, TOK_TILE), tile3)],
            out_specs=pl.BlockSpec(memory_space=pl.ANY),
            scratch_shapes=[pltpu.VMEM((2, n_exp * WINDOW_ROWS, d + GATE_LANES), BF16),
                            pltpu.VMEM((n_exp, ROW_ALIGN, d + GATE_LANES), BF16),
                            pltpu.SemaphoreType.DMA((2, n_exp)),
                            pltpu.SMEM((1,), I32),
                            pltpu.SMEM((2 * n_exp,), I32)]),
        compiler_params=_cparams("arbitrary"),
        name="gather_expert_tokens",
    )(off, cnt, g, posm_t, aff_t)


def _expert_kernel(xs_ref, wg_ref, wu_ref, wd_ref, y_ref, h_ref, gate_ref, *, nf, tf, tn):
    j = pl.program_id(1)
    d = wg_ref.shape[1]
    cap = xs_ref.shape[1]
    halves = [slice(r * (cap // EXPERT_ROW_SPLIT), (r + 1) * (cap // EXPERT_ROW_SPLIT))
              for r in range(EXPERT_ROW_SPLIT)]

    @pl.when(j == 0)
    def _():
        parts = xs_ref[0, :, d:].astype(F32)
        gate = parts[:, 0:1] + parts[:, 1:2] + parts[:, 2:3]
        gate_ref[...] = jnp.broadcast_to(gate, gate_ref.shape)

    @pl.when(j < nf)
    def _():
        wg = wg_ref[0].astype(BF16)
        wu = wu_ref[0].astype(BF16)
        for rs in halves:
            xs = xs_ref[0, rs, :d]
            gp = jnp.dot(xs, wg, preferred_element_type=F32)
            up = jnp.dot(xs, wu, preferred_element_type=F32)
            h = ((gp * jax.nn.sigmoid(gp)) * up).astype(BF16)
            for jj in range(nf):
                @pl.when(j == jj)
                def _():
                    h_ref[rs, jj * tf:(jj + 1) * tf] = h

    @pl.when(j >= nf)
    def _():
        wd = wd_ref[0].astype(BF16)
        for rs in halves:
            y = jnp.dot(h_ref[rs, :], wd, preferred_element_type=F32)
            y_ref[0, rs, :] = (y * gate_ref[rs, 0:1]).astype(BF16)


def _experts(xs3, w_gate, w_up, w_down, cap, tf=512, tn=512):
    n_exp, d, ff = w_gate.shape
    tf, tn = min(tf, ff), min(tn, d)
    nf, nn = ff // tf, d // tn

    def ahead(e, j):
        return jnp.minimum(e + (j >= nf).astype(I32), n_exp - 1)

    def ff_tile(e, j):
        return jnp.where(j >= nf, jnp.where(e == n_exp - 1, nf - 1, 0), j)

    return pl.pallas_call(
        functools.partial(_expert_kernel, nf=nf, tf=tf, tn=tn),
        out_shape=jax.ShapeDtypeStruct((n_exp, cap, d), BF16),
        grid=(n_exp, nf + nn),
        in_specs=[pl.BlockSpec((1, cap, d + GATE_LANES), lambda e, j: (ahead(e, j), 0, 0)),
                  pl.BlockSpec((1, d, tf), lambda e, j: (ahead(e, j), 0, ff_tile(e, j))),
                  pl.BlockSpec((1, d, tf), lambda e, j: (ahead(e, j), 0, ff_tile(e, j))),
                  pl.BlockSpec((1, ff, tn), lambda e, j: (e, 0, jnp.maximum(j - nf, 0)))],
        out_specs=pl.BlockSpec((1, cap, tn), lambda e, j: (e, 0, jnp.maximum(j - nf, 0))),
        scratch_shapes=[pltpu.VMEM((cap, ff), BF16), pltpu.VMEM((cap, LANES), F32)],
        compiler_params=_cparams("arbitrary", "arbitrary"),
        name="swiglu_experts",
    )(xs3, w_gate, w_up, w_down)


def _combine_kernel(off_ref, cnt_ref, y_hbm, posm_ref, x1_ref, gate_ref, nw_ref, o_ref,
                    stg, acc_ref, sem, *, n_exp, n_chunk, cap):
    c = pl.program_id(0)
    last_start = n_exp * cap - WINDOW_ROWS

    def firsts_of(tile, p):
        return [off_ref[e * n_chunk + tile] + p * PIECE for e in range(n_exp)]

    def starts_of(tile, p):
        return [pl.multiple_of(jnp.minimum(e * cap + _align_down(first), last_start), ROW_ALIGN)
                for e, first in enumerate(firsts_of(tile, p))]

    def rows_used_of(tile, p):
        out = []
        for e, (first, start) in enumerate(zip(firsts_of(tile, p), starts_of(tile, p))):
            left = jnp.clip(cnt_ref[e * n_chunk + tile] - p * PIECE, 0, PIECE)
            out.append(e * cap + first + left - start)
        return out

    def piece_copy(slot, e, src, size=WINDOW_ROWS):
        return pltpu.make_async_copy(y_hbm.at[pl.ds(src, size)],
                                     stg.at[slot, pl.ds(e * WINDOW_ROWS, size)], sem.at[slot])

    def fetch(tile, p, slot):
        for e, (src, used) in enumerate(zip(starts_of(tile, p), rows_used_of(tile, p))):
            _for_window_size(used, lambda size: piece_copy(slot, e, src, size).start())

    def wait_fetch(tile, p, slot):
        for e, used in enumerate(rows_used_of(tile, p)):
            _for_window_size(used, lambda size: piece_copy(slot, e, 0, size).wait())

    def contribution(p, slot, tokens):
        starts = [s - e * cap for e, s in enumerate(starts_of(c, p))]
        matches = _window_matches(posm_ref[0], firsts_of(c, p), starts)
        onehot = jnp.concatenate([jnp.where(mt[:, tokens], 1.0, 0.0).astype(BF16)
                                  for mt in matches], axis=0)
        return lax.dot_general(onehot, stg[slot], (((0,), (0,)), ((), ())),
                               preferred_element_type=F32)

    def finish(tokens, moe):
        x2 = x1_ref[tokens, :] + gate_ref[...] * moe
        ms = jnp.mean(x2 * x2, axis=-1, keepdims=True)
        o_ref[tokens, :] = x2 * lax.rsqrt(ms + RMS_EPS) * nw_ref[...]

    slot = c % 2

    @pl.when(c == 0)
    def _():
        stg[...] = jnp.zeros_like(stg)
        fetch(0, 0, 0)

    @pl.when(c + 1 < pl.num_programs(0))
    def _():
        fetch(c + 1, 0, 1 - slot)

    most = cnt_ref[c]
    for e in range(1, n_exp):
        most = jnp.maximum(most, cnt_ref[e * n_chunk + c])
    passes = (most + PIECE - 1) // PIECE
    wait_fetch(c, 0, slot)

    @pl.when(passes <= 1)
    def _():
        for t0 in range(0, TOK_TILE, LANES):
            tokens = slice(t0, t0 + LANES)
            finish(tokens, contribution(0, slot, tokens))

    @pl.when(passes > 1)
    def _():
        everyone = slice(0, TOK_TILE)
        acc_ref[...] = contribution(0, slot, everyone)

        def extra_pass(p, carry):
            fetch(c, p, slot)
            wait_fetch(c, p, slot)
            acc_ref[...] += contribution(p, slot, everyone)
            return carry

        lax.fori_loop(1, passes, extra_pass, 0)
        finish(everyone, acc_ref[...])


def _combine(off, cnt, y, posm_t, x1, gate, nw, cap):
    n, d = x1.shape
    n_chunk, n_exp, _ = posm_t.shape
    return pl.pallas_call(
        functools.partial(_combine_kernel, n_exp=n_exp, n_chunk=n_chunk, cap=cap),
        out_shape=jax.ShapeDtypeStruct((n, d), F32),
        grid_spec=pltpu.PrefetchScalarGridSpec(
            num_scalar_prefetch=2,
            grid=(n_chunk,),
            in_specs=[pl.BlockSpec(memory_space=pl.ANY),
                      pl.BlockSpec((1, n_exp, TOK_TILE), lambda i, o, c: (i, 0, 0)),
                      pl.BlockSpec((TOK_TILE, d), lambda i, o, c: (i, 0)),
                      pl.BlockSpec((1, d), lambda i, o, c: (0, 0)),
                      pl.BlockSpec((1, d), lambda i, o, c: (0, 0))],
            out_specs=pl.BlockSpec((TOK_TILE, d), lambda i, o, c: (i, 0)),
            scratch_shapes=[pltpu.VMEM((2, n_exp * WINDOW_ROWS, d), BF16),
                            pltpu.VMEM((TOK_TILE, d), F32),
                            pltpu.SemaphoreType.DMA((2,))]),
        compiler_params=_cparams("arbitrary"),
        name="combine_experts_final_norm",
    )(off, cnt, y, posm_t, x1, gate, nw)


def kernel(x, c, ctx, c_ctx, w_mod, b_mod, norm_mix, w_in, sink, w_fourier, w_out, norm_ffn,
           w_router, w_gate, w_up, w_down, norm_final):
    assert x.shape[0] == 1 and w_mod.shape[0] == 1
    n, d = x.shape[1], x.shape[2]
    xl, cx = x[0], ctx[0]
    n_exp = w_router.shape[2]
    cap = max(1, CAP_FACTOR * n // n_exp)
    n_chunk = n // LANES
    n_tile = n // TOK_TILE
    per_tile = TOK_TILE // LANES

    mod = _modulation(jnp.stack([c[0], c_ctx], axis=1), w_mod[0], b_mod[0][None, :])
    mx = mod[0].reshape(6, 1, d)
    mc = mod[1].reshape(6, 1, d)

    w_in_b = w_in[0].astype(BF16)
    nmix = norm_mix[0][None, :]
    ab = _fold_channel_dft(w_fourier[0])
    q, k, v, z = _project_latent(xl, nmix, mx[0], mx[1], w_in_b, ab)
    kc, vc = _project_context(cx, nmix, mc[0], mc[1], w_in_b[:, Q_DIM:Q_DIM + 2 * KV_DIM])
    ax = _attention(q, k, v, kc, vc, sink[0])
    fx = _position_dft(z)

    x1, g, aff = _output_projection(ax, fx, xl, w_out[0].astype(BF16), mx[2], norm_ffn[0][None, :],
                                    mx[3], mx[4], w_router[0].astype(BF16))

    aff2 = aff.T.reshape(n_exp * n_chunk, LANES)
    posm, cnt, off = _route(aff2, cap, n_exp)
    by_tile = lambda a: a.reshape(n_exp, n_tile, TOK_TILE).transpose(1, 0, 2)
    posm_t, aff_t = by_tile(posm), by_tile(aff2)
    cnt1 = cnt[:, 0].reshape(n_exp * n_tile, per_tile).sum(axis=1)
    off1 = off[:, 0].reshape(n_exp * n_tile, per_tile)[:, 0]

    xs = _gather(off1, cnt1, g, posm_t, aff_t, cap)
    y = _experts(xs.reshape(n_exp, cap + WINDOW_ROWS, d + GATE_LANES),
                 w_gate[0], w_up[0], w_down[0], cap)
    out = _combine(off1, cnt1, y.reshape(n_exp * cap, d), posm_t, x1, mx[5], norm_final[None, :],
                   cap)
    return out[None]
```

```python
import functools
import math

import jax
import jax.numpy as jnp
from jax import lax
from jax.experimental import pallas as pl
from jax.experimental.pallas import tpu as pltpu

F32 = jnp.float32
BF16 = jnp.bfloat16
I32 = jnp.int32

HEAD_DIM = 128
N_HEADS = 8
N_KV_HEADS = 2
GROUP = N_HEADS // N_KV_HEADS
WINDOW = 128
Q_DIM = N_HEADS * HEAD_DIM
KV_DIM = N_KV_HEADS * HEAD_DIM
N_FGROUPS = 4
FG = 256
F_DIM = N_FGROUPS * FG
D_IN = Q_DIM + 2 * KV_DIM + F_DIM
N_EXPERTS = 16
CAP_FACTOR = 2
GRID_W = 64
ROPE_THETA = 10000.0
RMS_EPS = 1e-6
NEG_INF = -1e30
LOG2_E = math.log2(math.e)

LANES = 128
SUBLANES_F32 = 8
VMEM_LIMIT_BYTES = 56 * 1024 * 1024

TOK_TILE = 2 * LANES
PIECE = 48
ROW_ALIGN = 16
WINDOW_ROWS = ROW_ALIGN + PIECE
WINDOW_SIZES = tuple(range(2 * ROW_ALIGN, WINDOW_ROWS + 1, ROW_ALIGN))
GATE_LANES = LANES
DFT_NB = LANES
OUTPROJ_CHUNKS = 2
PROJ_CHUNKS = 2
EXPERT_ROW_SPLIT = 2
NOT_SELECTED = -(1 << 20)


def _cparams(*sem):
    return pltpu.CompilerParams(dimension_semantics=sem, vmem_limit_bytes=VMEM_LIMIT_BYTES)


def _rms_mod(x, nw, shift, scale):
    ms = jnp.mean(x * x, axis=-1, keepdims=True)
    return x * lax.rsqrt(ms + RMS_EPS) * (nw * (1.0 + scale)) + shift


def _mod_kernel(ct_ref, w_ref, b_ref, o_ref):
    ct = ct_ref[...]
    s = ct * jax.nn.sigmoid(ct)
    w = w_ref[...]
    r0 = jnp.sum(s[:, 0:1] * w, axis=0, keepdims=True)
    r1 = jnp.sum(s[:, 1:2] * w, axis=0, keepdims=True)
    o_ref[...] = jnp.concatenate([r0, r1], axis=0) + b_ref[...]


def _modulation(ct, w_mod, b_mod):
    d, n = w_mod.shape
    tn = math.gcd(n, 2048)
    return pl.pallas_call(
        _mod_kernel,
        out_shape=jax.ShapeDtypeStruct((2, n), F32),
        grid=(n // tn,),
        in_specs=[pl.BlockSpec((d, 2), lambda i: (0, 0)),
                  pl.BlockSpec((d, tn), lambda i: (0, i)),
                  pl.BlockSpec((1, tn), lambda i: (0, i))],
        out_specs=pl.BlockSpec((2, tn), lambda i: (0, i)),
        compiler_params=_cparams("arbitrary"),
        name="modulation",
    )(ct, w_mod, b_mod)


def _ab_kernel(cf_ref, sf_ref, wf_ref, ab_ref):
    wf = wf_ref[0]
    a = jnp.dot(cf_ref[...], wf, preferred_element_type=F32, precision=lax.Precision.HIGHEST)
    b = jnp.dot(sf_ref[...], wf, preferred_element_type=F32, precision=lax.Precision.HIGHEST)
    ab_ref[0] = jnp.concatenate([a, b], axis=1).astype(BF16)


def _fold_channel_dft(w_fourier):
    g, fg, _ = w_fourier.shape
    idx = jnp.arange(fg, dtype=I32)
    th = ((idx[:, None] * idx[None, :]) % fg).astype(F32) * (2.0 * math.pi / fg)
    scale = 1.0 / math.sqrt(fg)
    cf = jnp.cos(th) * scale
    sf = jnp.sin(th) * scale
    return pl.pallas_call(
        _ab_kernel,
        out_shape=jax.ShapeDtypeStruct((g, fg, 2 * fg), BF16),
        grid=(g,),
        in_specs=[pl.BlockSpec((fg, fg), lambda i: (0, 0)),
                  pl.BlockSpec((fg, fg), lambda i: (0, 0)),
                  pl.BlockSpec((1, fg, fg), lambda i: (i, 0, 0))],
        out_specs=pl.BlockSpec((1, fg, 2 * fg), lambda i: (i, 0, 0)),
        compiler_params=_cparams("arbitrary"),
        name="fold_channel_dft",
    )(cf, sf, w_fourier)


def _rope_tables(n):
    quarter = HEAD_DIM // 4
    freqs = ROPE_THETA ** (-jnp.arange(quarter, dtype=F32) / quarter)
    zeros = lambda m: jnp.zeros((m, 2 * quarter), F32)

    def tables(pos, low_half):
        ang = pos[:, None] * freqs[None, :]
        ang = jnp.concatenate([ang, ang], axis=-1)
        cos, sin = jnp.cos(ang), jnp.sin(ang)
        first = (jnp.arange(2 * quarter) < quarter)[None, :]
        parts = (cos, jnp.where(first, -sin, 0.0), jnp.where(first, 0.0, sin))
        pad = zeros(pos.shape[0])
        return jnp.stack([jnp.concatenate([p, pad] if low_half else [pad, p], axis=-1)
                          for p in parts])

    row_tab = tables(jnp.arange(n // GRID_W, dtype=F32), True)
    col_tab = tables(jnp.arange(GRID_W, dtype=F32), False)
    return row_tab, col_tab


def _store_dft_rows(z_ref, plane, val, a0):
    groups = z_ref.shape[1]
    nb = groups * SUBLANES_F32
    for al in range(val.shape[0] // nb):
        z_ref[plane, :, (a0 + al) * SUBLANES_F32:(a0 + al + 1) * SUBLANES_F32, :] = (
            val[al * nb:(al + 1) * nb].reshape(groups, SUBLANES_F32, LANES))


def _proj_kernel(x_ref, nw_ref, sh_ref, sc_ref, w_ref, ab_ref, rt_ref, ct_ref,
                 q_ref, k_ref, v_ref, z_ref):
    quarter = HEAD_DIM // 4
    qscale = LOG2_E / math.sqrt(HEAD_DIM)
    tm = x_ref.shape[0]
    chunk = tm // PROJ_CHUNKS
    for r0 in range(0, tm, chunk):
        rs = slice(r0, r0 + chunk)
        hb = _rms_mod(x_ref[rs, :], nw_ref[...], sh_ref[...], sc_ref[...]).astype(BF16)
        cos, s_up, s_dn = (
            jnp.concatenate(
                [rt_ref[t, r:r + 1, :] + ct_ref[t]
                 for r in range(r0 // GRID_W, (r0 + chunk) // GRID_W)], axis=0)
            for t in range(3))

        def rope(t):
            return (t * cos + pltpu.roll(t, HEAD_DIM - quarter, 1) * s_up
                    + pltpu.roll(t, quarter, 1) * s_dn)

        q = jnp.dot(hb, w_ref[:, :Q_DIM], preferred_element_type=F32)
        for j in range(N_HEADS):
            sl = slice(j * HEAD_DIM, (j + 1) * HEAD_DIM)
            q_ref[rs, sl] = (rope(q[:, sl]) * qscale).astype(BF16)
        k = jnp.dot(hb, w_ref[:, Q_DIM:Q_DIM + KV_DIM], preferred_element_type=F32)
        for j in range(N_KV_HEADS):
            sl = slice(j * HEAD_DIM, (j + 1) * HEAD_DIM)
            k_ref[rs, sl] = rope(k[:, sl]).astype(BF16)
        v = jnp.dot(hb, w_ref[:, Q_DIM + KV_DIM:Q_DIM + 2 * KV_DIM], preferred_element_type=F32)
        v_ref[rs, :] = v.astype(BF16)
        u = jnp.dot(hb, w_ref[:, Q_DIM + 2 * KV_DIM:], preferred_element_type=F32).astype(BF16)
        for g in range(N_FGROUPS):
            pq = jnp.dot(u[:, g * FG:(g + 1) * FG], ab_ref[g], preferred_element_type=F32)
            per_group = FG // LANES
            for t in range(2 * per_group):
                plane = (t // per_group) * (F_DIM // LANES) + g * per_group + t % per_group
                _store_dft_rows(z_ref, plane, pq[:, t * LANES:(t + 1) * LANES], r0 // DFT_NB)


def _project_latent(x, nw, shift, scale, w_in_b, ab, tm=512):
    n, d = x.shape
    row_tab, col_tab = _rope_tables(n)
    row = lambda i: (i, 0)
    const2 = lambda i: (0, 0)
    return pl.pallas_call(
        _proj_kernel,
        out_shape=(jax.ShapeDtypeStruct((n, Q_DIM), BF16),
                   jax.ShapeDtypeStruct((n, KV_DIM), BF16),
                   jax.ShapeDtypeStruct((n, KV_DIM), BF16),
                   jax.ShapeDtypeStruct((2 * F_DIM // LANES, DFT_NB // SUBLANES_F32,
                                         (n // DFT_NB) * SUBLANES_F32, LANES), F32)),
        grid=(n // tm,),
        in_specs=[pl.BlockSpec((tm, d), row),
                  pl.BlockSpec((1, d), const2), pl.BlockSpec((1, d), const2),
                  pl.BlockSpec((1, d), const2),
                  pl.BlockSpec((d, D_IN), const2, pipeline_mode=pl.Buffered(1)),
                  pl.BlockSpec((N_FGROUPS, FG, 2 * FG), lambda i: (0, 0, 0),
                               pipeline_mode=pl.Buffered(1)),
                  pl.BlockSpec((3, tm // GRID_W, HEAD_DIM), lambda i: (0, i, 0)),
                  pl.BlockSpec((3, GRID_W, HEAD_DIM), lambda i: (0, 0, 0))],
        out_specs=(pl.BlockSpec((tm, Q_DIM), row), pl.BlockSpec((tm, KV_DIM), row),
                   pl.BlockSpec((tm, KV_DIM), row),
                   pl.BlockSpec((2 * F_DIM // LANES, DFT_NB // SUBLANES_F32,
                                 (tm // DFT_NB) * SUBLANES_F32, LANES), lambda i: (0, 0, i, 0))),
        compiler_params=_cparams("arbitrary"),
        name="project_latent",
    )(x, nw, shift, scale, w_in_b, ab, row_tab, col_tab)


def _ctx_kernel(x_ref, nw_ref, sh_ref, sc_ref, w_ref, k_ref, v_ref):
    hb = _rms_mod(x_ref[...], nw_ref[...], sh_ref[...], sc_ref[...]).astype(BF16)
    kv = jnp.dot(hb, w_ref[...], preferred_element_type=F32)
    k_ref[...] = kv[:, :KV_DIM].astype(BF16)
    v_ref[...] = kv[:, KV_DIM:].astype(BF16)


def _project_context(ctx, nw, shift, scale, w_kv_b):
    m, d = ctx.shape
    full = lambda shp: pl.BlockSpec(shp, lambda i: (0, 0))
    return pl.pallas_call(
        _ctx_kernel,
        out_shape=(jax.ShapeDtypeStruct((m, KV_DIM), BF16),
                   jax.ShapeDtypeStruct((m, KV_DIM), BF16)),
        grid=(1,),
        in_specs=[full((m, d)), full((1, d)), full((1, d)), full((1, d)), full((d, 2 * KV_DIM))],
        out_specs=(full((m, KV_DIM)), full((m, KV_DIM))),
        compiler_params=_cparams("arbitrary"),
        name="project_context",
    )(ctx, nw, shift, scale, w_kv_b)


def _attn_kernel(sink_ref, q_ref, kp_ref, km_ref, kn_ref, vp_ref, vm_ref, vn_ref,
                 kc_ref, vc_ref, lo_ref, hi_ref, o_ref, *, n_total, tq):
    i = pl.program_id(0)
    nsub = tq // WINDOW
    last_blk = n_total // WINDOW - 1
    kwin = jnp.concatenate([kp_ref[...], km_ref[...], kn_ref[...]], axis=0)
    vwin = jnp.concatenate([vp_ref[...], vm_ref[...], vn_ref[...]], axis=0)
    rows = GROUP * WINDOW
    span = 3 * WINDOW
    head_of_row = lax.broadcasted_iota(I32, (rows, 1), 0) // WINDOW
    ones_loc = jnp.ones((span, HEAD_DIM), BF16)
    ones_ctx = jnp.ones((kc_ref.shape[0], HEAD_DIM), BF16)
    nt = (((1,), (1,)), ((), ()))
    for b in range(nsub):
        blk = i * nsub + b
        bias_lo = lo_ref[(blk == 0).astype(I32)]
        bias_hi = hi_ref[(blk == last_blk).astype(I32)]
        for h in range(N_KV_HEADS):
            hs = slice(h * HEAD_DIM, (h + 1) * HEAD_DIM)
            qs = jnp.concatenate(
                [q_ref[b * WINDOW:(b + 1) * WINDOW,
                       (h * GROUP + g) * HEAD_DIM:(h * GROUP + g + 1) * HEAD_DIM]
                 for g in range(GROUP)], axis=0)
            kw = kwin[b * WINDOW:b * WINDOW + span, hs]
            vw = vwin[b * WINDOW:b * WINDOW + span, hs]
            s_loc = lax.dot_general(qs, kw, nt, preferred_element_type=F32)
            s_ctx = lax.dot_general(qs, kc_ref[:, hs], nt, preferred_element_type=F32)
            parts = [s_loc[:, :WINDOW] + bias_lo, s_loc[:, WINDOW:2 * WINDOW],
                     s_loc[:, 2 * WINDOW:] + bias_hi, s_ctx]
            sink_col = jnp.zeros((rows, 1), F32)
            for g in range(GROUP):
                sink_col = jnp.where(head_of_row == g, sink_ref[h * GROUP + g] * LOG2_E, sink_col)
            blocks = parts[:3] + [s_ctx[:, t * WINDOW:(t + 1) * WINDOW]
                                  for t in range(s_ctx.shape[1] // WINDOW)]
            widest = blocks[0]
            for blk_scores in blocks[1:]:
                widest = jnp.maximum(widest, blk_scores)
            m = jnp.maximum(sink_col, jnp.max(widest, axis=1, keepdims=True))
            p = [jnp.exp2(part - m).astype(BF16) for part in parts]
            ov = (jnp.dot(jnp.concatenate(p[:3], axis=1), jnp.concatenate([vw, ones_loc], axis=1),
                          preferred_element_type=F32)
                  + jnp.dot(p[3], jnp.concatenate([vc_ref[:, hs], ones_ctx], axis=1),
                            preferred_element_type=F32))
            o = ov[:, :HEAD_DIM] / (ov[:, HEAD_DIM:] + jnp.exp2(sink_col - m))
            for g in range(GROUP):
                o_ref[b * WINDOW:(b + 1) * WINDOW,
                      (h * GROUP + g) * HEAD_DIM:(h * GROUP + g + 1) * HEAD_DIM] = (
                    o[g * WINDOW:(g + 1) * WINDOW].astype(BF16))


def _band_biases():
    r = jnp.arange(GROUP * WINDOW, dtype=I32)[:, None] % WINDOW
    c = jnp.arange(WINDOW, dtype=I32)[None, :]
    masked = jnp.full((GROUP * WINDOW, WINDOW), NEG_INF, F32)
    lo = jnp.stack([jnp.where(c >= r, 0.0, NEG_INF).astype(F32), masked])
    hi = jnp.stack([jnp.where(c <= r, 0.0, NEG_INF).astype(F32), masked])
    return lo, hi


def _attention(q, k, v, kc, vc, sink, tq=1024):
    n = q.shape[0]
    bias_lo, bias_hi = _band_biases()
    bias_spec = pl.BlockSpec(bias_lo.shape, lambda i, s: (0, 0, 0))
    m = kc.shape[0]
    nsub = tq // WINDOW
    nblk = n // WINDOW
    prev = lambda i, s: (jnp.maximum(i * nsub - 1, 0), 0)
    main = lambda i, s: (i, 0)
    nxt = lambda i, s: (jnp.minimum(i * nsub + nsub, nblk - 1), 0)
    const = lambda i, s: (0, 0)
    kv_specs = [pl.BlockSpec((WINDOW, KV_DIM), prev), pl.BlockSpec((tq, KV_DIM), main),
                pl.BlockSpec((WINDOW, KV_DIM), nxt)]
    return pl.pallas_call(
        functools.partial(_attn_kernel, n_total=n, tq=tq),
        out_shape=jax.ShapeDtypeStruct((n, Q_DIM), BF16),
        grid_spec=pltpu.PrefetchScalarGridSpec(
            num_scalar_prefetch=1,
            grid=(n // tq,),
            in_specs=[pl.BlockSpec((tq, Q_DIM), main)] + kv_specs + kv_specs
                     + [pl.BlockSpec((m, KV_DIM), const), pl.BlockSpec((m, KV_DIM), const),
                        bias_spec, bias_spec],
            out_specs=pl.BlockSpec((tq, Q_DIM), main)),
        compiler_params=_cparams("arbitrary"),
        name="banded_attention",
    )(sink, q, k, k, k, v, v, v, kc, vc, bias_lo, bias_hi)


def _dft_tables(na, nb):
    n = na * nb
    s1 = 2.0 ** (-(int(math.log2(na)) // 2))
    s2 = (1.0 / math.sqrt(n)) / s1
    ka = jnp.arange(na, dtype=I32)
    th_tw = ((jnp.arange(nb, dtype=I32)[:, None] * ka[None, :]) % n).astype(F32) * (2.0 * math.pi / n)
    th_f = ((ka[:, None] * ka[None, :]) % na).astype(F32) * (2.0 * math.pi / na)
    twr, twi = (jnp.cos(th_tw) * s1)[:, :, None], (-jnp.sin(th_tw) * s1)[:, :, None]
    fr, fi = jnp.cos(th_f)[None, :, :], -jnp.sin(th_f)[None, :, :]
    tr, ti = twr * fr - twi * fi, twr * fi + twi * fr
    t1 = jnp.concatenate([jnp.concatenate([tr, ti], axis=-1),
                          jnp.concatenate([ti, -tr], axis=-1)], axis=-2).astype(BF16)
    kb = jnp.arange(nb, dtype=I32)
    th2 = ((kb[:, None] * kb[None, :]) % nb).astype(F32) * (2.0 * math.pi / nb)
    t2 = (jnp.concatenate([jnp.cos(th2), jnp.sin(th2)], axis=-1) * s2).astype(BF16)
    return t1, t2


def _dft1_kernel(z_ref, t_ref, y_ref):
    planes, _, rows, _ = z_ref.shape
    step = SUBLANES_F32
    half, na = planes // 2, rows // step
    for j in range(step):
        sel = pl.ds(j, na, stride=step)
        p = jnp.concatenate([z_ref[cc, 0, sel, :] for cc in range(half)], axis=1)
        q = jnp.concatenate([z_ref[cc, 0, sel, :] for cc in range(half, planes)], axis=1)
        xs = jnp.concatenate([p, q], axis=0).astype(BF16)
        y = jnp.dot(t_ref[j], xs, preferred_element_type=F32)
        for cc in range(half):
            y_ref[cc, 0, sel, :] = y[:na, cc * LANES:(cc + 1) * LANES]
            y_ref[half + cc, 0, sel, :] = y[na:, cc * LANES:(cc + 1) * LANES]


def _dft2_kernel(y_ref, t_ref, o_ref):
    planes, groups, rows, _ = y_ref.shape
    step = SUBLANES_F32
    half, nb = planes // 2, groups * step

    def tokens_b(cc, j):
        return y_ref[cc, :, j * step:(j + 1) * step, :].reshape(nb, LANES)

    for j in range(rows // step):
        yr = jnp.concatenate([tokens_b(cc, j) for cc in range(half)], axis=1)
        yi = jnp.concatenate([tokens_b(cc, j) for cc in range(half, planes)], axis=1)
        xs = jnp.concatenate([yr, yi], axis=0).astype(BF16)
        out = jnp.dot(t_ref[...], xs, preferred_element_type=F32)
        for cc in range(half):
            o_ref[cc, 0, pl.ds(j, nb, stride=step), :] = out[:, cc * LANES:(cc + 1) * LANES]


def _position_dft(z):
    planes, groups, rows, _ = z.shape
    step = SUBLANES_F32
    nb, na = groups * step, rows // step
    t1, t2 = _dft_tables(na, nb)
    y = pl.pallas_call(
        _dft1_kernel,
        out_shape=jax.ShapeDtypeStruct(z.shape, F32),
        grid=(groups,),
        in_specs=[pl.BlockSpec((planes, 1, rows, LANES), lambda i: (0, i, 0, 0)),
                  pl.BlockSpec((step, 2 * na, 2 * na), lambda i: (i, 0, 0))],
        out_specs=pl.BlockSpec((planes, 1, rows, LANES), lambda i: (0, i, 0, 0)),
        compiler_params=_cparams("arbitrary"),
        name="position_dft_stage1",
    )(z, t1)
    return pl.pallas_call(
        _dft2_kernel,
        out_shape=jax.ShapeDtypeStruct((planes // 2, na // step, nb * step, LANES), F32),
        grid=(na // step,),
        in_specs=[pl.BlockSpec((planes, groups, step * step, LANES), lambda i: (0, 0, i, 0)),
                  pl.BlockSpec((nb, 2 * nb), lambda i: (0, 0))],
        out_specs=pl.BlockSpec((planes // 2, 1, nb * step, LANES), lambda i: (0, i, 0, 0)),
        compiler_params=_cparams("arbitrary"),
        name="position_dft_stage2",
    )(y, t2)


def _outproj_kernel(ax_ref, fx_ref, x_ref, wo_ref, gate_ref, nw_ref, sh_ref, sc_ref, wr_ref,
                    x1_ref, g_ref, aff_ref):
    planes, groups, rows, _ = fx_ref.shape
    na = groups * SUBLANES_F32
    n_kb = rows // SUBLANES_F32
    kb_per_chunk = max(1, n_kb // OUTPROJ_CHUNKS)
    for first_kb in range(0, n_kb, kb_per_chunk):
        rs = slice(first_kb * na, (first_kb + kb_per_chunk) * na)
        fx = jnp.concatenate(
            [jnp.concatenate(
                [fx_ref[cc, :, kb * SUBLANES_F32:(kb + 1) * SUBLANES_F32, :].reshape(na, LANES)
                 for kb in range(first_kb, first_kb + kb_per_chunk)], axis=0)
             for cc in range(planes)], axis=1).astype(BF16)
        acc = (jnp.dot(ax_ref[rs, :], wo_ref[:Q_DIM, :], preferred_element_type=F32)
               + jnp.dot(fx, wo_ref[Q_DIM:, :], preferred_element_type=F32))
        x1 = x_ref[rs, :] + gate_ref[...] * acc
        x1_ref[rs, :] = x1
        gb = _rms_mod(x1, nw_ref[...], sh_ref[...], sc_ref[...]).astype(BF16)
        g_ref[rs, :] = gb
        logits = jnp.dot(gb, wr_ref[...], preferred_element_type=F32)
        e = jnp.exp(logits - jnp.max(logits, axis=1, keepdims=True))
        aff_ref[rs, :] = e / jnp.sum(e, axis=1, keepdims=True)


def _output_projection(ax, fx, x, wo_b, gate, nw, shift, scale, wr_b, tm=512):
    n, d = x.shape
    ne = wr_b.shape[1]
    row = lambda i: (i, 0)
    const = lambda i: (0, 0)
    vec = pl.BlockSpec((1, d), const)
    return pl.pallas_call(
        _outproj_kernel,
        out_shape=(jax.ShapeDtypeStruct((n, d), F32), jax.ShapeDtypeStruct((n, d), BF16),
                   jax.ShapeDtypeStruct((n, ne), F32)),
        grid=(n // tm,),
        in_specs=[pl.BlockSpec((tm, Q_DIM), row),
                  pl.BlockSpec((fx.shape[0], fx.shape[1], tm // (fx.shape[1] * SUBLANES_F32)
                                * SUBLANES_F32, LANES), lambda i: (0, 0, i, 0)),
                  pl.BlockSpec((tm, d), row),
                  pl.BlockSpec((Q_DIM + F_DIM, d), const, pipeline_mode=pl.Buffered(1)),
                  vec, vec, vec, vec, pl.BlockSpec((d, ne), const)],
        out_specs=(pl.BlockSpec((tm, d), row), pl.BlockSpec((tm, d), row),
                   pl.BlockSpec((tm, ne), row)),
        compiler_params=_cparams("arbitrary"),
        name="output_projection",
    )(ax, fx, x, wo_b, gate, nw, shift, scale, wr_b)


def _route_kernel(aff_ref, posm_ref, cnt_ref, off_ref, *, cap, n_exp, n_chunk):
    aff_all = aff_ref[...]
    blocks = [aff_all[e * n_chunk:(e + 1) * n_chunk] for e in range(n_exp)]

    def total(mask):
        s = jnp.sum(jnp.where(mask, 1.0, 0.0), axis=0, keepdims=True)
        return jnp.sum(s, axis=1, keepdims=True)

    def search(it, thr_bits):
        bit = jnp.left_shift(jnp.int32(1), 30 - it)
        out = []
        for e in range(n_exp):
            cand = thr_bits[e] | bit
            enough = total(blocks[e] >= lax.bitcast_convert_type(cand, F32)) >= cap
            out.append(jnp.where(enough, cand, thr_bits[e]))
        return tuple(out)

    thr_bits = lax.fori_loop(0, 31, search, tuple(jnp.zeros((1, 1), I32) for _ in range(n_exp)))
    thr = [lax.bitcast_convert_type(t, F32) for t in thr_bits]

    rl = lax.broadcasted_iota(I32, (LANES, LANES), 0)
    cl = lax.broadcasted_iota(I32, (LANES, LANES), 1)
    before = jnp.where(rl < cl, 1.0, 0.0).astype(BF16)
    ones = jnp.ones((LANES, LANES), BF16)
    rc = lax.broadcasted_iota(I32, (n_chunk, n_chunk), 0)
    cc = lax.broadcasted_iota(I32, (n_chunk, n_chunk), 1)
    earlier = jnp.where(cc < rc, 1.0, 0.0).astype(BF16)

    def excl_cumsum(mask):
        mb = jnp.where(mask, 1.0, 0.0).astype(BF16)
        within = jnp.dot(mb, before, preferred_element_type=F32)
        rowtot = jnp.dot(mb, ones, preferred_element_type=F32)
        choff = jnp.dot(earlier, rowtot.astype(BF16), preferred_element_type=F32)
        return within + choff, rowtot, choff

    for e in range(n_exp):
        gt = blocks[e] > thr[e]
        eq = blocks[e] == thr[e]
        need = cap - total(gt)
        tie_rank, _, _ = excl_cumsum(eq)
        sel = gt | (eq & (tie_rank < need))
        pos, rowtot, choff = excl_cumsum(sel)
        rs = slice(e * n_chunk, (e + 1) * n_chunk)
        posm_ref[rs, :] = jnp.where(sel, pos.astype(I32), NOT_SELECTED)
        cnt_ref[rs, :] = rowtot.astype(I32)
        off_ref[rs, :] = choff.astype(I32)


def _route(aff2, cap, n_exp):
    rows = aff2.shape[0]
    n_chunk = rows // n_exp
    spec = pl.BlockSpec((rows, LANES), lambda i: (0, 0))
    shp = jax.ShapeDtypeStruct((rows, LANES), I32)
    return pl.pallas_call(
        functools.partial(_route_kernel, cap=cap, n_exp=n_exp, n_chunk=n_chunk),
        out_shape=(shp, shp, shp),
        grid=(1,),
        in_specs=[spec],
        out_specs=(spec, spec, spec),
        compiler_params=_cparams("arbitrary"),
        name="expert_choice_routing",
    )(aff2)


def _window_matches(posm, firsts, win_starts):
    r = lax.broadcasted_iota(I32, (WINDOW_ROWS, TOK_TILE), 0)
    out = []
    for e, (first, start) in enumerate(zip(firsts, win_starts)):
        pm = posm[e:e + 1, :]
        pm = jnp.where((pm >= first) & (pm < first + PIECE), pm, NOT_SELECTED)
        out.append((pm - start) == r)
    return out


def _align_down(v):
    return (v // ROW_ALIGN) * ROW_ALIGN


def _for_window_size(rows_needed, fn):
    below = None
    for k, size in enumerate(WINDOW_SIZES):
        fits = rows_needed <= size if k + 1 < len(WINDOW_SIZES) else None
        cond = fits if below is None else (below if fits is None else jnp.logical_and(below, fits))
        pl.when(cond)(functools.partial(fn, size))
        below = rows_needed > size


def _gather_kernel(off_ref, cnt_ref, g_ref, posm_ref, aff_ref, xs_hbm,
                   stg, tail, sem, npass_ref, sent_ref, *, n_exp, n_chunk, cap):
    c = pl.program_id(0)
    stride = cap + WINDOW_ROWS

    @pl.when(c == 0)
    def _():
        npass_ref[0] = 0
        tail[...] = jnp.zeros_like(tail)

    def window_copy(slot, e, dst, size=WINDOW_ROWS):
        return pltpu.make_async_copy(stg.at[slot, pl.ds(e * WINDOW_ROWS, size)],
                                     xs_hbm.at[pl.ds(dst, size)], sem.at[slot, e])

    def start_window(slot, e, dst, rows_used):
        _for_window_size(rows_used, lambda size: window_copy(slot, e, dst, size).start())
        sent_ref[slot * n_exp + e] = rows_used

    def wait_window(slot, e):
        _for_window_size(sent_ref[slot * n_exp + e],
                         lambda size: window_copy(slot, e, 0, size).wait())

    @pl.when(c == 0)
    def _():
        stg[0] = jnp.zeros(stg.shape[1:], stg.dtype)
        for e in range(n_exp):
            window_copy(0, e, e * stride + cap).start()
        for e in range(n_exp):
            window_copy(0, e, 0).wait()

    most = cnt_ref[c]
    for e in range(1, n_exp):
        most = jnp.maximum(most, cnt_ref[e * n_chunk + c])
    passes = (most + PIECE - 1) // PIECE

    def one_pass(p, carry):
        done = npass_ref[0]
        slot = done % 2
        posm = posm_ref[0]
        aff = aff_ref[0]
        firsts, win_starts, next_shift, rows_used = [], [], [], []
        for e in range(n_exp):
            o, n_e = off_ref[e * n_chunk + c], cnt_ref[e * n_chunk + c]
            first = o + jnp.minimum(p * PIECE, n_e)
            after = o + jnp.minimum((p + 1) * PIECE, n_e)
            firsts.append(first)
            win_starts.append(_align_down(first))
            next_shift.append(_align_down(after) - _align_down(first))
            rows_used.append(after - _align_down(first))
        matches = _window_matches(posm, firsts, win_starts)
        onehot = jnp.concatenate([jnp.where(mt, 1.0, 0.0).astype(BF16) for mt in matches], axis=0)
        new = jnp.dot(onehot, g_ref[...], preferred_element_type=F32)
        lane = lax.broadcasted_iota(I32, (WINDOW_ROWS, GATE_LANES), 1)
        for e in range(n_exp):
            lo = e * WINDOW_ROWS
            gate = jnp.sum(jnp.where(matches[e], aff[e:e + 1, :], 0.0), axis=1, keepdims=True)
            hi = gate.astype(BF16).astype(F32)
            mid = (gate - hi).astype(BF16).astype(F32)
            low = gate - hi - mid
            parts = jnp.where(lane == 0, hi, jnp.where(lane == 1, mid,
                                                       jnp.where(lane == 2, low, 0.0)))
            rows = jnp.concatenate([new[lo:lo + WINDOW_ROWS], parts], axis=1)
            stg[slot, lo:lo + ROW_ALIGN] = (rows[:ROW_ALIGN] + tail[e].astype(F32)).astype(BF16)
            stg[slot, lo + ROW_ALIGN:lo + WINDOW_ROWS] = rows[ROW_ALIGN:].astype(BF16)

        dsts = [pl.multiple_of(e * stride + win_starts[e], ROW_ALIGN) for e in range(n_exp)]

        @pl.when(done > 0)
        def _():
            for e in range(n_exp):
                wait_window(1 - slot, e)
                start_window(slot, e, dsts[e], rows_used[e])

        @pl.when(done == 0)
        def _():
            for e in range(n_exp):
                start_window(slot, e, dsts[e], rows_used[e])

        for e in range(n_exp):
            src = pl.ds(pl.multiple_of(e * WINDOW_ROWS + next_shift[e], ROW_ALIGN), ROW_ALIGN)
            tail[e] = stg[slot, src, :]
        npass_ref[0] = done + 1
        return carry

    lax.fori_loop(0, passes, one_pass, 0)

    @pl.when(c == pl.num_programs(0) - 1)
    def _():
        done = npass_ref[0]

        @pl.when(done > 0)
        def _():
            for e in range(n_exp):
                wait_window((done - 1) % 2, e)


def _gather(off, cnt, g, posm_t, aff_t, cap):
    n, d = g.shape
    n_chunk, n_exp, _ = posm_t.shape
    rows = n_exp * (cap + WINDOW_ROWS)
    tile3 = lambda i, o, c: (i, 0, 0)
    return pl.pallas_call(
        functools.partial(_gather_kernel, n_exp=n_exp, n_chunk=n_chunk, cap=cap),
        out_shape=jax.ShapeDtypeStruct((rows, d + GATE_LANES), BF16),
        grid_spec=pltpu.PrefetchScalarGridSpec(
            num_scalar_prefetch=2,
            grid=(n_chunk,),
            in_specs=[pl.BlockSpec((TOK_TILE, d), lambda i, o, c: (i, 0)),
                      pl.BlockSpec((1, n_exp, TOK_TILE), tile3),
                      pl.BlockSpec((1, n_exp, TOK_TILE), tile3)],
            out_specs=pl.BlockSpec(memory_space=pl.ANY),
            scratch_shapes=[pltpu.VMEM((2, n_exp * WINDOW_ROWS, d + GATE_LANES), BF16),
                            pltpu.VMEM((n_exp, ROW_ALIGN, d + GATE_LANES), BF16),
                            pltpu.SemaphoreType.DMA((2, n_exp)),
                            pltpu.SMEM((1,), I32),
                            pltpu.SMEM((2 * n_exp,), I32)]),
        compiler_params=_cparams("arbitrary"),
        name="gather_expert_tokens",
    )(off, cnt, g, posm_t, aff_t)


def _expert_kernel(xs_ref, wg_ref, wu_ref, wd_ref, y_ref, h_ref, gate_ref, *, nf, tf, tn):
    j = pl.program_id(1)
    d = wg_ref.shape[1]
    cap = xs_ref.shape[1]
    halves = [slice(r * (cap // EXPERT_ROW_SPLIT), (r + 1) * (cap // EXPERT_ROW_SPLIT))
              for r in range(EXPERT_ROW_SPLIT)]

    @pl.when(j == 0)
    def _():
        parts = xs_ref[0, :, d:].astype(F32)
        gate = parts[:, 0:1] + parts[:, 1:2] + parts[:, 2:3]
        gate_ref[...] = jnp.broadcast_to(gate, gate_ref.shape)

    @pl.when(j < nf)
    def _():
        wg = wg_ref[0].astype(BF16)
        wu = wu_ref[0].astype(BF16)
        for rs in halves:
            xs = xs_ref[0, rs, :d]
            gp = jnp.dot(xs, wg, preferred_element_type=F32)
            up = jnp.dot(xs, wu, preferred_element_type=F32)
            h = ((gp * jax.nn.sigmoid(gp)) * up).astype(BF16)
            for jj in range(nf):
                @pl.when(j == jj)
                def _():
                    h_ref[rs, jj * tf:(jj + 1) * tf] = h

    @pl.when(j >= nf)
    def _():
        wd = wd_ref[0].astype(BF16)
        for rs in halves:
            y = jnp.dot(h_ref[rs, :], wd, preferred_element_type=F32)
            y_ref[0, rs, :] = (y * gate_ref[rs, 0:1]).astype(BF16)


def _experts(xs3, w_gate, w_up, w_down, cap, tf=512, tn=512):
    n_exp, d, ff = w_gate.shape
    tf, tn = min(tf, ff), min(tn, d)
    nf, nn = ff // tf, d // tn

    def ahead(e, j):
        return jnp.minimum(e + (j >= nf).astype(I32), n_exp - 1)

    def ff_tile(e, j):
        return jnp.where(j >= nf, jnp.where(e == n_exp - 1, nf - 1, 0), j)

    return pl.pallas_call(
        functools.partial(_expert_kernel, nf=nf, tf=tf, tn=tn),
        out_shape=jax.ShapeDtypeStruct((n_exp, cap, d), BF16),
        grid=(n_exp, nf + nn),
        in_specs=[pl.BlockSpec((1, cap, d + GATE_LANES), lambda e, j: (ahead(e, j), 0, 0)),
                  pl.BlockSpec((1, d, tf), lambda e, j: (ahead(e, j), 0, ff_tile(e, j))),
                  pl.BlockSpec((1, d, tf), lambda e, j: (ahead(e, j), 0, ff_tile(e, j))),
                  pl.BlockSpec((1, ff, tn), lambda e, j: (e, 0, jnp.maximum(j - nf, 0)))],
        out_specs=pl.BlockSpec((1, cap, tn), lambda e, j: (e, 0, jnp.maximum(j - nf, 0))),
        scratch_shapes=[pltpu.VMEM((cap, ff), BF16), pltpu.VMEM((cap, LANES), F32)],
        compiler_params=_cparams("arbitrary", "arbitrary"),
        name="swiglu_experts",
    )(xs3, w_gate, w_up, w_down)


def _combine_kernel(off_ref, cnt_ref, y_hbm, posm_ref, x1_ref, gate_ref, nw_ref, o_ref,
                    stg, acc_ref, sem, *, n_exp, n_chunk, cap):
    c = pl.program_id(0)
    last_start = n_exp * cap - WINDOW_ROWS

    def firsts_of(tile, p):
        return [off_ref[e * n_chunk + tile] + p * PIECE for e in range(n_exp)]

    def starts_of(tile, p):
        return [pl.multiple_of(jnp.minimum(e * cap + _align_down(first), last_start), ROW_ALIGN)
                for e, first in enumerate(firsts_of(tile, p))]

    def rows_used_of(tile, p):
        out = []
        for e, (first, start) in enumerate(zip(firsts_of(tile, p), starts_of(tile, p))):
            left = jnp.clip(cnt_ref[e * n_chunk + tile] - p * PIECE, 0, PIECE)
            out.append(e * cap + first + left - start)
        return out

    def piece_copy(slot, e, src, size=WINDOW_ROWS):
        return pltpu.make_async_copy(y_hbm.at[pl.ds(src, size)],
                                     stg.at[slot, pl.ds(e * WINDOW_ROWS, size)], sem.at[slot])

    def fetch(tile, p, slot):
        for e, (src, used) in enumerate(zip(starts_of(tile, p), rows_used_of(tile, p))):
            _for_window_size(used, lambda size: piece_copy(slot, e, src, size).start())

    def wait_fetch(tile, p, slot):
        for e, used in enumerate(rows_used_of(tile, p)):
            _for_window_size(used, lambda size: piece_copy(slot, e, 0, size).wait())

    def contribution(p, slot, tokens):
        starts = [s - e * cap for e, s in enumerate(starts_of(c, p))]
        matches = _window_matches(posm_ref[0], firsts_of(c, p), starts)
        onehot = jnp.concatenate([jnp.where(mt[:, tokens], 1.0, 0.0).astype(BF16)
                                  for mt in matches], axis=0)
        return lax.dot_general(onehot, stg[slot], (((0,), (0,)), ((), ())),
                               preferred_element_type=F32)

    def finish(tokens, moe):
        x2 = x1_ref[tokens, :] + gate_ref[...] * moe
        ms = jnp.mean(x2 * x2, axis=-1, keepdims=True)
        o_ref[tokens, :] = x2 * lax.rsqrt(ms + RMS_EPS) * nw_ref[...]

    slot = c % 2

    @pl.when(c == 0)
    def _():
        stg[...] = jnp.zeros_like(stg)
        fetch(0, 0, 0)

    @pl.when(c + 1 < pl.num_programs(0))
    def _():
        fetch(c + 1, 0, 1 - slot)

    most = cnt_ref[c]
    for e in range(1, n_exp):
        most = jnp.maximum(most, cnt_ref[e * n_chunk + c])
    passes = (most + PIECE - 1) // PIECE
    wait_fetch(c, 0, slot)

    @pl.when(passes <= 1)
    def _():
        for t0 in range(0, TOK_TILE, LANES):
            tokens = slice(t0, t0 + LANES)
            finish(tokens, contribution(0, slot, tokens))

    @pl.when(passes > 1)
    def _():
        everyone = slice(0, TOK_TILE)
        acc_ref[...] = contribution(0, slot, everyone)

        def extra_pass(p, carry):
            fetch(c, p, slot)
            wait_fetch(c, p, slot)
            acc_ref[...] += contribution(p, slot, everyone)
            return carry

        lax.fori_loop(1, passes, extra_pass, 0)
        finish(everyone, acc_ref[...])


def _combine(off, cnt, y, posm_t, x1, gate, nw, cap):
    n, d = x1.shape
    n_chunk, n_exp, _ = posm_t.shape
    return pl.pallas_call(
        functools.partial(_combine_kernel, n_exp=n_exp, n_chunk=n_chunk, cap=cap),
        out_shape=jax.ShapeDtypeStruct((n, d), F32),
        grid_spec=pltpu.PrefetchScalarGridSpec(
            num_scalar_prefetch=2,
            grid=(n_chunk,),
            in_specs=[pl.BlockSpec(memory_space=pl.ANY),
                      pl.BlockSpec((1, n_exp, TOK_TILE), lambda i, o, c: (i, 0, 0)),
                      pl.BlockSpec((TOK_TILE, d), lambda i, o, c: (i, 0)),
                      pl.BlockSpec((1, d), lambda i, o, c: (0, 0)),
                      pl.BlockSpec((1, d), lambda i, o, c: (0, 0))],
            out_specs=pl.BlockSpec((TOK_TILE, d), lambda i, o, c: (i, 0)),
            scratch_shapes=[pltpu.VMEM((2, n_exp * WINDOW_ROWS, d), BF16),
                            pltpu.VMEM((TOK_TILE, d), F32),
                            pltpu.SemaphoreType.DMA((2,))]),
        compiler_params=_cparams("arbitrary"),
        name="combine_experts_final_norm",
    )(off, cnt, y, posm_t, x1, gate, nw)


def kernel(x, c, ctx, c_ctx, w_mod, b_mod, norm_mix, w_in, sink, w_fourier, w_out, norm_ffn,
           w_router, w_gate, w_up, w_down, norm_final):
    assert x.shape[0] == 1 and w_mod.shape[0] == 1
    n, d = x.shape[1], x.shape[2]
    xl, cx = x[0], ctx[0]
    n_exp = w_router.shape[2]
    cap = max(1, CAP_FACTOR * n // n_exp)
    n_chunk = n // LANES
    n_tile = n // TOK_TILE
    per_tile = TOK_TILE // LANES

    mod = _modulation(jnp.stack([c[0], c_ctx], axis=1), w_mod[0], b_mod[0][None, :])
    mx = mod[0].reshape(6, 1, d)
    mc = mod[1].reshape(6, 1, d)

    w_in_b = w_in[0].astype(BF16)
    nmix = norm_mix[0][None, :]
    ab = _fold_channel_dft(w_fourier[0])
    q, k, v, z = _project_latent(xl, nmix, mx[0], mx[1], w_in_b, ab)
    kc, vc = _project_context(cx, nmix, mc[0], mc[1], w_in_b[:, Q_DIM:Q_DIM + 2 * KV_DIM])
    ax = _attention(q, k, v, kc, vc, sink[0])
    fx = _position_dft(z)

    x1, g, aff = _output_projection(ax, fx, xl, w_out[0].astype(BF16), mx[2], norm_ffn[0][None, :],
                                    mx[3], mx[4], w_router[0].astype(BF16))

    aff2 = aff.T.reshape(n_exp * n_chunk, LANES)
    posm, cnt, off = _route(aff2, cap, n_exp)
    by_tile = lambda a: a.reshape(n_exp, n_tile, TOK_TILE).transpose(1, 0, 2)
    posm_t, aff_t = by_tile(posm), by_tile(aff2)
    cnt1 = cnt[:, 0].reshape(n_exp * n_tile, per_tile).sum(axis=1)
    off1 = off[:, 0].reshape(n_exp * n_tile, per_tile)[:, 0]

    xs = _gather(off1, cnt1, g, posm_t, aff_t, cap)
    y = _experts(xs.reshape(n_exp, cap + WINDOW_ROWS, d + GATE_LANES),
                 w_gate[0], w_up[0], w_down[0], cap)
    out = _combine(off1, cnt1, y.reshape(n_exp * cap, d), posm_t, x1, mx[5], norm_final[None, :],
                   cap)
    return out[None]
```

```python
import functools
import math

import jax
import jax.numpy as jnp
from jax import lax
from jax.experimental import pallas as pl
from jax.experimental.pallas import tpu as pltpu

F32 = jnp.float32
BF16 = jnp.bfloat16
I32 = jnp.int32

HEAD_DIM = 128
N_HEADS = 8
N_KV_HEADS = 2
GROUP = N_HEADS // N_KV_HEADS
WINDOW = 128
Q_DIM = N_HEADS * HEAD_DIM
KV_DIM = N_KV_HEADS * HEAD_DIM
N_FGROUPS = 4
FG = 256
F_DIM = N_FGROUPS * FG
D_IN = Q_DIM + 2 * KV_DIM + F_DIM
N_EXPERTS = 16
CAP_FACTOR = 2
GRID_W = 64
ROPE_THETA = 10000.0
RMS_EPS = 1e-6
NEG_INF = -1e30
LOG2_E = math.log2(math.e)

LANES = 128
SUBLANES_F32 = 8
VMEM_LIMIT_BYTES = 56 * 1024 * 1024

TOK_TILE = 2 * LANES
PIECE = 48
ROW_ALIGN = 16
WINDOW_ROWS = ROW_ALIGN + PIECE
WINDOW_SIZES = tuple(range(2 * ROW_ALIGN, WINDOW_ROWS + 1, ROW_ALIGN))
GATE_LANES = LANES
DFT_NB = LANES
OUTPROJ_CHUNKS = 2
PROJ_CHUNKS = 2
EXPERT_ROW_SPLIT = 2
NOT_SELECTED = -(1 << 20)


def _cparams(*sem):
    return pltpu.CompilerParams(dimension_semantics=sem, vmem_limit_bytes=VMEM_LIMIT_BYTES)


def _rms_mod(x, nw, shift, scale):
    ms = jnp.mean(x * x, axis=-1, keepdims=True)
    return x * lax.rsqrt(ms + RMS_EPS) * (nw * (1.0 + scale)) + shift


def _mod_kernel(ct_ref, w_ref, b_ref, o_ref):
    ct = ct_ref[...]
    s = ct * jax.nn.sigmoid(ct)
    w = w_ref[...]
    r0 = jnp.sum(s[:, 0:1] * w, axis=0, keepdims=True)
    r1 = jnp.sum(s[:, 1:2] * w, axis=0, keepdims=True)
    o_ref[...] = jnp.concatenate([r0, r1], axis=0) + b_ref[...]


def _modulation(ct, w_mod, b_mod):
    d, n = w_mod.shape
    tn = math.gcd(n, 1024)
    return pl.pallas_call(
        _mod_kernel,
        out_shape=jax.ShapeDtypeStruct((2, n), F32),
        grid=(n // tn,),
        in_specs=[pl.BlockSpec((d, 2), lambda i: (0, 0)),
                  pl.BlockSpec((d, tn), lambda i: (0, i)),
                  pl.BlockSpec((1, tn), lambda i: (0, i))],
        out_specs=pl.BlockSpec((2, tn), lambda i: (0, i)),
        compiler_params=_cparams("arbitrary"),
        name="modulation",
    )(ct, w_mod, b_mod)


def _ab_kernel(cf_ref, sf_ref, wf_ref, ab_ref):
    wf = wf_ref[0]
    a = jnp.dot(cf_ref[...], wf, preferred_element_type=F32, precision=lax.Precision.HIGHEST)
    b = jnp.dot(sf_ref[...], wf, preferred_element_type=F32, precision=lax.Precision.HIGHEST)
    ab_ref[0] = jnp.concatenate([a, b], axis=1).astype(BF16)


def _fold_channel_dft(w_fourier):
    g, fg, _ = w_fourier.shape
    idx = jnp.arange(fg, dtype=I32)
    th = ((idx[:, None] * idx[None, :]) % fg).astype(F32) * (2.0 * math.pi / fg)
    scale = 1.0 / math.sqrt(fg)
    cf = jnp.cos(th) * scale
    sf = jnp.sin(th) * scale
    return pl.pallas_call(
        _ab_kernel,
        out_shape=jax.ShapeDtypeStruct((g, fg, 2 * fg), BF16),
        grid=(g,),
        in_specs=[pl.BlockSpec((fg, fg), lambda i: (0, 0)),
                  pl.BlockSpec((fg, fg), lambda i: (0, 0)),
                  pl.BlockSpec((1, fg, fg), lambda i: (i, 0, 0))],
        out_specs=pl.BlockSpec((1, fg, 2 * fg), lambda i: (i, 0, 0)),
        compiler_params=_cparams("arbitrary"),
        name="fold_channel_dft",
    )(cf, sf, w_fourier)


def _rope_tables(n):
    quarter = HEAD_DIM // 4
    freqs = ROPE_THETA ** (-jnp.arange(quarter, dtype=F32) / quarter)
    zeros = lambda m: jnp.zeros((m, 2 * quarter), F32)

    def tables(pos, low_half):
        ang = pos[:, None] * freqs[None, :]
        ang = jnp.concatenate([ang, ang], axis=-1)
        cos, sin = jnp.cos(ang), jnp.sin(ang)
        first = (jnp.arange(2 * quarter) < quarter)[None, :]
        parts = (cos, jnp.where(first, -sin, 0.0), jnp.where(first, 0.0, sin))
        pad = zeros(pos.shape[0])
        return jnp.stack([jnp.concatenate([p, pad] if low_half else [pad, p], axis=-1)
                          for p in parts])

    row_tab = tables(jnp.arange(n // GRID_W, dtype=F32), True)
    col_tab = tables(jnp.arange(GRID_W, dtype=F32), False)
    return row_tab, col_tab


def _store_dft_rows(z_ref, plane, val, a0):
    groups = z_ref.shape[1]
    nb = groups * SUBLANES_F32
    for al in range(val.shape[0] // nb):
        z_ref[plane, :, (a0 + al) * SUBLANES_F32:(a0 + al + 1) * SUBLANES_F32, :] = (
            val[al * nb:(al + 1) * nb].reshape(groups, SUBLANES_F32, LANES))


def _proj_kernel(x_ref, nw_ref, sh_ref, sc_ref, w_ref, ab_ref, rt_ref, ct_ref,
                 q_ref, k_ref, v_ref, z_ref):
    quarter = HEAD_DIM // 4
    qscale = LOG2_E / math.sqrt(HEAD_DIM)
    tm = x_ref.shape[0]
    chunk = tm // PROJ_CHUNKS
    for r0 in range(0, tm, chunk):
        rs = slice(r0, r0 + chunk)
        hb = _rms_mod(x_ref[rs, :], nw_ref[...], sh_ref[...], sc_ref[...]).astype(BF16)
        cos, s_up, s_dn = (
            jnp.concatenate(
                [rt_ref[t, r:r + 1, :] + ct_ref[t]
                 for r in range(r0 // GRID_W, (r0 + chunk) // GRID_W)], axis=0)
            for t in range(3))

        def rope(t):
            return (t * cos + pltpu.roll(t, HEAD_DIM - quarter, 1) * s_up
                    + pltpu.roll(t, quarter, 1) * s_dn)

        q = jnp.dot(hb, w_ref[:, :Q_DIM], preferred_element_type=F32)
        for j in range(N_HEADS):
            sl = slice(j * HEAD_DIM, (j + 1) * HEAD_DIM)
            q_ref[rs, sl] = (rope(q[:, sl]) * qscale).astype(BF16)
        k = jnp.dot(hb, w_ref[:, Q_DIM:Q_DIM + KV_DIM], preferred_element_type=F32)
        for j in range(N_KV_HEADS):
            sl = slice(j * HEAD_DIM, (j + 1) * HEAD_DIM)
            k_ref[rs, sl] = rope(k[:, sl]).astype(BF16)
        v = jnp.dot(hb, w_ref[:, Q_DIM + KV_DIM:Q_DIM + 2 * KV_DIM], preferred_element_type=F32)
        v_ref[rs, :] = v.astype(BF16)
        u = jnp.dot(hb, w_ref[:, Q_DIM + 2 * KV_DIM:], preferred_element_type=F32).astype(BF16)
        for g in range(N_FGROUPS):
            pq = jnp.dot(u[:, g * FG:(g + 1) * FG], ab_ref[g], preferred_element_type=F32)
            per_group = FG // LANES
            for t in range(2 * per_group):
                plane = (t // per_group) * (F_DIM // LANES) + g * per_group + t % per_group
                _store_dft_rows(z_ref, plane, pq[:, t * LANES:(t + 1) * LANES], r0 // DFT_NB)


def _project_latent(x, nw, shift, scale, w_in_b, ab, tm=512):
    n, d = x.shape
    row_tab, col_tab = _rope_tables(n)
    row = lambda i: (i, 0)
    const2 = lambda i: (0, 0)
    return pl.pallas_call(
        _proj_kernel,
        out_shape=(jax.ShapeDtypeStruct((n, Q_DIM), BF16),
                   jax.ShapeDtypeStruct((n, KV_DIM), BF16),
                   jax.ShapeDtypeStruct((n, KV_DIM), BF16),
                   jax.ShapeDtypeStruct((2 * F_DIM // LANES, DFT_NB // SUBLANES_F32,
                                         (n // DFT_NB) * SUBLANES_F32, LANES), F32)),
        grid=(n // tm,),
        in_specs=[pl.BlockSpec((tm, d), row),
                  pl.BlockSpec((1, d), const2), pl.BlockSpec((1, d), const2),
                  pl.BlockSpec((1, d), const2),
                  pl.BlockSpec((d, D_IN), const2, pipeline_mode=pl.Buffered(1)),
                  pl.BlockSpec((N_FGROUPS, FG, 2 * FG), lambda i: (0, 0, 0),
                               pipeline_mode=pl.Buffered(1)),
                  pl.BlockSpec((3, tm // GRID_W, HEAD_DIM), lambda i: (0, i, 0)),
                  pl.BlockSpec((3, GRID_W, HEAD_DIM), lambda i: (0, 0, 0))],
        out_specs=(pl.BlockSpec((tm, Q_DIM), row), pl.BlockSpec((tm, KV_DIM), row),
                   pl.BlockSpec((tm, KV_DIM), row),
                   pl.BlockSpec((2 * F_DIM // LANES, DFT_NB // SUBLANES_F32,
                                 (tm // DFT_NB) * SUBLANES_F32, LANES), lambda i: (0, 0, i, 0))),
        compiler_params=_cparams("arbitrary"),
        name="project_latent",
    )(x, nw, shift, scale, w_in_b, ab, row_tab, col_tab)


def _ctx_kernel(x_ref, nw_ref, sh_ref, sc_ref, w_ref, k_ref, v_ref):
    hb = _rms_mod(x_ref[...], nw_ref[...], sh_ref[...], sc_ref[...]).astype(BF16)
    kv = jnp.dot(hb, w_ref[...], preferred_element_type=F32)
    k_ref[...] = kv[:, :KV_DIM].astype(BF16)
    v_ref[...] = kv[:, KV_DIM:].astype(BF16)


def _project_context(ctx, nw, shift, scale, w_kv_b):
    m, d = ctx.shape
    full = lambda shp: pl.BlockSpec(shp, lambda i: (0, 0))
    return pl.pallas_call(
        _ctx_kernel,
        out_shape=(jax.ShapeDtypeStruct((m, KV_DIM), BF16),
                   jax.ShapeDtypeStruct((m, KV_DIM), BF16)),
        grid=(1,),
        in_specs=[full((m, d)), full((1, d)), full((1, d)), full((1, d)), full((d, 2 * KV_DIM))],
        out_specs=(full((m, KV_DIM)), full((m, KV_DIM))),
        compiler_params=_cparams("arbitrary"),
        name="project_context",
    )(ctx, nw, shift, scale, w_kv_b)


def _attn_body(sink_ref, q_ref, kp_ref, km_ref, kn_ref, vp_ref, vm_ref, vn_ref,
               kc_ref, vc_ref, lo_ref, hi_ref, o_ref, *, n_total, tq, first_step):
    i = pl.program_id(0) + first_step
    nsub = tq // WINDOW
    last_blk = n_total // WINDOW - 1
    kwin = jnp.concatenate([kp_ref[...], km_ref[...], kn_ref[...]], axis=0)
    vwin = jnp.concatenate([vp_ref[...], vm_ref[...], vn_ref[...]], axis=0)
    rows = GROUP * WINDOW
    span = 3 * WINDOW
    head_of_row = lax.broadcasted_iota(I32, (rows, 1), 0) // WINDOW
    ones_loc = jnp.ones((span, HEAD_DIM), BF16)
    ones_ctx = jnp.ones((kc_ref.shape[0], HEAD_DIM), BF16)
    nt = (((1,), (1,)), ((), ()))
    for b in range(nsub):
        blk = i * nsub + b
        bias_lo = lo_ref[(blk == 0).astype(I32)]
        bias_hi = hi_ref[(blk == last_blk).astype(I32)]
        for h in range(N_KV_HEADS):
            hs = slice(h * HEAD_DIM, (h + 1) * HEAD_DIM)
            qs = jnp.concatenate(
                [q_ref[b * WINDOW:(b + 1) * WINDOW,
                       (h * GROUP + g) * HEAD_DIM:(h * GROUP + g + 1) * HEAD_DIM]
                 for g in range(GROUP)], axis=0)
            kw = kwin[b * WINDOW:b * WINDOW + span, hs]
            vw = vwin[b * WINDOW:b * WINDOW + span, hs]
            s_loc = lax.dot_general(qs, kw, nt, preferred_element_type=F32)
            s_ctx = lax.dot_general(qs, kc_ref[:, hs], nt, preferred_element_type=F32)
            parts = [s_loc[:, :WINDOW] + bias_lo, s_loc[:, WINDOW:2 * WINDOW],
                     s_loc[:, 2 * WINDOW:] + bias_hi, s_ctx]
            sink_col = jnp.zeros((rows, 1), F32)
            for g in range(GROUP):
                sink_col = jnp.where(head_of_row == g, sink_ref[h * GROUP + g] * LOG2_E, sink_col)
            blocks = parts[:3] + [s_ctx[:, t * WINDOW:(t + 1) * WINDOW]
                                  for t in range(s_ctx.shape[1] // WINDOW)]
            widest = blocks[0]
            for blk_scores in blocks[1:]:
                widest = jnp.maximum(widest, blk_scores)
            m = jnp.maximum(sink_col, jnp.max(widest, axis=1, keepdims=True))
            p = [jnp.exp2(part - m).astype(BF16) for part in parts]
            ov = (jnp.dot(jnp.concatenate(p[:3], axis=1), jnp.concatenate([vw, ones_loc], axis=1),
                          preferred_element_type=F32)
                  + jnp.dot(p[3], jnp.concatenate([vc_ref[:, hs], ones_ctx], axis=1),
                            preferred_element_type=F32))
            o = ov[:, :HEAD_DIM] / (ov[:, HEAD_DIM:] + jnp.exp2(sink_col - m))
            for g in range(GROUP):
                o_ref[b * WINDOW:(b + 1) * WINDOW,
                      (h * GROUP + g) * HEAD_DIM:(h * GROUP + g + 1) * HEAD_DIM] = (
                    o[g * WINDOW:(g + 1) * WINDOW].astype(BF16))


def _band_biases():
    r = jnp.arange(GROUP * WINDOW, dtype=I32)[:, None] % WINDOW
    c = jnp.arange(WINDOW, dtype=I32)[None, :]
    masked = jnp.full((GROUP * WINDOW, WINDOW), NEG_INF, F32)
    lo = jnp.stack([jnp.where(c >= r, 0.0, NEG_INF).astype(F32), masked])
    hi = jnp.stack([jnp.where(c <= r, 0.0, NEG_INF).astype(F32), masked])
    return lo, hi


def _dft_tables(na, nb):
    n = na * nb
    s1 = 2.0 ** (-(int(math.log2(na)) // 2))
    s2 = (1.0 / math.sqrt(n)) / s1
    ka = jnp.arange(na, dtype=I32)
    th_tw = ((jnp.arange(nb, dtype=I32)[:, None] * ka[None, :]) % n).astype(F32) * (2.0 * math.pi / n)
    th_f = ((ka[:, None] * ka[None, :]) % na).astype(F32) * (2.0 * math.pi / na)
    twr, twi = (jnp.cos(th_tw) * s1)[:, :, None], (-jnp.sin(th_tw) * s1)[:, :, None]
    fr, fi = jnp.cos(th_f)[None, :, :], -jnp.sin(th_f)[None, :, :]
    tr, ti = twr * fr - twi * fi, twr * fi + twi * fr
    t1 = jnp.concatenate([jnp.concatenate([tr, ti], axis=-1),
                          jnp.concatenate([ti, -tr], axis=-1)], axis=-2).astype(BF16)
    kb = jnp.arange(nb, dtype=I32)
    th2 = ((kb[:, None] * kb[None, :]) % nb).astype(F32) * (2.0 * math.pi / nb)
    t2 = (jnp.concatenate([jnp.cos(th2), jnp.sin(th2)], axis=-1) * s2).astype(BF16)
    return t1, t2


def _dft1_kernel(z_ref, t_ref, y_ref):
    planes, _, rows, _ = z_ref.shape
    step = SUBLANES_F32
    half, na = planes // 2, rows // step
    for j in range(step):
        sel = pl.ds(j, na, stride=step)
        p = jnp.concatenate([z_ref[cc, 0, sel, :] for cc in range(half)], axis=1)
        q = jnp.concatenate([z_ref[cc, 0, sel, :] for cc in range(half, planes)], axis=1)
        xs = jnp.concatenate([p, q], axis=0).astype(BF16)
        y = jnp.dot(t_ref[j], xs, preferred_element_type=F32)
        for cc in range(half):
            y_ref[cc, 0, sel, :] = y[:na, cc * LANES:(cc + 1) * LANES]
            y_ref[half + cc, 0, sel, :] = y[na:, cc * LANES:(cc + 1) * LANES]


def _dft2_kernel(y_ref, t_ref, o_ref):
    planes, groups, rows, _ = y_ref.shape
    step = SUBLANES_F32
    half, nb = planes // 2, groups * step

    def tokens_b(cc, j):
        return y_ref[cc, :, j * step:(j + 1) * step, :].reshape(nb, LANES)

    for j in range(rows // step):
        yr = jnp.concatenate([tokens_b(cc, j) for cc in range(half)], axis=1)
        yi = jnp.concatenate([tokens_b(cc, j) for cc in range(half, planes)], axis=1)
        xs = jnp.concatenate([yr, yi], axis=0).astype(BF16)
        out = jnp.dot(t_ref[...], xs, preferred_element_type=F32)
        for cc in range(half):
            o_ref[cc, 0, pl.ds(j, nb, stride=step), :] = out[:, cc * LANES:(cc + 1) * LANES]


def _attn_dft_kernel(*refs, n_total, tq, first_step, stage):
    attn_in, (d_in, tab), (o_ref, d_out) = refs[:12], refs[12:14], refs[14:]
    _attn_body(*attn_in, o_ref, n_total=n_total, tq=tq, first_step=first_step)
    (_dft1_kernel if stage == 1 else _dft2_kernel)(d_in, tab, d_out)


def _attention_and_dft(q, k, v, kc, vc, sink, z):
    n, m = q.shape[0], kc.shape[0]
    planes, groups, rows, _ = z.shape
    step = SUBLANES_F32
    nb, na = groups * step, rows // step
    t1, t2 = _dft_tables(na, nb)
    bias_lo, bias_hi = _band_biases()
    nblk = n // WINDOW
    half = n // 2

    def call(stage, steps, first_row, d_in, tab, d_in_spec, tab_spec, d_out_shape, d_out_spec):
        tq = half // steps
        nsub = tq // WINDOW
        first_step = first_row // tq
        main = lambda i, s: (i + first_step, 0)
        prev = lambda i, s: (jnp.maximum((i + first_step) * nsub - 1, 0), 0)
        nxt = lambda i, s: (jnp.minimum((i + first_step) * nsub + nsub, nblk - 1), 0)
        const = lambda i, s: (0, 0)
        kv_specs = [pl.BlockSpec((WINDOW, KV_DIM), prev), pl.BlockSpec((tq, KV_DIM), main),
                    pl.BlockSpec((WINDOW, KV_DIM), nxt)]
        bias_spec = pl.BlockSpec(bias_lo.shape, lambda i, s: (0, 0, 0))
        return pl.pallas_call(
            functools.partial(_attn_dft_kernel, n_total=n, tq=tq, first_step=first_step,
                              stage=stage),
            out_shape=(jax.ShapeDtypeStruct((half, Q_DIM), BF16), d_out_shape),
            grid_spec=pltpu.PrefetchScalarGridSpec(
                num_scalar_prefetch=1,
                grid=(steps,),
                in_specs=[pl.BlockSpec((tq, Q_DIM), main)] + kv_specs + kv_specs
                         + [pl.BlockSpec((m, KV_DIM), const), pl.BlockSpec((m, KV_DIM), const),
                            bias_spec, bias_spec, d_in_spec, tab_spec],
                out_specs=(pl.BlockSpec((tq, Q_DIM), lambda i, s: (i, 0)), d_out_spec)),
            compiler_params=_cparams("arbitrary"),
            name=f"attention_half{stage}_dft_stage{stage}",
        )(sink, q, k, k, k, v, v, v, kc, vc, bias_lo, bias_hi, d_in, tab)

    ax_first, y = call(
        1, groups, 0, z, t1,
        pl.BlockSpec((planes, 1, rows, LANES), lambda i, s: (0, i, 0, 0)),
        pl.BlockSpec((step, 2 * na, 2 * na), lambda i, s: (i, 0, 0)),
        jax.ShapeDtypeStruct(z.shape, F32),
        pl.BlockSpec((planes, 1, rows, LANES), lambda i, s: (0, i, 0, 0)))
    ax_second, fx = call(
        2, na // step, half, y, t2,
        pl.BlockSpec((planes, groups, step * step, LANES), lambda i, s: (0, 0, i, 0)),
        pl.BlockSpec((nb, 2 * nb), lambda i, s: (0, 0)),
        jax.ShapeDtypeStruct((planes // 2, na // step, nb * step, LANES), F32),
        pl.BlockSpec((planes // 2, 1, nb * step, LANES), lambda i, s: (0, i, 0, 0)))
    return ax_first, ax_second, fx


def _outproj_kernel(ax1_ref, ax2_ref, fx_ref, x_ref, wo_ref, gate_ref, nw_ref, sh_ref, sc_ref,
                    wr_ref, x1_ref, g_ref, aff_ref):
    planes, groups, rows, _ = fx_ref.shape
    in_first_half = pl.program_id(0) < pl.num_programs(0) // 2
    na = groups * SUBLANES_F32
    n_kb = rows // SUBLANES_F32
    kb_per_chunk = max(1, n_kb // OUTPROJ_CHUNKS)
    for first_kb in range(0, n_kb, kb_per_chunk):
        rs = slice(first_kb * na, (first_kb + kb_per_chunk) * na)
        fx = jnp.concatenate(
            [jnp.concatenate(
                [fx_ref[cc, :, kb * SUBLANES_F32:(kb + 1) * SUBLANES_F32, :].reshape(na, LANES)
                 for kb in range(first_kb, first_kb + kb_per_chunk)], axis=0)
             for cc in range(planes)], axis=1).astype(BF16)
        ax = jnp.where(in_first_half, ax1_ref[rs, :], ax2_ref[rs, :])
        acc = (jnp.dot(ax, wo_ref[:Q_DIM, :], preferred_element_type=F32)
               + jnp.dot(fx, wo_ref[Q_DIM:, :], preferred_element_type=F32))
        x1 = x_ref[rs, :] + gate_ref[...] * acc
        x1_ref[rs, :] = x1
        gb = _rms_mod(x1, nw_ref[...], sh_ref[...], sc_ref[...]).astype(BF16)
        g_ref[rs, :] = gb
        logits = jnp.dot(gb, wr_ref[...], preferred_element_type=F32)
        e = jnp.exp(logits - jnp.max(logits, axis=1, keepdims=True))
        aff_ref[rs, :] = e / jnp.sum(e, axis=1, keepdims=True)


def _output_projection(ax1, ax2, fx, x, wo_b, gate, nw, shift, scale, wr_b, tm=512):
    n, d = x.shape
    ne = wr_b.shape[1]
    half_steps = n // tm // 2
    row = lambda i: (i, 0)
    const = lambda i: (0, 0)
    vec = pl.BlockSpec((1, d), const)
    return pl.pallas_call(
        _outproj_kernel,
        out_shape=(jax.ShapeDtypeStruct((n, d), F32), jax.ShapeDtypeStruct((n, d), BF16),
                   jax.ShapeDtypeStruct((n, ne), F32)),
        grid=(n // tm,),
        in_specs=[pl.BlockSpec((tm, Q_DIM), lambda i: (jnp.minimum(i, half_steps - 1), 0)),
                  pl.BlockSpec((tm, Q_DIM), lambda i: (jnp.maximum(i - half_steps, 0), 0)),
                  pl.BlockSpec((fx.shape[0], fx.shape[1], tm // (fx.shape[1] * SUBLANES_F32)
                                * SUBLANES_F32, LANES), lambda i: (0, 0, i, 0)),
                  pl.BlockSpec((tm, d), row),
                  pl.BlockSpec((Q_DIM + F_DIM, d), const, pipeline_mode=pl.Buffered(1)),
                  vec, vec, vec, vec, pl.BlockSpec((d, ne), const)],
        out_specs=(pl.BlockSpec((tm, d), row), pl.BlockSpec((tm, d), row),
                   pl.BlockSpec((tm, ne), row)),
        compiler_params=_cparams("arbitrary"),
        name="output_projection",
    )(ax1, ax2, fx, x, wo_b, gate, nw, shift, scale, wr_b)


def _route_kernel(aff_ref, posm_ref, cnt_ref, off_ref, *, cap, n_exp, n_chunk):
    aff_all = aff_ref[...]
    blocks = [aff_all[e * n_chunk:(e + 1) * n_chunk] for e in range(n_exp)]

    def total(mask):
        s = jnp.sum(jnp.where(mask, 1.0, 0.0), axis=0, keepdims=True)
        return jnp.sum(s, axis=1, keepdims=True)

    def search(it, thr_bits):
        bit = jnp.left_shift(jnp.int32(1), 30 - it)
        out = []
        for e in range(n_exp):
            cand = thr_bits[e] | bit
            enough = total(blocks[e] >= lax.bitcast_convert_type(cand, F32)) >= cap
            out.append(jnp.where(enough, cand, thr_bits[e]))
        return tuple(out)

    thr_bits = lax.fori_loop(0, 31, search, tuple(jnp.zeros((1, 1), I32) for _ in range(n_exp)))
    thr = [lax.bitcast_convert_type(t, F32) for t in thr_bits]

    rl = lax.broadcasted_iota(I32, (LANES, LANES), 0)
    cl = lax.broadcasted_iota(I32, (LANES, LANES), 1)
    before = jnp.where(rl < cl, 1.0, 0.0).astype(BF16)
    ones = jnp.ones((LANES, LANES), BF16)
    rc = lax.broadcasted_iota(I32, (n_chunk, n_chunk), 0)
    cc = lax.broadcasted_iota(I32, (n_chunk, n_chunk), 1)
    earlier = jnp.where(cc < rc, 1.0, 0.0).astype(BF16)

    def excl_cumsum(mask):
        mb = jnp.where(mask, 1.0, 0.0).astype(BF16)
        within = jnp.dot(mb, before, preferred_element_type=F32)
        rowtot = jnp.dot(mb, ones, preferred_element_type=F32)
        choff = jnp.dot(earlier, rowtot.astype(BF16), preferred_element_type=F32)
        return within + choff, rowtot, choff

    for e in range(n_exp):
        gt = blocks[e] > thr[e]
        eq = blocks[e] == thr[e]
        need = cap - total(gt)
        tie_rank, _, _ = excl_cumsum(eq)
        sel = gt | (eq & (tie_rank < need))
        pos, rowtot, choff = excl_cumsum(sel)
        rs = slice(e * n_chunk, (e + 1) * n_chunk)
        posm_ref[rs, :] = jnp.where(sel, pos.astype(I32), NOT_SELECTED)
        cnt_ref[rs, :] = rowtot.astype(I32)
        off_ref[rs, :] = choff.astype(I32)


def _route(aff2, cap, n_exp):
    rows = aff2.shape[0]
    n_chunk = rows // n_exp
    spec = pl.BlockSpec((rows, LANES), lambda i: (0, 0))
    shp = jax.ShapeDtypeStruct((rows, LANES), I32)
    return pl.pallas_call(
        functools.partial(_route_kernel, cap=cap, n_exp=n_exp, n_chunk=n_chunk),
        out_shape=(shp, shp, shp),
        grid=(1,),
        in_specs=[spec],
        out_specs=(spec, spec, spec),
        compiler_params=_cparams("arbitrary"),
        name="expert_choice_routing",
    )(aff2)


def _window_matches(posm, firsts, win_starts):
    r = lax.broadcasted_iota(I32, (WINDOW_ROWS, TOK_TILE), 0)
    out = []
    for e, (first, start) in enumerate(zip(firsts, win_starts)):
        pm = posm[e:e + 1, :]
        pm = jnp.where((pm >= first) & (pm < first + PIECE), pm, NOT_SELECTED)
        out.append((pm - start) == r)
    return out


def _align_down(v):
    return (v // ROW_ALIGN) * ROW_ALIGN


def _for_window_size(rows_needed, fn):
    below = None
    for k, size in enumerate(WINDOW_SIZES):
        fits = rows_needed <= size if k + 1 < len(WINDOW_SIZES) else None
        cond = fits if below is None else (below if fits is None else jnp.logical_and(below, fits))
        pl.when(cond)(functools.partial(fn, size))
        below = rows_needed > size


def _gather_kernel(off_ref, cnt_ref, g_ref, posm_ref, aff_ref, xs_hbm,
                   stg, tail, sem, npass_ref, sent_ref, *, n_exp, n_chunk, cap):
    c = pl.program_id(0)
    stride = cap + WINDOW_ROWS

    @pl.when(c == 0)
    def _():
        npass_ref[0] = 0
        tail[...] = jnp.zeros_like(tail)

    def window_copy(slot, e, dst, size=WINDOW_ROWS):
        return pltpu.make_async_copy(stg.at[slot, pl.ds(e * WINDOW_ROWS, size)],
                                     xs_hbm.at[pl.ds(dst, size)], sem.at[slot, e])

    def start_window(slot, e, dst, rows_used):
        _for_window_size(rows_used, lambda size: window_copy(slot, e, dst, size).start())
        sent_ref[slot * n_exp + e] = rows_used

    def wait_window(slot, e):
        _for_window_size(sent_ref[slot * n_exp + e],
                         lambda size: window_copy(slot, e, 0, size).wait())

    @pl.when(c == 0)
    def _():
        stg[0] = jnp.zeros(stg.shape[1:], stg.dtype)
        for e in range(n_exp):
            window_copy(0, e, e * stride + cap).start()
        for e in range(n_exp):
            window_copy(0, e, 0).wait()

    most = cnt_ref[c]
    for e in range(1, n_exp):
        most = jnp.maximum(most, cnt_ref[e * n_chunk + c])
    passes = (most + PIECE - 1) // PIECE

    def one_pass(p, carry):
        done = npass_ref[0]
        slot = done % 2
        posm = posm_ref[0]
        aff = aff_ref[0]
        firsts, win_starts, next_shift, rows_used = [], [], [], []
        for e in range(n_exp):
            o, n_e = off_ref[e * n_chunk + c], cnt_ref[e * n_chunk + c]
            first = o + jnp.minimum(p * PIECE, n_e)
            after = o + jnp.minimum((p + 1) * PIECE, n_e)
            firsts.append(first)
            win_starts.append(_align_down(first))
            next_shift.append(_align_down(after) - _align_down(first))
            rows_used.append(after - _align_down(first))
        matches = _window_matches(posm, firsts, win_starts)
        onehot = jnp.concatenate([jnp.where(mt, 1.0, 0.0).astype(BF16) for mt in matches], axis=0)
        new = jnp.dot(onehot, g_ref[...], preferred_element_type=F32)
        lane = lax.broadcasted_iota(I32, (WINDOW_ROWS, GATE_LANES), 1)
        for e in range(n_exp):
            lo = e * WINDOW_ROWS
            gate = jnp.sum(jnp.where(matches[e], aff[e:e + 1, :], 0.0), axis=1, keepdims=True)
            hi = gate.astype(BF16).astype(F32)
            mid = (gate - hi).astype(BF16).astype(F32)
            low = gate - hi - mid
            parts = jnp.where(lane == 0, hi, jnp.where(lane == 1, mid,
                                                       jnp.where(lane == 2, low, 0.0)))
            rows = jnp.concatenate([new[lo:lo + WINDOW_ROWS], parts], axis=1)
            stg[slot, lo:lo + ROW_ALIGN] = (rows[:ROW_ALIGN] + tail[e].astype(F32)).astype(BF16)
            stg[slot, lo + ROW_ALIGN:lo + WINDOW_ROWS] = rows[ROW_ALIGN:].astype(BF16)

        dsts = [pl.multiple_of(e * stride + win_starts[e], ROW_ALIGN) for e in range(n_exp)]

        @pl.when(done > 0)
        def _():
            for e in range(n_exp):
                wait_window(1 - slot, e)
                start_window(slot, e, dsts[e], rows_used[e])

        @pl.when(done == 0)
        def _():
            for e in range(n_exp):
                start_window(slot, e, dsts[e], rows_used[e])

        for e in range(n_exp):
            src = pl.ds(pl.multiple_of(e * WINDOW_ROWS + next_shift[e], ROW_ALIGN), ROW_ALIGN)
            tail[e] = stg[slot, src, :]
        npass_ref[0] = done + 1
        return carry

    lax.fori_loop(0, passes, one_pass, 0)

    @pl.when(c == pl.num_programs(0) - 1)
    def _():
        done = npass_ref[0]

        @pl.when(done > 0)
        def _():
            for e in range(n_exp):
                wait_window((done - 1) % 2, e)


def _gather(off, cnt, g, posm_t, aff_t, cap):
    n, d = g.shape
    n_chunk, n_exp, _ = posm_t.shape
    rows = n_exp * (cap + WINDOW_ROWS)
    tile3 = lambda i, o, c: (i, 0, 0)
    return pl.pallas_call(
        functools.partial(_gather_kernel, n_exp=n_exp, n_chunk=n_chunk, cap=cap),
        out_shape=jax.ShapeDtypeStruct((rows, d + GATE_LANES), BF16),
        grid_spec=pltpu.PrefetchScalarGridSpec(
            num_scalar_prefetch=2,
            grid=(n_chunk,),
            in_specs=[pl.BlockSpec((TOK_TILE, d), lambda i, o, c: (i, 0)),
                      pl.BlockSpec((1, n_exp, TOK_TILE), tile3),
                      pl.BlockSpec((1, n_exp, TOK_TILE), tile3)],
            out_specs=pl.BlockSpec(memory_space=pl.ANY),
            scratch_shapes=[pltpu.VMEM((2, n_exp * WINDOW_ROWS, d + GATE_LANES), BF16),
                            pltpu.VMEM((n_exp, ROW_ALIGN, d + GATE_LANES), BF16),
                            pltpu.SemaphoreType.DMA((2, n_exp)),
                            pltpu.SMEM((1,), I32),
                            pltpu.SMEM((2 * n_exp,), I32)]),
        compiler_params=_cparams("arbitrary"),
        name="gather_expert_tokens",
    )(off, cnt, g, posm_t, aff_t)


def _expert_kernel(xs_ref, wg_ref, wu_ref, wd_ref, y_ref, h_ref, gate_ref, *, nf, tf, tn):
    j = pl.program_id(1)
    d = wg_ref.shape[1]
    cap = xs_ref.shape[1]
    halves = [slice(r * (cap // EXPERT_ROW_SPLIT), (r + 1) * (cap // EXPERT_ROW_SPLIT))
              for r in range(EXPERT_ROW_SPLIT)]

    @pl.when(j == 0)
    def _():
        parts = xs_ref[0, :, d:].astype(F32)
        gate = parts[:, 0:1] + parts[:, 1:2] + parts[:, 2:3]
        gate_ref[...] = jnp.broadcast_to(gate, gate_ref.shape)

    @pl.when(j < nf)
    def _():
        wg = wg_ref[0].astype(BF16)
        wu = wu_ref[0].astype(BF16)
        for rs in halves:
            xs = xs_ref[0, rs, :d]
            gp = jnp.dot(xs, wg, preferred_element_type=F32)
            up = jnp.dot(xs, wu, preferred_element_type=F32)
            h = ((gp * jax.nn.sigmoid(gp)) * up).astype(BF16)
            for jj in range(nf):
                @pl.when(j == jj)
                def _():
                    h_ref[rs, jj * tf:(jj + 1) * tf] = h

    @pl.when(j >= nf)
    def _():
        wd = wd_ref[0].astype(BF16)
        for rs in halves:
            y = jnp.dot(h_ref[rs, :], wd, preferred_element_type=F32)
            y_ref[0, rs, :] = (y * gate_ref[rs, 0:1]).astype(BF16)


def _experts(xs3, w_gate, w_up, w_down, cap, tf=512, tn=512):
    n_exp, d, ff = w_gate.shape
    tf, tn = min(tf, ff), min(tn, d)
    nf, nn = ff // tf, d // tn

    def ahead(e, j):
        return jnp.minimum(e + (j >= nf).astype(I32), n_exp - 1)

    def ff_tile(e, j):
        return jnp.where(j >= nf, jnp.where(e == n_exp - 1, nf - 1, 0), j)

    return pl.pallas_call(
        functools.partial(_expert_kernel, nf=nf, tf=tf, tn=tn),
        out_shape=jax.ShapeDtypeStruct((n_exp, cap, d), BF16),
        grid=(n_exp, nf + nn),
        in_specs=[pl.BlockSpec((1, cap, d + GATE_LANES), lambda e, j: (ahead(e, j), 0, 0)),
                  pl.BlockSpec((1, d, tf), lambda e, j: (ahead(e, j), 0, ff_tile(e, j))),
                  pl.BlockSpec((1, d, tf), lambda e, j: (ahead(e, j), 0, ff_tile(e, j))),
                  pl.BlockSpec((1, ff, tn), lambda e, j: (e, 0, jnp.maximum(j - nf, 0)))],
        out_specs=pl.BlockSpec((1, cap, tn), lambda e, j: (e, 0, jnp.maximum(j - nf, 0))),
        scratch_shapes=[pltpu.VMEM((cap, ff), BF16), pltpu.VMEM((cap, LANES), F32)],
        compiler_params=_cparams("arbitrary", "arbitrary"),
        name="swiglu_experts",
    )(xs3, w_gate, w_up, w_down)


def _combine_kernel(off_ref, cnt_ref, y_hbm, posm_ref, x1_ref, gate_ref, nw_ref, o_ref,
                    stg, acc_ref, sem, *, n_exp, n_chunk, cap):
    c = pl.program_id(0)
    last_start = n_exp * cap - WINDOW_ROWS

    def firsts_of(tile, p):
        return [off_ref[e * n_chunk + tile] + p * PIECE for e in range(n_exp)]

    def starts_of(tile, p):
        return [pl.multiple_of(jnp.minimum(e * cap + _align_down(first), last_start), ROW_ALIGN)
                for e, first in enumerate(firsts_of(tile, p))]

    def rows_used_of(tile, p):
        out = []
        for e, (first, start) in enumerate(zip(firsts_of(tile, p), starts_of(tile, p))):
            left = jnp.clip(cnt_ref[e * n_chunk + tile] - p * PIECE, 0, PIECE)
            out.append(e * cap + first + left - start)
        return out

    def piece_copy(slot, e, src, size=WINDOW_ROWS):
        return pltpu.make_async_copy(y_hbm.at[pl.ds(src, size)],
                                     stg.at[slot, pl.ds(e * WINDOW_ROWS, size)], sem.at[slot])

    def fetch(tile, p, slot):
        for e, (src, used) in enumerate(zip(starts_of(tile, p), rows_used_of(tile, p))):
            _for_window_size(used, lambda size: piece_copy(slot, e, src, size).start())

    def wait_fetch(tile, p, slot):
        for e, used in enumerate(rows_used_of(tile, p)):
            _for_window_size(used, lambda size: piece_copy(slot, e, 0, size).wait())

    def contribution(p, slot, tokens):
        starts = [s - e * cap for e, s in enumerate(starts_of(c, p))]
        matches = _window_matches(posm_ref[0], firsts_of(c, p), starts)
        onehot = jnp.concatenate([jnp.where(mt[:, tokens], 1.0, 0.0).astype(BF16)
                                  for mt in matches], axis=0)
        return lax.dot_general(onehot, stg[slot], (((0,), (0,)), ((), ())),
                               preferred_element_type=F32)

    def finish(tokens, moe):
        x2 = x1_ref[tokens, :] + gate_ref[...] * moe
        ms = jnp.mean(x2 * x2, axis=-1, keepdims=True)
        o_ref[tokens, :] = x2 * lax.rsqrt(ms + RMS_EPS) * nw_ref[...]

    slot = c % 2

    @pl.when(c == 0)
    def _():
        stg[...] = jnp.zeros_like(stg)
        fetch(0, 0, 0)

    @pl.when(c + 1 < pl.num_programs(0))
    def _():
        fetch(c + 1, 0, 1 - slot)

    most = cnt_ref[c]
    for e in range(1, n_exp):
        most = jnp.maximum(most, cnt_ref[e * n_chunk + c])
    passes = (most + PIECE - 1) // PIECE
    wait_fetch(c, 0, slot)

    @pl.when(passes <= 1)
    def _():
        for t0 in range(0, TOK_TILE, LANES):
            tokens = slice(t0, t0 + LANES)
            finish(tokens, contribution(0, slot, tokens))

    @pl.when(passes > 1)
    def _():
        everyone = slice(0, TOK_TILE)
        acc_ref[...] = contribution(0, slot, everyone)

        def extra_pass(p, carry):
            fetch(c, p, slot)
            wait_fetch(c, p, slot)
            acc_ref[...] += contribution(p, slot, everyone)
            return carry

        lax.fori_loop(1, passes, extra_pass, 0)
        finish(everyone, acc_ref[...])


def _combine(off, cnt, y, posm_t, x1, gate, nw, cap):
    n, d = x1.shape
    n_chunk, n_exp, _ = posm_t.shape
    return pl.pallas_call(
        functools.partial(_combine_kernel, n_exp=n_exp, n_chunk=n_chunk, cap=cap),
        out_shape=jax.ShapeDtypeStruct((n, d), F32),
        grid_spec=pltpu.PrefetchScalarGridSpec(
            num_scalar_prefetch=2,
            grid=(n_chunk,),
            in_specs=[pl.BlockSpec(memory_space=pl.ANY),
                      pl.BlockSpec((1, n_exp, TOK_TILE), lambda i, o, c: (i, 0, 0)),
                      pl.BlockSpec((TOK_TILE, d), lambda i, o, c: (i, 0)),
                      pl.BlockSpec((1, d), lambda i, o, c: (0, 0)),
                      pl.BlockSpec((1, d), lambda i, o, c: (0, 0))],
            out_specs=pl.BlockSpec((TOK_TILE, d), lambda i, o, c: (i, 0)),
            scratch_shapes=[pltpu.VMEM((2, n_exp * WINDOW_ROWS, d), BF16),
                            pltpu.VMEM((TOK_TILE, d), F32),
                            pltpu.SemaphoreType.DMA((2,))]),
        compiler_params=_cparams("arbitrary"),
        name="combine_experts_final_norm",
    )(off, cnt, y, posm_t, x1, gate, nw)


def kernel(x, c, ctx, c_ctx, w_mod, b_mod, norm_mix, w_in, sink, w_fourier, w_out, norm_ffn,
           w_router, w_gate, w_up, w_down, norm_final):
    assert x.shape[0] == 1 and w_mod.shape[0] == 1
    n, d = x.shape[1], x.shape[2]
    xl, cx = x[0], ctx[0]
    n_exp = w_router.shape[2]
    cap = max(1, CAP_FACTOR * n // n_exp)
    n_chunk = n // LANES
    n_tile = n // TOK_TILE
    per_tile = TOK_TILE // LANES

    mod = _modulation(jnp.stack([c[0], c_ctx], axis=1), w_mod[0], b_mod[0][None, :])
    mx = mod[0].reshape(6, 1, d)
    mc = mod[1].reshape(6, 1, d)

    w_in_b = w_in[0].astype(BF16)
    nmix = norm_mix[0][None, :]
    ab = _fold_channel_dft(w_fourier[0])
    q, k, v, z = _project_latent(xl, nmix, mx[0], mx[1], w_in_b, ab)
    kc, vc = _project_context(cx, nmix, mc[0], mc[1], w_in_b[:, Q_DIM:Q_DIM + 2 * KV_DIM])
    ax1, ax2, fx = _attention_and_dft(q, k, v, kc, vc, sink[0], z)

    x1, g, aff = _output_projection(ax1, ax2, fx, xl, w_out[0].astype(BF16), mx[2], norm_ffn[0][None, :],
                                    mx[3], mx[4], w_router[0].astype(BF16))

    aff2 = aff.T.reshape(n_exp * n_chunk, LANES)
    posm, cnt, off = _route(aff2, cap, n_exp)
    by_tile = lambda a: a.reshape(n_exp, n_tile, TOK_TILE).transpose(1, 0, 2)
    posm_t, aff_t = by_tile(posm), by_tile(aff2)
    cnt1 = cnt[:, 0].reshape(n_exp * n_tile, per_tile).sum(axis=1)
    off1 = off[:, 0].reshape(n_exp * n_tile, per_tile)[:, 0]

    xs = _gather(off1, cnt1, g, posm_t, aff_t, cap)
    y = _experts(xs.reshape(n_exp, cap + WINDOW_ROWS, d + GATE_LANES),
                 w_gate[0], w_up[0], w_down[0], cap)
    out = _combine(off1, cnt1, y.reshape(n_exp * cap, d), posm_t, x1, mx[5], norm_final[None, :],
                   cap)
    return out[None]
```

```python
import functools
import math

import jax
import jax.numpy as jnp
from jax import lax
from jax.experimental import pallas as pl
from jax.experimental.pallas import tpu as pltpu

F32 = jnp.float32
BF16 = jnp.bfloat16
I32 = jnp.int32

HEAD_DIM = 128
N_HEADS = 8
N_KV_HEADS = 2
GROUP = N_HEADS // N_KV_HEADS
WINDOW = 128
Q_DIM = N_HEADS * HEAD_DIM
KV_DIM = N_KV_HEADS * HEAD_DIM
N_FGROUPS = 4
FG = 256
F_DIM = N_FGROUPS * FG
D_IN = Q_DIM + 2 * KV_DIM + F_DIM
N_EXPERTS = 16
CAP_FACTOR = 2
GRID_W = 64
ROPE_THETA = 10000.0
RMS_EPS = 1e-6
NEG_INF = -1e30
LOG2_E = math.log2(math.e)

LANES = 128
SUBLANES_F32 = 8
VMEM_LIMIT_BYTES = 56 * 1024 * 1024

TOK_TILE = 2 * LANES
PIECE = 48
ROW_ALIGN = 16
WINDOW_ROWS = ROW_ALIGN + PIECE
WINDOW_SIZES = tuple(range(2 * ROW_ALIGN, WINDOW_ROWS + 1, ROW_ALIGN))
GATE_LANES = LANES
DFT_NB = LANES
OUTPROJ_CHUNKS = 2
PROJ_CHUNKS = 2
EXPERT_ROW_SPLIT = 2
NOT_SELECTED = -(1 << 20)


def _cparams(*sem):
    return pltpu.CompilerParams(dimension_semantics=sem, vmem_limit_bytes=VMEM_LIMIT_BYTES)


def _rms_mod(x, nw, shift, scale):
    ms = jnp.mean(x * x, axis=-1, keepdims=True)
    return x * lax.rsqrt(ms + RMS_EPS) * (nw * (1.0 + scale)) + shift


def _mod_kernel(ct_ref, w_ref, b_ref, o_ref):
    ct = ct_ref[...]
    s = ct * jax.nn.sigmoid(ct)
    w = w_ref[...]
    r0 = jnp.sum(s[:, 0:1] * w, axis=0, keepdims=True)
    r1 = jnp.sum(s[:, 1:2] * w, axis=0, keepdims=True)
    o_ref[...] = jnp.concatenate([r0, r1], axis=0) + b_ref[...]


def _modulation(ct, w_mod, b_mod):
    d, n = w_mod.shape
    tn = math.gcd(n, 1024)
    return pl.pallas_call(
        _mod_kernel,
        out_shape=jax.ShapeDtypeStruct((2, n), F32),
        grid=(n // tn,),
        in_specs=[pl.BlockSpec((d, 2), lambda i: (0, 0)),
                  pl.BlockSpec((d, tn), lambda i: (0, i)),
                  pl.BlockSpec((1, tn), lambda i: (0, i))],
        out_specs=pl.BlockSpec((2, tn), lambda i: (0, i)),
        compiler_params=_cparams("arbitrary"),
        name="modulation",
    )(ct, w_mod, b_mod)


def _ab_kernel(cf_ref, sf_ref, wf_ref, ab_ref):
    wf = wf_ref[0]
    a = jnp.dot(cf_ref[...], wf, preferred_element_type=F32, precision=lax.Precision.HIGHEST)
    b = jnp.dot(sf_ref[...], wf, preferred_element_type=F32, precision=lax.Precision.HIGHEST)
    ab_ref[0] = jnp.concatenate([a, b], axis=1).astype(BF16)


def _fold_channel_dft(w_fourier):
    g, fg, _ = w_fourier.shape
    idx = jnp.arange(fg, dtype=I32)
    th = ((idx[:, None] * idx[None, :]) % fg).astype(F32) * (2.0 * math.pi / fg)
    scale = 1.0 / math.sqrt(fg)
    cf = jnp.cos(th) * scale
    sf = jnp.sin(th) * scale
    return pl.pallas_call(
        _ab_kernel,
        out_shape=jax.ShapeDtypeStruct((g, fg, 2 * fg), BF16),
        grid=(g,),
        in_specs=[pl.BlockSpec((fg, fg), lambda i: (0, 0)),
                  pl.BlockSpec((fg, fg), lambda i: (0, 0)),
                  pl.BlockSpec((1, fg, fg), lambda i: (i, 0, 0))],
        out_specs=pl.BlockSpec((1, fg, 2 * fg), lambda i: (i, 0, 0)),
        compiler_params=_cparams("arbitrary"),
        name="fold_channel_dft",
    )(cf, sf, w_fourier)


def _rope_tables(n):
    quarter = HEAD_DIM // 4
    freqs = ROPE_THETA ** (-jnp.arange(quarter, dtype=F32) / quarter)
    zeros = lambda m: jnp.zeros((m, 2 * quarter), F32)

    def tables(pos, low_half):
        ang = pos[:, None] * freqs[None, :]
        ang = jnp.concatenate([ang, ang], axis=-1)
        cos, sin = jnp.cos(ang), jnp.sin(ang)
        first = (jnp.arange(2 * quarter) < quarter)[None, :]
        parts = (cos, jnp.where(first, -sin, 0.0), jnp.where(first, 0.0, sin))
        pad = zeros(pos.shape[0])
        return jnp.stack([jnp.concatenate([p, pad] if low_half else [pad, p], axis=-1)
                          for p in parts])

    row_tab = tables(jnp.arange(n // GRID_W, dtype=F32), True)
    col_tab = tables(jnp.arange(GRID_W, dtype=F32), False)
    return row_tab, col_tab


def _store_dft_rows(z_ref, plane, val, a0):
    groups = z_ref.shape[1]
    nb = groups * SUBLANES_F32
    for al in range(val.shape[0] // nb):
        z_ref[plane, :, (a0 + al) * SUBLANES_F32:(a0 + al + 1) * SUBLANES_F32, :] = (
            val[al * nb:(al + 1) * nb].reshape(groups, SUBLANES_F32, LANES))


def _proj_kernel(x_ref, nw_ref, sh_ref, sc_ref, w_ref, ab_ref, rt_ref, ct_ref, wo_ref,
                 q_ref, k_ref, v_ref, z_ref, wo_b_ref):
    wo_b_ref[...] = wo_ref[...].astype(BF16)
    quarter = HEAD_DIM // 4
    qscale = LOG2_E / math.sqrt(HEAD_DIM)
    tm = x_ref.shape[0]
    chunk = tm // PROJ_CHUNKS
    for r0 in range(0, tm, chunk):
        rs = slice(r0, r0 + chunk)
        hb = _rms_mod(x_ref[rs, :], nw_ref[...], sh_ref[...], sc_ref[...]).astype(BF16)
        cos, s_up, s_dn = (
            jnp.concatenate(
                [rt_ref[t, r:r + 1, :] + ct_ref[t]
                 for r in range(r0 // GRID_W, (r0 + chunk) // GRID_W)], axis=0)
            for t in range(3))

        def rope(t):
            return (t * cos + pltpu.roll(t, HEAD_DIM - quarter, 1) * s_up
                    + pltpu.roll(t, quarter, 1) * s_dn)

        q = jnp.dot(hb, w_ref[:, :Q_DIM], preferred_element_type=F32)
        for j in range(N_HEADS):
            sl = slice(j * HEAD_DIM, (j + 1) * HEAD_DIM)
            q_ref[rs, sl] = (rope(q[:, sl]) * qscale).astype(BF16)
        k = jnp.dot(hb, w_ref[:, Q_DIM:Q_DIM + KV_DIM], preferred_element_type=F32)
        for j in range(N_KV_HEADS):
            sl = slice(j * HEAD_DIM, (j + 1) * HEAD_DIM)
            k_ref[rs, sl] = rope(k[:, sl]).astype(BF16)
        v = jnp.dot(hb, w_ref[:, Q_DIM + KV_DIM:Q_DIM + 2 * KV_DIM], preferred_element_type=F32)
        v_ref[rs, :] = v.astype(BF16)
        u = jnp.dot(hb, w_ref[:, Q_DIM + 2 * KV_DIM:], preferred_element_type=F32).astype(BF16)
        for g in range(N_FGROUPS):
            pq = jnp.dot(u[:, g * FG:(g + 1) * FG], ab_ref[g], preferred_element_type=F32)
            per_group = FG // LANES
            for t in range(2 * per_group):
                plane = (t // per_group) * (F_DIM // LANES) + g * per_group + t % per_group
                _store_dft_rows(z_ref, plane, pq[:, t * LANES:(t + 1) * LANES], r0 // DFT_NB)


def _project_latent(x, nw, shift, scale, w_in_b, ab, w_out, tm=512):
    n, d = x.shape
    wo_rows = w_out.shape[0] // (n // tm)
    row_tab, col_tab = _rope_tables(n)
    row = lambda i: (i, 0)
    const2 = lambda i: (0, 0)
    return pl.pallas_call(
        _proj_kernel,
        out_shape=(jax.ShapeDtypeStruct((n, Q_DIM), BF16),
                   jax.ShapeDtypeStruct((n, KV_DIM), BF16),
                   jax.ShapeDtypeStruct((n, KV_DIM), BF16),
                   jax.ShapeDtypeStruct((2 * F_DIM // LANES, DFT_NB // SUBLANES_F32,
                                         (n // DFT_NB) * SUBLANES_F32, LANES), F32),
                   jax.ShapeDtypeStruct(w_out.shape, BF16)),
        grid=(n // tm,),
        in_specs=[pl.BlockSpec((tm, d), row),
                  pl.BlockSpec((1, d), const2), pl.BlockSpec((1, d), const2),
                  pl.BlockSpec((1, d), const2),
                  pl.BlockSpec((d, D_IN), const2, pipeline_mode=pl.Buffered(1)),
                  pl.BlockSpec((N_FGROUPS, FG, 2 * FG), lambda i: (0, 0, 0),
                               pipeline_mode=pl.Buffered(1)),
                  pl.BlockSpec((3, tm // GRID_W, HEAD_DIM), lambda i: (0, i, 0)),
                  pl.BlockSpec((3, GRID_W, HEAD_DIM), lambda i: (0, 0, 0)),
                  pl.BlockSpec((wo_rows, w_out.shape[1]), row)],
        out_specs=(pl.BlockSpec((tm, Q_DIM), row), pl.BlockSpec((tm, KV_DIM), row),
                   pl.BlockSpec((tm, KV_DIM), row),
                   pl.BlockSpec((2 * F_DIM // LANES, DFT_NB // SUBLANES_F32,
                                 (tm // DFT_NB) * SUBLANES_F32, LANES), lambda i: (0, 0, i, 0)),
                   pl.BlockSpec((wo_rows, w_out.shape[1]), row)),
        compiler_params=_cparams("arbitrary"),
        name="project_latent",
    )(x, nw, shift, scale, w_in_b, ab, row_tab, col_tab, w_out)


def _ctx_kernel(x_ref, nw_ref, sh_ref, sc_ref, w_ref, k_ref, v_ref):
    hb = _rms_mod(x_ref[...], nw_ref[...], sh_ref[...], sc_ref[...]).astype(BF16)
    kv = jnp.dot(hb, w_ref[...], preferred_element_type=F32)
    k_ref[...] = kv[:, :KV_DIM].astype(BF16)
    v_ref[...] = kv[:, KV_DIM:].astype(BF16)


def _project_context(ctx, nw, shift, scale, w_kv_b):
    m, d = ctx.shape
    full = lambda shp: pl.BlockSpec(shp, lambda i: (0, 0))
    return pl.pallas_call(
        _ctx_kernel,
        out_shape=(jax.ShapeDtypeStruct((m, KV_DIM), BF16),
                   jax.ShapeDtypeStruct((m, KV_DIM), BF16)),
        grid=(1,),
        in_specs=[full((m, d)), full((1, d)), full((1, d)), full((1, d)), full((d, 2 * KV_DIM))],
        out_specs=(full((m, KV_DIM)), full((m, KV_DIM))),
        compiler_params=_cparams("arbitrary"),
        name="project_context",
    )(ctx, nw, shift, scale, w_kv_b)


def _attn_body(sink_ref, q_ref, kp_ref, km_ref, kn_ref, vp_ref, vm_ref, vn_ref,
               kc_ref, vc_ref, lo_ref, hi_ref, o_ref, *, n_total, tq, first_step):
    i = pl.program_id(0) + first_step
    nsub = tq // WINDOW
    last_blk = n_total // WINDOW - 1
    kwin = jnp.concatenate([kp_ref[...], km_ref[...], kn_ref[...]], axis=0)
    vwin = jnp.concatenate([vp_ref[...], vm_ref[...], vn_ref[...]], axis=0)
    rows = GROUP * WINDOW
    span = 3 * WINDOW
    head_of_row = lax.broadcasted_iota(I32, (rows, 1), 0) // WINDOW
    ones_loc = jnp.ones((span, HEAD_DIM), BF16)
    ones_ctx = jnp.ones((kc_ref.shape[0], HEAD_DIM), BF16)
    nt = (((1,), (1,)), ((), ()))
    for b in range(nsub):
        blk = i * nsub + b
        bias_lo = lo_ref[(blk == 0).astype(I32)]
        bias_hi = hi_ref[(blk == last_blk).astype(I32)]
        for h in range(N_KV_HEADS):
            hs = slice(h * HEAD_DIM, (h + 1) * HEAD_DIM)
            qs = jnp.concatenate(
                [q_ref[b * WINDOW:(b + 1) * WINDOW,
                       (h * GROUP + g) * HEAD_DIM:(h * GROUP + g + 1) * HEAD_DIM]
                 for g in range(GROUP)], axis=0)
            kw = kwin[b * WINDOW:b * WINDOW + span, hs]
            vw = vwin[b * WINDOW:b * WINDOW + span, hs]
            s_loc = lax.dot_general(qs, kw, nt, preferred_element_type=F32)
            s_ctx = lax.dot_general(qs, kc_ref[:, hs], nt, preferred_element_type=F32)
            parts = [s_loc[:, :WINDOW] + bias_lo, s_loc[:, WINDOW:2 * WINDOW],
                     s_loc[:, 2 * WINDOW:] + bias_hi, s_ctx]
            sink_col = jnp.zeros((rows, 1), F32)
            for g in range(GROUP):
                sink_col = jnp.where(head_of_row == g, sink_ref[h * GROUP + g] * LOG2_E, sink_col)
            blocks = parts[:3] + [s_ctx[:, t * WINDOW:(t + 1) * WINDOW]
                                  for t in range(s_ctx.shape[1] // WINDOW)]
            widest = blocks[0]
            for blk_scores in blocks[1:]:
                widest = jnp.maximum(widest, blk_scores)
            m = jnp.maximum(sink_col, jnp.max(widest, axis=1, keepdims=True))
            p = [jnp.exp2(part - m).astype(BF16) for part in parts]
            ov = (jnp.dot(jnp.concatenate(p[:3], axis=1), jnp.concatenate([vw, ones_loc], axis=1),
                          preferred_element_type=F32)
                  + jnp.dot(p[3], jnp.concatenate([vc_ref[:, hs], ones_ctx], axis=1),
                            preferred_element_type=F32))
            o = ov[:, :HEAD_DIM] / (ov[:, HEAD_DIM:] + jnp.exp2(sink_col - m))
            for g in range(GROUP):
                o_ref[b * WINDOW:(b + 1) * WINDOW,
                      (h * GROUP + g) * HEAD_DIM:(h * GROUP + g + 1) * HEAD_DIM] = (
                    o[g * WINDOW:(g + 1) * WINDOW].astype(BF16))


def _band_biases():
    r = jnp.arange(GROUP * WINDOW, dtype=I32)[:, None] % WINDOW
    c = jnp.arange(WINDOW, dtype=I32)[None, :]
    masked = jnp.full((GROUP * WINDOW, WINDOW), NEG_INF, F32)
    lo = jnp.stack([jnp.where(c >= r, 0.0, NEG_INF).astype(F32), masked])
    hi = jnp.stack([jnp.where(c <= r, 0.0, NEG_INF).astype(F32), masked])
    return lo, hi


def _dft_tables(na, nb):
    n = na * nb
    s1 = 2.0 ** (-(int(math.log2(na)) // 2))
    s2 = (1.0 / math.sqrt(n)) / s1
    ka = jnp.arange(na, dtype=I32)
    th_tw = ((jnp.arange(nb, dtype=I32)[:, None] * ka[None, :]) % n).astype(F32) * (2.0 * math.pi / n)
    th_f = ((ka[:, None] * ka[None, :]) % na).astype(F32) * (2.0 * math.pi / na)
    tw = jnp.stack([jnp.cos(th_tw) * s1, -jnp.sin(th_tw) * s1])
    tw = tw.reshape(2, nb // SUBLANES_F32, SUBLANES_F32, na).transpose(0, 1, 3, 2)
    f = jnp.stack([jnp.cos(th_f), -jnp.sin(th_f)])
    kb = jnp.arange(nb, dtype=I32)
    th2 = ((kb[:, None] * kb[None, :]) % nb).astype(F32) * (2.0 * math.pi / nb)
    t2 = (jnp.concatenate([jnp.cos(th2), jnp.sin(th2)], axis=-1) * s2).astype(BF16)
    return tw, f, t2


def _dft1_kernel(z_ref, tw_ref, f_ref, y_ref):
    planes, _, rows, _ = z_ref.shape
    step = SUBLANES_F32
    half, na = planes // 2, rows // step
    fr, fi = f_ref[0], f_ref[1]
    for j in range(step):
        twr, twi = tw_ref[0, 0][:, j:j + 1], tw_ref[1, 0][:, j:j + 1]
        tr, ti = twr * fr - twi * fi, twr * fi + twi * fr
        t_b = jnp.concatenate([jnp.concatenate([tr, ti], axis=1),
                               jnp.concatenate([ti, -tr], axis=1)], axis=0).astype(BF16)
        sel = pl.ds(j, na, stride=step)
        p = jnp.concatenate([z_ref[cc, 0, sel, :] for cc in range(half)], axis=1)
        q = jnp.concatenate([z_ref[cc, 0, sel, :] for cc in range(half, planes)], axis=1)
        xs = jnp.concatenate([p, q], axis=0).astype(BF16)
        y = jnp.dot(t_b, xs, preferred_element_type=F32)
        for cc in range(half):
            y_ref[cc, 0, sel, :] = y[:na, cc * LANES:(cc + 1) * LANES]
            y_ref[half + cc, 0, sel, :] = y[na:, cc * LANES:(cc + 1) * LANES]


def _dft2_kernel(y_ref, t_ref, o_ref):
    planes, groups, rows, _ = y_ref.shape
    step = SUBLANES_F32
    half, nb = planes // 2, groups * step

    def tokens_b(cc, j):
        return y_ref[cc, :, j * step:(j + 1) * step, :].reshape(nb, LANES)

    for j in range(rows // step):
        yr = jnp.concatenate([tokens_b(cc, j) for cc in range(half)], axis=1)
        yi = jnp.concatenate([tokens_b(cc, j) for cc in range(half, planes)], axis=1)
        xs = jnp.concatenate([yr, yi], axis=0).astype(BF16)
        out = jnp.dot(t_ref[...], xs, preferred_element_type=F32)
        for cc in range(half):
            o_ref[cc, 0, pl.ds(j, nb, stride=step), :] = out[:, cc * LANES:(cc + 1) * LANES]


def _attn_dft_kernel(*refs, n_total, tq, first_step, stage):
    attn_in, dft_in, (o_ref, d_out) = refs[:12], refs[12:-2], refs[-2:]
    _attn_body(*attn_in, o_ref, n_total=n_total, tq=tq, first_step=first_step)
    (_dft1_kernel if stage == 1 else _dft2_kernel)(*dft_in, d_out)


def _attention_and_dft(q, k, v, kc, vc, sink, z):
    n, m = q.shape[0], kc.shape[0]
    planes, groups, rows, _ = z.shape
    step = SUBLANES_F32
    nb, na = groups * step, rows // step
    tw, f, t2 = _dft_tables(na, nb)
    bias_lo, bias_hi = _band_biases()
    nblk = n // WINDOW
    half = n // 2

    def call(stage, steps, first_row, dft_in, dft_in_specs, d_out_shape, d_out_spec):
        tq = half // steps
        nsub = tq // WINDOW
        first_step = first_row // tq
        main = lambda i, s: (i + first_step, 0)
        prev = lambda i, s: (jnp.maximum((i + first_step) * nsub - 1, 0), 0)
        nxt = lambda i, s: (jnp.minimum((i + first_step) * nsub + nsub, nblk - 1), 0)
        const = lambda i, s: (0, 0)
        kv_specs = [pl.BlockSpec((WINDOW, KV_DIM), prev), pl.BlockSpec((tq, KV_DIM), main),
                    pl.BlockSpec((WINDOW, KV_DIM), nxt)]
        bias_spec = pl.BlockSpec(bias_lo.shape, lambda i, s: (0, 0, 0))
        return pl.pallas_call(
            functools.partial(_attn_dft_kernel, n_total=n, tq=tq, first_step=first_step,
                              stage=stage),
            out_shape=(jax.ShapeDtypeStruct((half, Q_DIM), BF16), d_out_shape),
            grid_spec=pltpu.PrefetchScalarGridSpec(
                num_scalar_prefetch=1,
                grid=(steps,),
                in_specs=[pl.BlockSpec((tq, Q_DIM), main)] + kv_specs + kv_specs
                         + [pl.BlockSpec((m, KV_DIM), const), pl.BlockSpec((m, KV_DIM), const),
                            bias_spec, bias_spec] + dft_in_specs,
                out_specs=(pl.BlockSpec((tq, Q_DIM), lambda i, s: (i, 0)), d_out_spec)),
            compiler_params=_cparams("arbitrary"),
            name=f"attention_half{stage}_dft_stage{stage}",
        )(sink, q, k, k, k, v, v, v, kc, vc, bias_lo, bias_hi, *dft_in)

    ax_first, y = call(
        1, groups, 0, [z, tw, f],
        [pl.BlockSpec((planes, 1, rows, LANES), lambda i, s: (0, i, 0, 0)),
         pl.BlockSpec((2, 1, na, step), lambda i, s: (0, i, 0, 0)),
         pl.BlockSpec((2, na, na), lambda i, s: (0, 0, 0))],
        jax.ShapeDtypeStruct(z.shape, F32),
        pl.BlockSpec((planes, 1, rows, LANES), lambda i, s: (0, i, 0, 0)))
    ax_second, fx = call(
        2, na // step, half, [y, t2],
        [pl.BlockSpec((planes, groups, step * step, LANES), lambda i, s: (0, 0, i, 0)),
         pl.BlockSpec((nb, 2 * nb), lambda i, s: (0, 0))],
        jax.ShapeDtypeStruct((planes // 2, na // step, nb * step, LANES), F32),
        pl.BlockSpec((planes // 2, 1, nb * step, LANES), lambda i, s: (0, i, 0, 0)))
    return ax_first, ax_second, fx


def _outproj_kernel(ax1_ref, ax2_ref, fx_ref, x_ref, wo_ref, gate_ref, nw_ref, sh_ref, sc_ref,
                    wr_ref, x1_ref, g_ref, aff_ref):
    planes, groups, rows, _ = fx_ref.shape
    in_first_half = pl.program_id(0) < pl.num_programs(0) // 2
    na = groups * SUBLANES_F32
    n_kb = rows // SUBLANES_F32
    kb_per_chunk = max(1, n_kb // OUTPROJ_CHUNKS)
    for first_kb in range(0, n_kb, kb_per_chunk):
        rs = slice(first_kb * na, (first_kb + kb_per_chunk) * na)
        fx = jnp.concatenate(
            [jnp.concatenate(
                [fx_ref[cc, :, kb * SUBLANES_F32:(kb + 1) * SUBLANES_F32, :].reshape(na, LANES)
                 for kb in range(first_kb, first_kb + kb_per_chunk)], axis=0)
             for cc in range(planes)], axis=1).astype(BF16)
        ax = jnp.where(in_first_half, ax1_ref[rs, :], ax2_ref[rs, :])
        acc = (jnp.dot(ax, wo_ref[:Q_DIM, :], preferred_element_type=F32)
               + jnp.dot(fx, wo_ref[Q_DIM:, :], preferred_element_type=F32))
        x1 = x_ref[rs, :] + gate_ref[...] * acc
        x1_ref[rs, :] = x1
        gb = _rms_mod(x1, nw_ref[...], sh_ref[...], sc_ref[...]).astype(BF16)
        g_ref[rs, :] = gb
        logits = jnp.dot(gb, wr_ref[...], preferred_element_type=F32)
        e = jnp.exp(logits - jnp.max(logits, axis=1, keepdims=True))
        aff_ref[rs, :] = e / jnp.sum(e, axis=1, keepdims=True)


def _output_projection(ax1, ax2, fx, x, wo_b, gate, nw, shift, scale, wr_b, tm=512):
    n, d = x.shape
    ne = wr_b.shape[1]
    half_steps = n // tm // 2
    row = lambda i: (i, 0)
    const = lambda i: (0, 0)
    vec = pl.BlockSpec((1, d), const)
    return pl.pallas_call(
        _outproj_kernel,
        out_shape=(jax.ShapeDtypeStruct((n, d), F32), jax.ShapeDtypeStruct((n, d), BF16),
                   jax.ShapeDtypeStruct((n, ne), F32)),
        grid=(n // tm,),
        in_specs=[pl.BlockSpec((tm, Q_DIM), lambda i: (jnp.minimum(i, half_steps - 1), 0)),
                  pl.BlockSpec((tm, Q_DIM), lambda i: (jnp.maximum(i - half_steps, 0), 0)),
                  pl.BlockSpec((fx.shape[0], fx.shape[1], tm // (fx.shape[1] * SUBLANES_F32)
                                * SUBLANES_F32, LANES), lambda i: (0, 0, i, 0)),
                  pl.BlockSpec((tm, d), row),
                  pl.BlockSpec((Q_DIM + F_DIM, d), const, pipeline_mode=pl.Buffered(1)),
                  vec, vec, vec, vec, pl.BlockSpec((d, ne), const)],
        out_specs=(pl.BlockSpec((tm, d), row), pl.BlockSpec((tm, d), row),
                   pl.BlockSpec((tm, ne), row)),
        compiler_params=_cparams("arbitrary"),
        name="output_projection",
    )(ax1, ax2, fx, x, wo_b, gate, nw, shift, scale, wr_b)


def _route_kernel(aff_ref, posm_ref, cnt_ref, off_ref, *, cap, n_exp, n_chunk):
    aff_all = aff_ref[...]
    blocks = [aff_all[e * n_chunk:(e + 1) * n_chunk] for e in range(n_exp)]

    def total(mask):
        s = jnp.sum(jnp.where(mask, 1.0, 0.0), axis=0, keepdims=True)
        return jnp.sum(s, axis=1, keepdims=True)

    def search(it, thr_bits):
        bit = jnp.left_shift(jnp.int32(1), 30 - it)
        out = []
        for e in range(n_exp):
            cand = thr_bits[e] | bit
            enough = total(blocks[e] >= lax.bitcast_convert_type(cand, F32)) >= cap
            out.append(jnp.where(enough, cand, thr_bits[e]))
        return tuple(out)

    thr_bits = lax.fori_loop(0, 31, search, tuple(jnp.zeros((1, 1), I32) for _ in range(n_exp)))
    thr = [lax.bitcast_convert_type(t, F32) for t in thr_bits]

    rl = lax.broadcasted_iota(I32, (LANES, LANES), 0)
    cl = lax.broadcasted_iota(I32, (LANES, LANES), 1)
    before = jnp.where(rl < cl, 1.0, 0.0).astype(BF16)
    ones = jnp.ones((LANES, LANES), BF16)
    rc = lax.broadcasted_iota(I32, (n_chunk, n_chunk), 0)
    cc = lax.broadcasted_iota(I32, (n_chunk, n_chunk), 1)
    earlier = jnp.where(cc < rc, 1.0, 0.0).astype(BF16)

    def excl_cumsum(mask):
        mb = jnp.where(mask, 1.0, 0.0).astype(BF16)
        within = jnp.dot(mb, before, preferred_element_type=F32)
        rowtot = jnp.dot(mb, ones, preferred_element_type=F32)
        choff = jnp.dot(earlier, rowtot.astype(BF16), preferred_element_type=F32)
        return within + choff, rowtot, choff

    for e in range(n_exp):
        gt = blocks[e] > thr[e]
        eq = blocks[e] == thr[e]
        need = cap - total(gt)
        tie_rank, _, _ = excl_cumsum(eq)
        sel = gt | (eq & (tie_rank < need))
        pos, rowtot, choff = excl_cumsum(sel)
        rs = slice(e * n_chunk, (e + 1) * n_chunk)
        posm_ref[rs, :] = jnp.where(sel, pos.astype(I32), NOT_SELECTED)
        cnt_ref[rs, :] = rowtot.astype(I32)
        off_ref[rs, :] = choff.astype(I32)


def _route(aff2, cap, n_exp):
    rows = aff2.shape[0]
    n_chunk = rows // n_exp
    spec = pl.BlockSpec((rows, LANES), lambda i: (0, 0))
    shp = jax.ShapeDtypeStruct((rows, LANES), I32)
    return pl.pallas_call(
        functools.partial(_route_kernel, cap=cap, n_exp=n_exp, n_chunk=n_chunk),
        out_shape=(shp, shp, shp),
        grid=(1,),
        in_specs=[spec],
        out_specs=(spec, spec, spec),
        compiler_params=_cparams("arbitrary"),
        name="expert_choice_routing",
    )(aff2)


def _window_matches(posm, firsts, win_starts):
    r = lax.broadcasted_iota(I32, (WINDOW_ROWS, TOK_TILE), 0)
    out = []
    for e, (first, start) in enumerate(zip(firsts, win_starts)):
        pm = posm[e:e + 1, :]
        pm = jnp.where((pm >= first) & (pm < first + PIECE), pm, NOT_SELECTED)
        out.append((pm - start) == r)
    return out


def _align_down(v):
    return (v // ROW_ALIGN) * ROW_ALIGN


def _for_window_size(rows_needed, fn):
    below = None
    for k, size in enumerate(WINDOW_SIZES):
        fits = rows_needed <= size if k + 1 < len(WINDOW_SIZES) else None
        cond = fits if below is None else (below if fits is None else jnp.logical_and(below, fits))
        pl.when(cond)(functools.partial(fn, size))
        below = rows_needed > size


def _gather_kernel(off_ref, cnt_ref, g_ref, posm_ref, aff_ref, xs_hbm,
                   stg, tail, sem, npass_ref, sent_ref, *, n_exp, n_chunk, cap):
    c = pl.program_id(0)
    stride = cap + WINDOW_ROWS

    @pl.when(c == 0)
    def _():
        npass_ref[0] = 0
        tail[...] = jnp.zeros_like(tail)

    def window_copy(slot, e, dst, size=WINDOW_ROWS):
        return pltpu.make_async_copy(stg.at[slot, pl.ds(e * WINDOW_ROWS, size)],
                                     xs_hbm.at[pl.ds(dst, size)], sem.at[slot, e])

    def start_window(slot, e, dst, rows_used):
        _for_window_size(rows_used, lambda size: window_copy(slot, e, dst, size).start())
        sent_ref[slot * n_exp + e] = rows_used

    def wait_window(slot, e):
        _for_window_size(sent_ref[slot * n_exp + e],
                         lambda size: window_copy(slot, e, 0, size).wait())

    @pl.when(c == 0)
    def _():
        stg[0] = jnp.zeros(stg.shape[1:], stg.dtype)
        for e in range(n_exp):
            window_copy(0, e, e * stride + cap).start()
        for e in range(n_exp):
            window_copy(0, e, 0).wait()

    most = cnt_ref[c]
    for e in range(1, n_exp):
        most = jnp.maximum(most, cnt_ref[e * n_chunk + c])
    passes = (most + PIECE - 1) // PIECE

    def one_pass(p, carry):
        done = npass_ref[0]
        slot = done % 2
        posm = posm_ref[0]
        aff = aff_ref[0]
        firsts, win_starts, next_shift, rows_used = [], [], [], []
        for e in range(n_exp):
            o, n_e = off_ref[e * n_chunk + c], cnt_ref[e * n_chunk + c]
            first = o + jnp.minimum(p * PIECE, n_e)
            after = o + jnp.minimum((p + 1) * PIECE, n_e)
            firsts.append(first)
            win_starts.append(_align_down(first))
            next_shift.append(_align_down(after) - _align_down(first))
            rows_used.append(after - _align_down(first))
        matches = _window_matches(posm, firsts, win_starts)
        onehot = jnp.concatenate([jnp.where(mt, 1.0, 0.0).astype(BF16) for mt in matches], axis=0)
        new = jnp.dot(onehot, g_ref[...], preferred_element_type=F32)
        lane = lax.broadcasted_iota(I32, (WINDOW_ROWS, GATE_LANES), 1)
        for e in range(n_exp):
            lo = e * WINDOW_ROWS
            gate = jnp.sum(jnp.where(matches[e], aff[e:e + 1, :], 0.0), axis=1, keepdims=True)
            hi = gate.astype(BF16).astype(F32)
            mid = (gate - hi).astype(BF16).astype(F32)
            low = gate - hi - mid
            parts = jnp.where(lane == 0, hi, jnp.where(lane == 1, mid,
                                                       jnp.where(lane == 2, low, 0.0)))
            rows = jnp.concatenate([new[lo:lo + WINDOW_ROWS], parts], axis=1)
            stg[slot, lo:lo + ROW_ALIGN] = (rows[:ROW_ALIGN] + tail[e].astype(F32)).astype(BF16)
            stg[slot, lo + ROW_ALIGN:lo + WINDOW_ROWS] = rows[ROW_ALIGN:].astype(BF16)

        dsts = [pl.multiple_of(e * stride + win_starts[e], ROW_ALIGN) for e in range(n_exp)]

        @pl.when(done > 0)
        def _():
            for e in range(n_exp):
                wait_window(1 - slot, e)
                start_window(slot, e, dsts[e], rows_used[e])

        @pl.when(done == 0)
        def _():
            for e in range(n_exp):
                start_window(slot, e, dsts[e], rows_used[e])

        for e in range(n_exp):
            src = pl.ds(pl.multiple_of(e * WINDOW_ROWS + next_shift[e], ROW_ALIGN), ROW_ALIGN)
            tail[e] = stg[slot, src, :]
        npass_ref[0] = done + 1
        return carry

    lax.fori_loop(0, passes, one_pass, 0)

    @pl.when(c == pl.num_programs(0) - 1)
    def _():
        done = npass_ref[0]

        @pl.when(done > 0)
        def _():
            for e in range(n_exp):
                wait_window((done - 1) % 2, e)


def _gather(off, cnt, g, posm_t, aff_t, cap):
    n, d = g.shape
    n_chunk, n_exp, _ = posm_t.shape
    rows = n_exp * (cap + WINDOW_ROWS)
    tile3 = lambda i, o, c: (i, 0, 0)
    return pl.pallas_call(
        functools.partial(_gather_kernel, n_exp=n_exp, n_chunk=n_chunk, cap=cap),
        out_shape=jax.ShapeDtypeStruct((rows, d + GATE_LANES), BF16),
        grid_spec=pltpu.PrefetchScalarGridSpec(
            num_scalar_prefetch=2,
            grid=(n_chunk,),
            in_specs=[pl.BlockSpec((TOK_TILE, d), lambda i, o, c: (i, 0)),
                      pl.BlockSpec((1, n_exp, TOK_TILE), tile3),
                      pl.BlockSpec((1, n_exp, TOK_TILE), tile3)],
            out_specs=pl.BlockSpec(memory_space=pl.ANY),
            scratch_shapes=[pltpu.VMEM((2, n_exp * WINDOW_ROWS, d + GATE_LANES), BF16),
                            pltpu.VMEM((n_exp, ROW_ALIGN, d + GATE_LANES), BF16),
                            pltpu.SemaphoreType.DMA((2, n_exp)),
                            pltpu.SMEM((1,), I32),
                            pltpu.SMEM((2 * n_exp,), I32)]),
        compiler_params=_cparams("arbitrary"),
        name="gather_expert_tokens",
    )(off, cnt, g, posm_t, aff_t)


def _expert_kernel(xs_ref, wg_ref, wu_ref, wd_ref, y_ref, h_ref, gate_ref, *, nf, tf, tn):
    j = pl.program_id(1)
    d = wg_ref.shape[1]
    cap = xs_ref.shape[1]
    halves = [slice(r * (cap // EXPERT_ROW_SPLIT), (r + 1) * (cap // EXPERT_ROW_SPLIT))
              for r in range(EXPERT_ROW_SPLIT)]

    @pl.when(j == 0)
    def _():
        parts = xs_ref[0, :, d:].astype(F32)
        gate = parts[:, 0:1] + parts[:, 1:2] + parts[:, 2:3]
        gate_ref[...] = jnp.broadcast_to(gate, gate_ref.shape)

    @pl.when(j < nf)
    def _():
        wg = wg_ref[0].astype(BF16)
        wu = wu_ref[0].astype(BF16)
        for rs in halves:
            xs = xs_ref[0, rs, :d]
            gp = jnp.dot(xs, wg, preferred_element_type=F32)
            up = jnp.dot(xs, wu, preferred_element_type=F32)
            h = ((gp * jax.nn.sigmoid(gp)) * up).astype(BF16)
            for jj in range(nf):
                @pl.when(j == jj)
                def _():
                    h_ref[rs, jj * tf:(jj + 1) * tf] = h

    @pl.when(j >= nf)
    def _():
        wd = wd_ref[0].astype(BF16)
        for rs in halves:
            y = jnp.dot(h_ref[rs, :], wd, preferred_element_type=F32)
            y_ref[0, rs, :] = (y * gate_ref[rs, 0:1]).astype(BF16)


def _experts(xs3, w_gate, w_up, w_down, cap, tf=512, tn=512):
    n_exp, d, ff = w_gate.shape
    tf, tn = min(tf, ff), min(tn, d)
    nf, nn = ff // tf, d // tn

    def ahead(e, j):
        return jnp.minimum(e + (j >= nf).astype(I32), n_exp - 1)

    def ff_tile(e, j):
        return jnp.where(j >= nf, jnp.where(e == n_exp - 1, nf - 1, 0), j)

    return pl.pallas_call(
        functools.partial(_expert_kernel, nf=nf, tf=tf, tn=tn),
        out_shape=jax.ShapeDtypeStruct((n_exp, cap, d), BF16),
        grid=(n_exp, nf + nn),
        in_specs=[pl.BlockSpec((1, cap, d + GATE_LANES), lambda e, j: (ahead(e, j), 0, 0)),
                  pl.BlockSpec((1, d, tf), lambda e, j: (ahead(e, j), 0, ff_tile(e, j))),
                  pl.BlockSpec((1, d, tf), lambda e, j: (ahead(e, j), 0, ff_tile(e, j))),
                  pl.BlockSpec((1, ff, tn), lambda e, j: (e, 0, jnp.maximum(j - nf, 0)))],
        out_specs=pl.BlockSpec((1, cap, tn), lambda e, j: (e, 0, jnp.maximum(j - nf, 0))),
        scratch_shapes=[pltpu.VMEM((cap, ff), BF16), pltpu.VMEM((cap, LANES), F32)],
        compiler_params=_cparams("arbitrary", "arbitrary"),
        name="swiglu_experts",
    )(xs3, w_gate, w_up, w_down)


def _combine_kernel(off_ref, cnt_ref, y_hbm, posm_ref, x1_ref, gate_ref, nw_ref, o_ref,
                    stg, acc_ref, sem, *, n_exp, n_chunk, cap):
    c = pl.program_id(0)
    last_start = n_exp * cap - WINDOW_ROWS

    def firsts_of(tile, p):
        return [off_ref[e * n_chunk + tile] + p * PIECE for e in range(n_exp)]

    def starts_of(tile, p):
        return [pl.multiple_of(jnp.minimum(e * cap + _align_down(first), last_start), ROW_ALIGN)
                for e, first in enumerate(firsts_of(tile, p))]

    def rows_used_of(tile, p):
        out = []
        for e, (first, start) in enumerate(zip(firsts_of(tile, p), starts_of(tile, p))):
            left = jnp.clip(cnt_ref[e * n_chunk + tile] - p * PIECE, 0, PIECE)
            out.append(e * cap + first + left - start)
        return out

    def piece_copy(slot, e, src, size=WINDOW_ROWS):
        return pltpu.make_async_copy(y_hbm.at[pl.ds(src, size)],
                                     stg.at[slot, pl.ds(e * WINDOW_ROWS, size)], sem.at[slot])

    def fetch(tile, p, slot):
        for e, (src, used) in enumerate(zip(starts_of(tile, p), rows_used_of(tile, p))):
            _for_window_size(used, lambda size: piece_copy(slot, e, src, size).start())

    def wait_fetch(tile, p, slot):
        for e, used in enumerate(rows_used_of(tile, p)):
            _for_window_size(used, lambda size: piece_copy(slot, e, 0, size).wait())

    def contribution(p, slot, tokens):
        starts = [s - e * cap for e, s in enumerate(starts_of(c, p))]
        matches = _window_matches(posm_ref[0], firsts_of(c, p), starts)
        onehot = jnp.concatenate([jnp.where(mt[:, tokens], 1.0, 0.0).astype(BF16)
                                  for mt in matches], axis=0)
        return lax.dot_general(onehot, stg[slot], (((0,), (0,)), ((), ())),
                               preferred_element_type=F32)

    def finish(tokens, moe):
        x2 = x1_ref[tokens, :] + gate_ref[...] * moe
        ms = jnp.mean(x2 * x2, axis=-1, keepdims=True)
        o_ref[tokens, :] = x2 * lax.rsqrt(ms + RMS_EPS) * nw_ref[...]

    slot = c % 2

    @pl.when(c == 0)
    def _():
        stg[...] = jnp.zeros_like(stg)
        fetch(0, 0, 0)

    @pl.when(c + 1 < pl.num_programs(0))
    def _():
        fetch(c + 1, 0, 1 - slot)

    most = cnt_ref[c]
    for e in range(1, n_exp):
        most = jnp.maximum(most, cnt_ref[e * n_chunk + c])
    passes = (most + PIECE - 1) // PIECE
    wait_fetch(c, 0, slot)

    @pl.when(passes <= 1)
    def _():
        for t0 in range(0, TOK_TILE, LANES):
            tokens = slice(t0, t0 + LANES)
            finish(tokens, contribution(0, slot, tokens))

    @pl.when(passes > 1)
    def _():
        everyone = slice(0, TOK_TILE)
        acc_ref[...] = contribution(0, slot, everyone)

        def extra_pass(p, carry):
            fetch(c, p, slot)
            wait_fetch(c, p, slot)
            acc_ref[...] += contribution(p, slot, everyone)
            return carry

        lax.fori_loop(1, passes, extra_pass, 0)
        finish(everyone, acc_ref[...])


def _combine(off, cnt, y, posm_t, x1, gate, nw, cap):
    n, d = x1.shape
    n_chunk, n_exp, _ = posm_t.shape
    return pl.pallas_call(
        functools.partial(_combine_kernel, n_exp=n_exp, n_chunk=n_chunk, cap=cap),
        out_shape=jax.ShapeDtypeStruct((n, d), F32),
        grid_spec=pltpu.PrefetchScalarGridSpec(
            num_scalar_prefetch=2,
            grid=(n_chunk,),
            in_specs=[pl.BlockSpec(memory_space=pl.ANY),
                      pl.BlockSpec((1, n_exp, TOK_TILE), lambda i, o, c: (i, 0, 0)),
                      pl.BlockSpec((TOK_TILE, d), lambda i, o, c: (i, 0)),
                      pl.BlockSpec((1, d), lambda i, o, c: (0, 0)),
                      pl.BlockSpec((1, d), lambda i, o, c: (0, 0))],
            out_specs=pl.BlockSpec((TOK_TILE, d), lambda i, o, c: (i, 0)),
            scratch_shapes=[pltpu.VMEM((2, n_exp * WINDOW_ROWS, d), BF16),
                            pltpu.VMEM((TOK_TILE, d), F32),
                            pltpu.SemaphoreType.DMA((2,))]),
        compiler_params=_cparams("arbitrary"),
        name="combine_experts_final_norm",
    )(off, cnt, y, posm_t, x1, gate, nw)


def kernel(x, c, ctx, c_ctx, w_mod, b_mod, norm_mix, w_in, sink, w_fourier, w_out, norm_ffn,
           w_router, w_gate, w_up, w_down, norm_final):
    assert x.shape[0] == 1 and w_mod.shape[0] == 1
    n, d = x.shape[1], x.shape[2]
    xl, cx = x[0], ctx[0]
    n_exp = w_router.shape[2]
    cap = max(1, CAP_FACTOR * n // n_exp)
    n_chunk = n // LANES
    n_tile = n // TOK_TILE
    per_tile = TOK_TILE // LANES

    mod = _modulation(jnp.stack([c[0], c_ctx], axis=1), w_mod[0], b_mod[0][None, :])
    mx = mod[0].reshape(6, 1, d)
    mc = mod[1].reshape(6, 1, d)

    w_in_b = w_in[0].astype(BF16)
    nmix = norm_mix[0][None, :]
    ab = _fold_channel_dft(w_fourier[0])
    q, k, v, z, w_out_b = _project_latent(xl, nmix, mx[0], mx[1], w_in_b, ab, w_out[0])
    kc, vc = _project_context(cx, nmix, mc[0], mc[1], w_in_b[:, Q_DIM:Q_DIM + 2 * KV_DIM])
    ax1, ax2, fx = _attention_and_dft(q, k, v, kc, vc, sink[0], z)

    x1, g, aff = _output_projection(ax1, ax2, fx, xl, w_out_b, mx[2], norm_ffn[0][None, :],
                                    mx[3], mx[4], w_router[0].astype(BF16))

    aff2 = aff.T.reshape(n_exp * n_chunk, LANES)
    posm, cnt, off = _route(aff2, cap, n_exp)
    by_tile = lambda a: a.reshape(n_exp, n_tile, TOK_TILE).transpose(1, 0, 2)
    posm_t, aff_t = by_tile(posm), by_tile(aff2)
    cnt1 = cnt[:, 0].reshape(n_exp * n_tile, per_tile).sum(axis=1)
    off1 = off[:, 0].reshape(n_exp * n_tile, per_tile)[:, 0]

    xs = _gather(off1, cnt1, g, posm_t, aff_t, cap)
    y = _experts(xs.reshape(n_exp, cap + WINDOW_ROWS, d + GATE_LANES),
                 w_gate[0], w_up[0], w_down[0], cap)
    out = _combine(off1, cnt1, y.reshape(n_exp * cap, d), posm_t, x1, mx[5], norm_final[None, :],
                   cap)
    return out[None]
```

```python
import functools
import math

import jax
import jax.numpy as jnp
from jax import lax
from jax.experimental import pallas as pl
from jax.experimental.pallas import tpu as pltpu

F32 = jnp.float32
BF16 = jnp.bfloat16
I32 = jnp.int32

HEAD_DIM = 128
N_HEADS = 8
N_KV_HEADS = 2
GROUP = N_HEADS // N_KV_HEADS
WINDOW = 128
Q_DIM = N_HEADS * HEAD_DIM
KV_DIM = N_KV_HEADS * HEAD_DIM
N_FGROUPS = 4
FG = 256
F_DIM = N_FGROUPS * FG
D_IN = Q_DIM + 2 * KV_DIM + F_DIM
N_EXPERTS = 16
CAP_FACTOR = 2
GRID_W = 64
ROPE_THETA = 10000.0
RMS_EPS = 1e-6
NEG_INF = -1e30
LOG2_E = math.log2(math.e)

LANES = 128
SUBLANES_F32 = 8
VMEM_LIMIT_BYTES = 56 * 1024 * 1024

TOK_TILE = 2 * LANES
PIECE = 48
ROW_ALIGN = 16
WINDOW_ROWS = ROW_ALIGN + PIECE
WINDOW_SIZES = tuple(range(2 * ROW_ALIGN, WINDOW_ROWS + 1, ROW_ALIGN))
GATE_LANES = LANES
DFT_NB = LANES
OUTPROJ_CHUNKS = 2
PROJ_CHUNKS = 2
EXPERT_ROW_SPLIT = 2
NOT_SELECTED = -(1 << 20)


def _cparams(*sem):
    return pltpu.CompilerParams(dimension_semantics=sem, vmem_limit_bytes=VMEM_LIMIT_BYTES)


def _rms_mod(x, nw, shift, scale):
    ms = jnp.mean(x * x, axis=-1, keepdims=True)
    return x * lax.rsqrt(ms + RMS_EPS) * (nw * (1.0 + scale)) + shift


def _mod_kernel(ct_ref, w_ref, b_ref, o_ref):
    ct = ct_ref[...]
    s = ct * jax.nn.sigmoid(ct)
    w = w_ref[...]
    r0 = jnp.sum(s[:, 0:1] * w, axis=0, keepdims=True)
    r1 = jnp.sum(s[:, 1:2] * w, axis=0, keepdims=True)
    o_ref[...] = jnp.concatenate([r0, r1], axis=0) + b_ref[...]


def _modulation(ct, w_mod, b_mod, n_cols):
    d, n = w_mod.shape[0], n_cols
    tn = math.gcd(n, 1024)
    return pl.pallas_call(
        _mod_kernel,
        out_shape=jax.ShapeDtypeStruct((2, n), F32),
        grid=(n // tn,),
        in_specs=[pl.BlockSpec((d, 2), lambda i: (0, 0)),
                  pl.BlockSpec((d, tn), lambda i: (0, i)),
                  pl.BlockSpec((1, tn), lambda i: (0, i))],
        out_specs=pl.BlockSpec((2, tn), lambda i: (0, i)),
        compiler_params=_cparams("arbitrary"),
        name="modulation",
    )(ct, w_mod, b_mod)


def _ab_kernel(cf_ref, sf_ref, wf_ref, ab_ref):
    wf = wf_ref[0]
    a = jnp.dot(cf_ref[...], wf, preferred_element_type=F32, precision=lax.Precision.HIGHEST)
    b = jnp.dot(sf_ref[...], wf, preferred_element_type=F32, precision=lax.Precision.HIGHEST)
    ab_ref[0] = jnp.concatenate([a, b], axis=1).astype(BF16)


def _fold_channel_dft(w_fourier):
    g, fg, _ = w_fourier.shape
    idx = jnp.arange(fg, dtype=I32)
    th = ((idx[:, None] * idx[None, :]) % fg).astype(F32) * (2.0 * math.pi / fg)
    scale = 1.0 / math.sqrt(fg)
    cf = jnp.cos(th) * scale
    sf = jnp.sin(th) * scale
    return pl.pallas_call(
        _ab_kernel,
        out_shape=jax.ShapeDtypeStruct((g, fg, 2 * fg), BF16),
        grid=(g,),
        in_specs=[pl.BlockSpec((fg, fg), lambda i: (0, 0)),
                  pl.BlockSpec((fg, fg), lambda i: (0, 0)),
                  pl.BlockSpec((1, fg, fg), lambda i: (i, 0, 0))],
        out_specs=pl.BlockSpec((1, fg, 2 * fg), lambda i: (i, 0, 0)),
        compiler_params=_cparams("arbitrary"),
        name="fold_channel_dft",
    )(cf, sf, w_fourier)


def _rope_tables(n):
    quarter = HEAD_DIM // 4
    freqs = ROPE_THETA ** (-jnp.arange(quarter, dtype=F32) / quarter)
    zeros = lambda m: jnp.zeros((m, 2 * quarter), F32)

    def tables(pos, low_half):
        ang = pos[:, None] * freqs[None, :]
        ang = jnp.concatenate([ang, ang], axis=-1)
        cos, sin = jnp.cos(ang), jnp.sin(ang)
        first = (jnp.arange(2 * quarter) < quarter)[None, :]
        parts = (cos, jnp.where(first, -sin, 0.0), jnp.where(first, 0.0, sin))
        pad = zeros(pos.shape[0])
        return jnp.stack([jnp.concatenate([p, pad] if low_half else [pad, p], axis=-1)
                          for p in parts])

    row_tab = tables(jnp.arange(n // GRID_W, dtype=F32), True)
    col_tab = tables(jnp.arange(GRID_W, dtype=F32), False)
    return row_tab, col_tab


def _store_dft_rows(z_ref, plane, val, a0):
    groups = z_ref.shape[1]
    nb = groups * SUBLANES_F32
    for al in range(val.shape[0] // nb):
        z_ref[plane, :, (a0 + al) * SUBLANES_F32:(a0 + al + 1) * SUBLANES_F32, :] = (
            val[al * nb:(al + 1) * nb].reshape(groups, SUBLANES_F32, LANES))


def _proj_kernel(x_ref, nw_ref, sh_ref, sc_ref, w_ref, ab_ref, rt_ref, ct_ref, wo_ref,
                 cvec_ref, wm_ref, bm_ref, q_ref, k_ref, v_ref, z_ref, wo_b_ref, mod_ref):
    wo_b_ref[...] = wo_ref[...].astype(BF16)
    _mod_kernel(cvec_ref, wm_ref, bm_ref, mod_ref)
    quarter = HEAD_DIM // 4
    qscale = LOG2_E / math.sqrt(HEAD_DIM)
    tm = x_ref.shape[0]
    chunk = tm // PROJ_CHUNKS
    for r0 in range(0, tm, chunk):
        rs = slice(r0, r0 + chunk)
        hb = _rms_mod(x_ref[rs, :], nw_ref[...], sh_ref[...], sc_ref[...]).astype(BF16)
        cos, s_up, s_dn = (
            jnp.concatenate(
                [rt_ref[t, r:r + 1, :] + ct_ref[t]
                 for r in range(r0 // GRID_W, (r0 + chunk) // GRID_W)], axis=0)
            for t in range(3))

        def rope(t):
            return (t * cos + pltpu.roll(t, HEAD_DIM - quarter, 1) * s_up
                    + pltpu.roll(t, quarter, 1) * s_dn)

        q = jnp.dot(hb, w_ref[:, :Q_DIM], preferred_element_type=F32)
        for j in range(N_HEADS):
            sl = slice(j * HEAD_DIM, (j + 1) * HEAD_DIM)
            q_ref[rs, sl] = (rope(q[:, sl]) * qscale).astype(BF16)
        k = jnp.dot(hb, w_ref[:, Q_DIM:Q_DIM + KV_DIM], preferred_element_type=F32)
        for j in range(N_KV_HEADS):
            sl = slice(j * HEAD_DIM, (j + 1) * HEAD_DIM)
            k_ref[rs, sl] = rope(k[:, sl]).astype(BF16)
        v = jnp.dot(hb, w_ref[:, Q_DIM + KV_DIM:Q_DIM + 2 * KV_DIM], preferred_element_type=F32)
        v_ref[rs, :] = v.astype(BF16)
        u = jnp.dot(hb, w_ref[:, Q_DIM + 2 * KV_DIM:], preferred_element_type=F32).astype(BF16)
        for g in range(N_FGROUPS):
            pq = jnp.dot(u[:, g * FG:(g + 1) * FG], ab_ref[g], preferred_element_type=F32)
            per_group = FG // LANES
            for t in range(2 * per_group):
                plane = (t // per_group) * (F_DIM // LANES) + g * per_group + t % per_group
                _store_dft_rows(z_ref, plane, pq[:, t * LANES:(t + 1) * LANES], r0 // DFT_NB)


def _project_latent(x, nw, shift, scale, w_in_b, ab, w_out, cvec, w_mod, b_mod, mod_first, tm=512):
    n, d = x.shape
    steps = n // tm
    wo_rows = w_out.shape[0] // steps
    mod_cols = (w_mod.shape[1] - mod_first) // steps
    mod_blk0 = mod_first // mod_cols
    row_tab, col_tab = _rope_tables(n)
    row = lambda i: (i, 0)
    const2 = lambda i: (0, 0)
    return pl.pallas_call(
        _proj_kernel,
        out_shape=(jax.ShapeDtypeStruct((n, Q_DIM), BF16),
                   jax.ShapeDtypeStruct((n, KV_DIM), BF16),
                   jax.ShapeDtypeStruct((n, KV_DIM), BF16),
                   jax.ShapeDtypeStruct((2 * F_DIM // LANES, DFT_NB // SUBLANES_F32,
                                         (n // DFT_NB) * SUBLANES_F32, LANES), F32),
                   jax.ShapeDtypeStruct(w_out.shape, BF16),
                   jax.ShapeDtypeStruct((2, w_mod.shape[1] - mod_first), F32)),
        grid=(n // tm,),
        in_specs=[pl.BlockSpec((tm, d), row),
                  pl.BlockSpec((1, d), const2), pl.BlockSpec((1, d), const2),
                  pl.BlockSpec((1, d), const2),
                  pl.BlockSpec((d, D_IN), const2, pipeline_mode=pl.Buffered(1)),
                  pl.BlockSpec((N_FGROUPS, FG, 2 * FG), lambda i: (0, 0, 0),
                               pipeline_mode=pl.Buffered(1)),
                  pl.BlockSpec((3, tm // GRID_W, HEAD_DIM), lambda i: (0, i, 0)),
                  pl.BlockSpec((3, GRID_W, HEAD_DIM), lambda i: (0, 0, 0)),
                  pl.BlockSpec((wo_rows, w_out.shape[1]), row),
                  pl.BlockSpec(cvec.shape, const2),
                  pl.BlockSpec((w_mod.shape[0], mod_cols), lambda i: (0, mod_blk0 + i)),
                  pl.BlockSpec((1, mod_cols), lambda i: (0, mod_blk0 + i))],
        out_specs=(pl.BlockSpec((tm, Q_DIM), row), pl.BlockSpec((tm, KV_DIM), row),
                   pl.BlockSpec((tm, KV_DIM), row),
                   pl.BlockSpec((2 * F_DIM // LANES, DFT_NB // SUBLANES_F32,
                                 (tm // DFT_NB) * SUBLANES_F32, LANES), lambda i: (0, 0, i, 0)),
                   pl.BlockSpec((wo_rows, w_out.shape[1]), row),
                   pl.BlockSpec((2, mod_cols), lambda i: (0, i))),
        compiler_params=_cparams("arbitrary"),
        name="project_latent",
    )(x, nw, shift, scale, w_in_b, ab, row_tab, col_tab, w_out, cvec, w_mod, b_mod)


def _ctx_kernel(x_ref, nw_ref, sh_ref, sc_ref, w_ref, k_ref, v_ref):
    hb = _rms_mod(x_ref[...], nw_ref[...], sh_ref[...], sc_ref[...]).astype(BF16)
    kv = jnp.dot(hb, w_ref[...], preferred_element_type=F32)
    k_ref[...] = kv[:, :KV_DIM].astype(BF16)
    v_ref[...] = kv[:, KV_DIM:].astype(BF16)


def _project_context(ctx, nw, shift, scale, w_kv_b):
    m, d = ctx.shape
    full = lambda shp: pl.BlockSpec(shp, lambda i: (0, 0))
    return pl.pallas_call(
        _ctx_kernel,
        out_shape=(jax.ShapeDtypeStruct((m, KV_DIM), BF16),
                   jax.ShapeDtypeStruct((m, KV_DIM), BF16)),
        grid=(1,),
        in_specs=[full((m, d)), full((1, d)), full((1, d)), full((1, d)), full((d, 2 * KV_DIM))],
        out_specs=(full((m, KV_DIM)), full((m, KV_DIM))),
        compiler_params=_cparams("arbitrary"),
        name="project_context",
    )(ctx, nw, shift, scale, w_kv_b)


def _attn_body(sink_ref, q_ref, kp_ref, km_ref, kn_ref, vp_ref, vm_ref, vn_ref,
               kc_ref, vc_ref, lo_ref, hi_ref, o_ref, *, n_total, tq, first_step):
    i = pl.program_id(0) + first_step
    nsub = tq // WINDOW
    last_blk = n_total // WINDOW - 1
    kwin = jnp.concatenate([kp_ref[...], km_ref[...], kn_ref[...]], axis=0)
    vwin = jnp.concatenate([vp_ref[...], vm_ref[...], vn_ref[...]], axis=0)
    rows = GROUP * WINDOW
    span = 3 * WINDOW
    head_of_row = lax.broadcasted_iota(I32, (rows, 1), 0) // WINDOW
    ones_loc = jnp.ones((span, HEAD_DIM), BF16)
    ones_ctx = jnp.ones((kc_ref.shape[0], HEAD_DIM), BF16)
    nt = (((1,), (1,)), ((), ()))
    for b in range(nsub):
        blk = i * nsub + b
        bias_lo = lo_ref[(blk == 0).astype(I32)]
        bias_hi = hi_ref[(blk == last_blk).astype(I32)]
        for h in range(N_KV_HEADS):
            hs = slice(h * HEAD_DIM, (h + 1) * HEAD_DIM)
            qs = jnp.concatenate(
                [q_ref[b * WINDOW:(b + 1) * WINDOW,
                       (h * GROUP + g) * HEAD_DIM:(h * GROUP + g + 1) * HEAD_DIM]
                 for g in range(GROUP)], axis=0)
            kw = kwin[b * WINDOW:b * WINDOW + span, hs]
            vw = vwin[b * WINDOW:b * WINDOW + span, hs]
            s_loc = lax.dot_general(qs, kw, nt, preferred_element_type=F32)
            s_ctx = lax.dot_general(qs, kc_ref[:, hs], nt, preferred_element_type=F32)
            parts = [s_loc[:, :WINDOW] + bias_lo, s_loc[:, WINDOW:2 * WINDOW],
                     s_loc[:, 2 * WINDOW:] + bias_hi, s_ctx]
            sink_col = jnp.zeros((rows, 1), F32)
            for g in range(GROUP):
                sink_col = jnp.where(head_of_row == g, sink_ref[h * GROUP + g] * LOG2_E, sink_col)
            blocks = parts[:3] + [s_ctx[:, t * WINDOW:(t + 1) * WINDOW]
                                  for t in range(s_ctx.shape[1] // WINDOW)]
            widest = blocks[0]
            for blk_scores in blocks[1:]:
                widest = jnp.maximum(widest, blk_scores)
            m = jnp.maximum(sink_col, jnp.max(widest, axis=1, keepdims=True))
            p = [jnp.exp2(part - m).astype(BF16) for part in parts]
            ov = (jnp.dot(jnp.concatenate(p[:3], axis=1), jnp.concatenate([vw, ones_loc], axis=1),
                          preferred_element_type=F32)
                  + jnp.dot(p[3], jnp.concatenate([vc_ref[:, hs], ones_ctx], axis=1),
                            preferred_element_type=F32))
            o = ov[:, :HEAD_DIM] / (ov[:, HEAD_DIM:] + jnp.exp2(sink_col - m))
            for g in range(GROUP):
                o_ref[b * WINDOW:(b + 1) * WINDOW,
                      (h * GROUP + g) * HEAD_DIM:(h * GROUP + g + 1) * HEAD_DIM] = (
                    o[g * WINDOW:(g + 1) * WINDOW].astype(BF16))


def _band_biases():
    r = jnp.arange(GROUP * WINDOW, dtype=I32)[:, None] % WINDOW
    c = jnp.arange(WINDOW, dtype=I32)[None, :]
    masked = jnp.full((GROUP * WINDOW, WINDOW), NEG_INF, F32)
    lo = jnp.stack([jnp.where(c >= r, 0.0, NEG_INF).astype(F32), masked])
    hi = jnp.stack([jnp.where(c <= r, 0.0, NEG_INF).astype(F32), masked])
    return lo, hi


def _dft_tables(na, nb):
    n = na * nb
    s1 = 2.0 ** (-(int(math.log2(na)) // 2))
    s2 = (1.0 / math.sqrt(n)) / s1
    ka = jnp.arange(na, dtype=I32)
    th_tw = ((jnp.arange(nb, dtype=I32)[:, None] * ka[None, :]) % n).astype(F32) * (2.0 * math.pi / n)
    th_f = ((ka[:, None] * ka[None, :]) % na).astype(F32) * (2.0 * math.pi / na)
    tw = jnp.stack([jnp.cos(th_tw) * s1, -jnp.sin(th_tw) * s1])
    tw = tw.reshape(2, nb // SUBLANES_F32, SUBLANES_F32, na).transpose(0, 1, 3, 2)
    f = jnp.stack([jnp.cos(th_f), -jnp.sin(th_f)])
    kb = jnp.arange(nb, dtype=I32)
    th2 = ((kb[:, None] * kb[None, :]) % nb).astype(F32) * (2.0 * math.pi / nb)
    t2 = (jnp.concatenate([jnp.cos(th2), jnp.sin(th2)], axis=-1) * s2).astype(BF16)
    return tw, f, t2


def _dft1_kernel(z_ref, tw_ref, f_ref, y_ref):
    planes, _, rows, _ = z_ref.shape
    step = SUBLANES_F32
    half, na = planes // 2, rows // step
    fr, fi = f_ref[0], f_ref[1]
    for j in range(step):
        twr, twi = tw_ref[0, 0][:, j:j + 1], tw_ref[1, 0][:, j:j + 1]
        tr, ti = twr * fr - twi * fi, twr * fi + twi * fr
        t_b = jnp.concatenate([jnp.concatenate([tr, ti], axis=1),
                               jnp.concatenate([ti, -tr], axis=1)], axis=0).astype(BF16)
        sel = pl.ds(j, na, stride=step)
        p = jnp.concatenate([z_ref[cc, 0, sel, :] for cc in range(half)], axis=1)
        q = jnp.concatenate([z_ref[cc, 0, sel, :] for cc in range(half, planes)], axis=1)
        xs = jnp.concatenate([p, q], axis=0).astype(BF16)
        y = jnp.dot(t_b, xs, preferred_element_type=F32)
        for cc in range(half):
            y_ref[cc, 0, sel, :] = y[:na, cc * LANES:(cc + 1) * LANES]
            y_ref[half + cc, 0, sel, :] = y[na:, cc * LANES:(cc + 1) * LANES]


def _dft2_kernel(y_ref, t_ref, o_ref):
    planes, groups, rows, _ = y_ref.shape
    step = SUBLANES_F32
    half, nb = planes // 2, groups * step

    def tokens_b(cc, j):
        return y_ref[cc, :, j * step:(j + 1) * step, :].reshape(nb, LANES)

    for j in range(rows // step):
        yr = jnp.concatenate([tokens_b(cc, j) for cc in range(half)], axis=1)
        yi = jnp.concatenate([tokens_b(cc, j) for cc in range(half, planes)], axis=1)
        xs = jnp.concatenate([yr, yi], axis=0).astype(BF16)
        out = jnp.dot(t_ref[...], xs, preferred_element_type=F32)
        for cc in range(half):
            o_ref[cc, 0, pl.ds(j, nb, stride=step), :] = out[:, cc * LANES:(cc + 1) * LANES]


def _attn_dft_kernel(*refs, n_total, tq, first_step, stage):
    attn_in, dft_in, (o_ref, d_out) = refs[:12], refs[12:-2], refs[-2:]
    _attn_body(*attn_in, o_ref, n_total=n_total, tq=tq, first_step=first_step)
    (_dft1_kernel if stage == 1 else _dft2_kernel)(*dft_in, d_out)


def _attention_and_dft(q, k, v, kc, vc, sink, z):
    n, m = q.shape[0], kc.shape[0]
    planes, groups, rows, _ = z.shape
    step = SUBLANES_F32
    nb, na = groups * step, rows // step
    tw, f, t2 = _dft_tables(na, nb)
    bias_lo, bias_hi = _band_biases()
    nblk = n // WINDOW
    half = n // 2

    def call(stage, steps, first_row, dft_in, dft_in_specs, d_out_shape, d_out_spec):
        tq = half // steps
        nsub = tq // WINDOW
        first_step = first_row // tq
        main = lambda i, s: (i + first_step, 0)
        prev = lambda i, s: (jnp.maximum((i + first_step) * nsub - 1, 0), 0)
        nxt = lambda i, s: (jnp.minimum((i + first_step) * nsub + nsub, nblk - 1), 0)
        const = lambda i, s: (0, 0)
        kv_specs = [pl.BlockSpec((WINDOW, KV_DIM), prev), pl.BlockSpec((tq, KV_DIM), main),
                    pl.BlockSpec((WINDOW, KV_DIM), nxt)]
        bias_spec = pl.BlockSpec(bias_lo.shape, lambda i, s: (0, 0, 0))
        return pl.pallas_call(
            functools.partial(_attn_dft_kernel, n_total=n, tq=tq, first_step=first_step,
                              stage=stage),
            out_shape=(jax.ShapeDtypeStruct((half, Q_DIM), BF16), d_out_shape),
            grid_spec=pltpu.PrefetchScalarGridSpec(
                num_scalar_prefetch=1,
                grid=(steps,),
                in_specs=[pl.BlockSpec((tq, Q_DIM), main)] + kv_specs + kv_specs
                         + [pl.BlockSpec((m, KV_DIM), const), pl.BlockSpec((m, KV_DIM), const),
                            bias_spec, bias_spec] + dft_in_specs,
                out_specs=(pl.BlockSpec((tq, Q_DIM), lambda i, s: (i, 0)), d_out_spec)),
            compiler_params=_cparams("arbitrary"),
            name=f"attention_half{stage}_dft_stage{stage}",
        )(sink, q, k, k, k, v, v, v, kc, vc, bias_lo, bias_hi, *dft_in)

    ax_first, y = call(
        1, groups, 0, [z, tw, f],
        [pl.BlockSpec((planes, 1, rows, LANES), lambda i, s: (0, i, 0, 0)),
         pl.BlockSpec((2, 1, na, step), lambda i, s: (0, i, 0, 0)),
         pl.BlockSpec((2, na, na), lambda i, s: (0, 0, 0))],
        jax.ShapeDtypeStruct(z.shape, F32),
        pl.BlockSpec((planes, 1, rows, LANES), lambda i, s: (0, i, 0, 0)))
    ax_second, fx = call(
        2, na // step, half, [y, t2],
        [pl.BlockSpec((planes, groups, step * step, LANES), lambda i, s: (0, 0, i, 0)),
         pl.BlockSpec((nb, 2 * nb), lambda i, s: (0, 0))],
        jax.ShapeDtypeStruct((planes // 2, na // step, nb * step, LANES), F32),
        pl.BlockSpec((planes // 2, 1, nb * step, LANES), lambda i, s: (0, i, 0, 0)))
    return ax_first, ax_second, fx


def _outproj_kernel(ax1_ref, ax2_ref, fx_ref, x_ref, wo_ref, gate_ref, nw_ref, sh_ref, sc_ref,
                    wr_ref, x1_ref, g_ref, aff_ref):
    planes, groups, rows, _ = fx_ref.shape
    in_first_half = pl.program_id(0) < pl.num_programs(0) // 2
    na = groups * SUBLANES_F32
    n_kb = rows // SUBLANES_F32
    kb_per_chunk = max(1, n_kb // OUTPROJ_CHUNKS)
    for first_kb in range(0, n_kb, kb_per_chunk):
        rs = slice(first_kb * na, (first_kb + kb_per_chunk) * na)
        fx = jnp.concatenate(
            [jnp.concatenate(
                [fx_ref[cc, :, kb * SUBLANES_F32:(kb + 1) * SUBLANES_F32, :].reshape(na, LANES)
                 for kb in range(first_kb, first_kb + kb_per_chunk)], axis=0)
             for cc in range(planes)], axis=1).astype(BF16)
        ax = jnp.where(in_first_half, ax1_ref[rs, :], ax2_ref[rs, :])
        acc = (jnp.dot(ax, wo_ref[:Q_DIM, :], preferred_element_type=F32)
               + jnp.dot(fx, wo_ref[Q_DIM:, :], preferred_element_type=F32))
        x1 = x_ref[rs, :] + gate_ref[...] * acc
        x1_ref[rs, :] = x1
        gb = _rms_mod(x1, nw_ref[...], sh_ref[...], sc_ref[...]).astype(BF16)
        g_ref[rs, :] = gb
        logits = jnp.dot(gb, wr_ref[...], preferred_element_type=F32)
        e = jnp.exp(logits - jnp.max(logits, axis=1, keepdims=True))
        aff_ref[rs, :] = e / jnp.sum(e, axis=1, keepdims=True)


def _output_projection(ax1, ax2, fx, x, wo_b, gate, nw, shift, scale, wr_b, tm=512):
    n, d = x.shape
    ne = wr_b.shape[1]
    half_steps = n // tm // 2
    row = lambda i: (i, 0)
    const = lambda i: (0, 0)
    vec = pl.BlockSpec((1, d), const)
    return pl.pallas_call(
        _outproj_kernel,
        out_shape=(jax.ShapeDtypeStruct((n, d), F32), jax.ShapeDtypeStruct((n, d), BF16),
                   jax.ShapeDtypeStruct((n, ne), F32)),
        grid=(n // tm,),
        in_specs=[pl.BlockSpec((tm, Q_DIM), lambda i: (jnp.minimum(i, half_steps - 1), 0)),
                  pl.BlockSpec((tm, Q_DIM), lambda i: (jnp.maximum(i - half_steps, 0), 0)),
                  pl.BlockSpec((fx.shape[0], fx.shape[1], tm // (fx.shape[1] * SUBLANES_F32)
                                * SUBLANES_F32, LANES), lambda i: (0, 0, i, 0)),
                  pl.BlockSpec((tm, d), row),
                  pl.BlockSpec((Q_DIM + F_DIM, d), const, pipeline_mode=pl.Buffered(1)),
                  vec, vec, vec, vec, pl.BlockSpec((d, ne), const)],
        out_specs=(pl.BlockSpec((tm, d), row), pl.BlockSpec((tm, d), row),
                   pl.BlockSpec((tm, ne), row)),
        compiler_params=_cparams("arbitrary"),
        name="output_projection",
    )(ax1, ax2, fx, x, wo_b, gate, nw, shift, scale, wr_b)


def _route_kernel(aff_ref, posm_ref, cnt_ref, off_ref, *, cap, n_exp, n_chunk):
    aff_all = aff_ref[...]
    blocks = [aff_all[e * n_chunk:(e + 1) * n_chunk] for e in range(n_exp)]

    def total(mask):
        s = jnp.sum(jnp.where(mask, 1.0, 0.0), axis=0, keepdims=True)
        return jnp.sum(s, axis=1, keepdims=True)

    def search(it, thr_bits):
        bit = jnp.left_shift(jnp.int32(1), 30 - it)
        out = []
        for e in range(n_exp):
            cand = thr_bits[e] | bit
            enough = total(blocks[e] >= lax.bitcast_convert_type(cand, F32)) >= cap
            out.append(jnp.where(enough, cand, thr_bits[e]))
        return tuple(out)

    thr_bits = lax.fori_loop(0, 31, search, tuple(jnp.zeros((1, 1), I32) for _ in range(n_exp)))
    thr = [lax.bitcast_convert_type(t, F32) for t in thr_bits]

    rl = lax.broadcasted_iota(I32, (LANES, LANES), 0)
    cl = lax.broadcasted_iota(I32, (LANES, LANES), 1)
    before = jnp.where(rl < cl, 1.0, 0.0).astype(BF16)
    ones = jnp.ones((LANES, LANES), BF16)
    rc = lax.broadcasted_iota(I32, (n_chunk, n_chunk), 0)
    cc = lax.broadcasted_iota(I32, (n_chunk, n_chunk), 1)
    earlier = jnp.where(cc < rc, 1.0, 0.0).astype(BF16)

    def excl_cumsum(mask):
        mb = jnp.where(mask, 1.0, 0.0).astype(BF16)
        within = jnp.dot(mb, before, preferred_element_type=F32)
        rowtot = jnp.dot(mb, ones, preferred_element_type=F32)
        choff = jnp.dot(earlier, rowtot.astype(BF16), preferred_element_type=F32)
        return within + choff, rowtot, choff

    for e in range(n_exp):
        gt = blocks[e] > thr[e]
        eq = blocks[e] == thr[e]
        need = cap - total(gt)
        tie_rank, _, _ = excl_cumsum(eq)
        sel = gt | (eq & (tie_rank < need))
        pos, rowtot, choff = excl_cumsum(sel)
        rs = slice(e * n_chunk, (e + 1) * n_chunk)
        posm_ref[rs, :] = jnp.where(sel, pos.astype(I32), NOT_SELECTED)
        cnt_ref[rs, :] = rowtot.astype(I32)
        off_ref[rs, :] = choff.astype(I32)


def _route(aff2, cap, n_exp):
    rows = aff2.shape[0]
    n_chunk = rows // n_exp
    spec = pl.BlockSpec((rows, LANES), lambda i: (0, 0))
    shp = jax.ShapeDtypeStruct((rows, LANES), I32)
    return pl.pallas_call(
        functools.partial(_route_kernel, cap=cap, n_exp=n_exp, n_chunk=n_chunk),
        out_shape=(shp, shp, shp),
        grid=(1,),
        in_specs=[spec],
        out_specs=(spec, spec, spec),
        compiler_params=_cparams("arbitrary"),
        name="expert_choice_routing",
    )(aff2)


def _window_matches(posm, firsts, win_starts):
    r = lax.broadcasted_iota(I32, (WINDOW_ROWS, TOK_TILE), 0)
    out = []
    for e, (first, start) in enumerate(zip(firsts, win_starts)):
        pm = posm[e:e + 1, :]
        pm = jnp.where((pm >= first) & (pm < first + PIECE), pm, NOT_SELECTED)
        out.append((pm - start) == r)
    return out


def _align_down(v):
    return (v // ROW_ALIGN) * ROW_ALIGN


def _for_window_size(rows_needed, fn):
    below = None
    for k, size in enumerate(WINDOW_SIZES):
        fits = rows_needed <= size if k + 1 < len(WINDOW_SIZES) else None
        cond = fits if below is None else (below if fits is None else jnp.logical_and(below, fits))
        pl.when(cond)(functools.partial(fn, size))
        below = rows_needed > size


def _gather_kernel(off_ref, cnt_ref, g_ref, posm_ref, aff_ref, xs_hbm,
                   stg, tail, sem, npass_ref, sent_ref, *, n_exp, n_chunk, cap):
    c = pl.program_id(0)
    stride = cap + WINDOW_ROWS

    @pl.when(c == 0)
    def _():
        npass_ref[0] = 0
        tail[...] = jnp.zeros_like(tail)

    def window_copy(slot, e, dst, size=WINDOW_ROWS):
        return pltpu.make_async_copy(stg.at[slot, pl.ds(e * WINDOW_ROWS, size)],
                                     xs_hbm.at[pl.ds(dst, size)], sem.at[slot, e])

    def start_window(slot, e, dst, rows_used):
        _for_window_size(rows_used, lambda size: window_copy(slot, e, dst, size).start())
        sent_ref[slot * n_exp + e] = rows_used

    def wait_window(slot, e):
        _for_window_size(sent_ref[slot * n_exp + e],
                         lambda size: window_copy(slot, e, 0, size).wait())

    @pl.when(c == 0)
    def _():
        stg[0] = jnp.zeros(stg.shape[1:], stg.dtype)
        for e in range(n_exp):
            window_copy(0, e, e * stride + cap).start()
        for e in range(n_exp):
            window_copy(0, e, 0).wait()

    most = cnt_ref[c]
    for e in range(1, n_exp):
        most = jnp.maximum(most, cnt_ref[e * n_chunk + c])
    passes = (most + PIECE - 1) // PIECE

    def one_pass(p, carry):
        done = npass_ref[0]
        slot = done % 2
        posm = posm_ref[0]
        aff = aff_ref[0]
        firsts, win_starts, next_shift, rows_used = [], [], [], []
        for e in range(n_exp):
            o, n_e = off_ref[e * n_chunk + c], cnt_ref[e * n_chunk + c]
            first = o + jnp.minimum(p * PIECE, n_e)
            after = o + jnp.minimum((p + 1) * PIECE, n_e)
            firsts.append(first)
            win_starts.append(_align_down(first))
            next_shift.append(_align_down(after) - _align_down(first))
            rows_used.append(after - _align_down(first))
        matches = _window_matches(posm, firsts, win_starts)
        onehot = jnp.concatenate([jnp.where(mt, 1.0, 0.0).astype(BF16) for mt in matches], axis=0)
        new = jnp.dot(onehot, g_ref[...], preferred_element_type=F32)
        lane = lax.broadcasted_iota(I32, (WINDOW_ROWS, GATE_LANES), 1)
        for e in range(n_exp):
            lo = e * WINDOW_ROWS
            gate = jnp.sum(jnp.where(matches[e], aff[e:e + 1, :], 0.0), axis=1, keepdims=True)
            hi = gate.astype(BF16).astype(F32)
            mid = (gate - hi).astype(BF16).astype(F32)
            low = gate - hi - mid
            parts = jnp.where(lane == 0, hi, jnp.where(lane == 1, mid,
                                                       jnp.where(lane == 2, low, 0.0)))
            rows = jnp.concatenate([new[lo:lo + WINDOW_ROWS], parts], axis=1)
            stg[slot, lo:lo + ROW_ALIGN] = (rows[:ROW_ALIGN] + tail[e].astype(F32)).astype(BF16)
            stg[slot, lo + ROW_ALIGN:lo + WINDOW_ROWS] = rows[ROW_ALIGN:].astype(BF16)

        dsts = [pl.multiple_of(e * stride + win_starts[e], ROW_ALIGN) for e in range(n_exp)]

        @pl.when(done > 0)
        def _():
            for e in range(n_exp):
                wait_window(1 - slot, e)
                start_window(slot, e, dsts[e], rows_used[e])

        @pl.when(done == 0)
        def _():
            for e in range(n_exp):
                start_window(slot, e, dsts[e], rows_used[e])

        for e in range(n_exp):
            src = pl.ds(pl.multiple_of(e * WINDOW_ROWS + next_shift[e], ROW_ALIGN), ROW_ALIGN)
            tail[e] = stg[slot, src, :]
        npass_ref[0] = done + 1
        return carry

    lax.fori_loop(0, passes, one_pass, 0)

    @pl.when(c == pl.num_programs(0) - 1)
    def _():
        done = npass_ref[0]

        @pl.when(done > 0)
        def _():
            for e in range(n_exp):
                wait_window((done - 1) % 2, e)


def _gather(off, cnt, g, posm_t, aff_t, cap):
    n, d = g.shape
    n_chunk, n_exp, _ = posm_t.shape
    rows = n_exp * (cap + WINDOW_ROWS)
    tile3 = lambda i, o, c: (i, 0, 0)
    return pl.pallas_call(
        functools.partial(_gather_kernel, n_exp=n_exp, n_chunk=n_chunk, cap=cap),
        out_shape=jax.ShapeDtypeStruct((rows, d + GATE_LANES), BF16),
        grid_spec=pltpu.PrefetchScalarGridSpec(
            num_scalar_prefetch=2,
            grid=(n_chunk,),
            in_specs=[pl.BlockSpec((TOK_TILE, d), lambda i, o, c: (i, 0)),
                      pl.BlockSpec((1, n_exp, TOK_TILE), tile3),
                      pl.BlockSpec((1, n_exp, TOK_TILE), tile3)],
            out_specs=pl.BlockSpec(memory_space=pl.ANY),
            scratch_shapes=[pltpu.VMEM((2, n_exp * WINDOW_ROWS, d + GATE_LANES), BF16),
                            pltpu.VMEM((n_exp, ROW_ALIGN, d + GATE_LANES), BF16),
                            pltpu.SemaphoreType.DMA((2, n_exp)),
                            pltpu.SMEM((1,), I32),
                            pltpu.SMEM((2 * n_exp,), I32)]),
        compiler_params=_cparams("arbitrary"),
        name="gather_expert_tokens",
    )(off, cnt, g, posm_t, aff_t)


def _expert_kernel(xs_ref, wg_ref, wu_ref, wd_ref, y_ref, h_ref, gate_ref, *, nf, tf, tn):
    j = pl.program_id(1)
    d = wg_ref.shape[1]
    cap = xs_ref.shape[1]
    halves = [slice(r * (cap // EXPERT_ROW_SPLIT), (r + 1) * (cap // EXPERT_ROW_SPLIT))
              for r in range(EXPERT_ROW_SPLIT)]

    @pl.when(j == 0)
    def _():
        parts = xs_ref[0, :, d:].astype(F32)
        gate = parts[:, 0:1] + parts[:, 1:2] + parts[:, 2:3]
        gate_ref[...] = jnp.broadcast_to(gate, gate_ref.shape)

    @pl.when(j < nf)
    def _():
        wg = wg_ref[0].astype(BF16)
        wu = wu_ref[0].astype(BF16)
        for rs in halves:
            xs = xs_ref[0, rs, :d]
            gp = jnp.dot(xs, wg, preferred_element_type=F32)
            up = jnp.dot(xs, wu, preferred_element_type=F32)
            h = ((gp * jax.nn.sigmoid(gp)) * up).astype(BF16)
            for jj in range(nf):
                @pl.when(j == jj)
                def _():
                    h_ref[rs, jj * tf:(jj + 1) * tf] = h

    @pl.when(j >= nf)
    def _():
        wd = wd_ref[0].astype(BF16)
        for rs in halves:
            y = jnp.dot(h_ref[rs, :], wd, preferred_element_type=F32)
            y_ref[0, rs, :] = (y * gate_ref[rs, 0:1]).astype(BF16)


def _experts(xs3, w_gate, w_up, w_down, cap, tf=512, tn=512):
    n_exp, d, ff = w_gate.shape
    tf, tn = min(tf, ff), min(tn, d)
    nf, nn = ff // tf, d // tn

    def ahead(e, j):
        return jnp.minimum(e + (j >= nf).astype(I32), n_exp - 1)

    def ff_tile(e, j):
        return jnp.where(j >= nf, jnp.where(e == n_exp - 1, nf - 1, 0), j)

    return pl.pallas_call(
        functools.partial(_expert_kernel, nf=nf, tf=tf, tn=tn),
        out_shape=jax.ShapeDtypeStruct((n_exp, cap, d), BF16),
        grid=(n_exp, nf + nn),
        in_specs=[pl.BlockSpec((1, cap, d + GATE_LANES), lambda e, j: (ahead(e, j), 0, 0)),
                  pl.BlockSpec((1, d, tf), lambda e, j: (ahead(e, j), 0, ff_tile(e, j))),
                  pl.BlockSpec((1, d, tf), lambda e, j: (ahead(e, j), 0, ff_tile(e, j))),
                  pl.BlockSpec((1, ff, tn), lambda e, j: (e, 0, jnp.maximum(j - nf, 0)))],
        out_specs=pl.BlockSpec((1, cap, tn), lambda e, j: (e, 0, jnp.maximum(j - nf, 0))),
        scratch_shapes=[pltpu.VMEM((cap, ff), BF16), pltpu.VMEM((cap, LANES), F32)],
        compiler_params=_cparams("arbitrary", "arbitrary"),
        name="swiglu_experts",
    )(xs3, w_gate, w_up, w_down)


def _combine_kernel(off_ref, cnt_ref, y_hbm, posm_ref, x1_ref, gate_ref, nw_ref, o_ref,
                    stg, acc_ref, sem, *, n_exp, n_chunk, cap):
    c = pl.program_id(0)
    last_start = n_exp * cap - WINDOW_ROWS

    def firsts_of(tile, p):
        return [off_ref[e * n_chunk + tile] + p * PIECE for e in range(n_exp)]

    def starts_of(tile, p):
        return [pl.multiple_of(jnp.minimum(e * cap + _align_down(first), last_start), ROW_ALIGN)
                for e, first in enumerate(firsts_of(tile, p))]

    def rows_used_of(tile, p):
        out = []
        for e, (first, start) in enumerate(zip(firsts_of(tile, p), starts_of(tile, p))):
            left = jnp.clip(cnt_ref[e * n_chunk + tile] - p * PIECE, 0, PIECE)
            out.append(e * cap + first + left - start)
        return out

    def piece_copy(slot, e, src, size=WINDOW_ROWS):
        return pltpu.make_async_copy(y_hbm.at[pl.ds(src, size)],
                                     stg.at[slot, pl.ds(e * WINDOW_ROWS, size)], sem.at[slot])

    def fetch(tile, p, slot):
        for e, (src, used) in enumerate(zip(starts_of(tile, p), rows_used_of(tile, p))):
            _for_window_size(used, lambda size: piece_copy(slot, e, src, size).start())

    def wait_fetch(tile, p, slot):
        for e, used in enumerate(rows_used_of(tile, p)):
            _for_window_size(used, lambda size: piece_copy(slot, e, 0, size).wait())

    def contribution(p, slot, tokens):
        starts = [s - e * cap for e, s in enumerate(starts_of(c, p))]
        matches = _window_matches(posm_ref[0], firsts_of(c, p), starts)
        onehot = jnp.concatenate([jnp.where(mt[:, tokens], 1.0, 0.0).astype(BF16)
                                  for mt in matches], axis=0)
        return lax.dot_general(onehot, stg[slot], (((0,), (0,)), ((), ())),
                               preferred_element_type=F32)

    def finish(tokens, moe):
        x2 = x1_ref[tokens, :] + gate_ref[...] * moe
        ms = jnp.mean(x2 * x2, axis=-1, keepdims=True)
        o_ref[tokens, :] = x2 * lax.rsqrt(ms + RMS_EPS) * nw_ref[...]

    slot = c % 2

    @pl.when(c == 0)
    def _():
        stg[...] = jnp.zeros_like(stg)
        fetch(0, 0, 0)

    @pl.when(c + 1 < pl.num_programs(0))
    def _():
        fetch(c + 1, 0, 1 - slot)

    most = cnt_ref[c]
    for e in range(1, n_exp):
        most = jnp.maximum(most, cnt_ref[e * n_chunk + c])
    passes = (most + PIECE - 1) // PIECE
    wait_fetch(c, 0, slot)

    @pl.when(passes <= 1)
    def _():
        for t0 in range(0, TOK_TILE, LANES):
            tokens = slice(t0, t0 + LANES)
            finish(tokens, contribution(0, slot, tokens))

    @pl.when(passes > 1)
    def _():
        everyone = slice(0, TOK_TILE)
        acc_ref[...] = contribution(0, slot, everyone)

        def extra_pass(p, carry):
            fetch(c, p, slot)
            wait_fetch(c, p, slot)
            acc_ref[...] += contribution(p, slot, everyone)
            return carry

        lax.fori_loop(1, passes, extra_pass, 0)
        finish(everyone, acc_ref[...])


def _combine(off, cnt, y, posm_t, x1, gate, nw, cap):
    n, d = x1.shape
    n_chunk, n_exp, _ = posm_t.shape
    return pl.pallas_call(
        functools.partial(_combine_kernel, n_exp=n_exp, n_chunk=n_chunk, cap=cap),
        out_shape=jax.ShapeDtypeStruct((n, d), F32),
        grid_spec=pltpu.PrefetchScalarGridSpec(
            num_scalar_prefetch=2,
            grid=(n_chunk,),
            in_specs=[pl.BlockSpec(memory_space=pl.ANY),
                      pl.BlockSpec((1, n_exp, TOK_TILE), lambda i, o, c: (i, 0, 0)),
                      pl.BlockSpec((TOK_TILE, d), lambda i, o, c: (i, 0)),
                      pl.BlockSpec((1, d), lambda i, o, c: (0, 0)),
                      pl.BlockSpec((1, d), lambda i, o, c: (0, 0))],
            out_specs=pl.BlockSpec((TOK_TILE, d), lambda i, o, c: (i, 0)),
            scratch_shapes=[pltpu.VMEM((2, n_exp * WINDOW_ROWS, d), BF16),
                            pltpu.VMEM((TOK_TILE, d), F32),
                            pltpu.SemaphoreType.DMA((2,))]),
        compiler_params=_cparams("arbitrary"),
        name="combine_experts_final_norm",
    )(off, cnt, y, posm_t, x1, gate, nw)


def kernel(x, c, ctx, c_ctx, w_mod, b_mod, norm_mix, w_in, sink, w_fourier, w_out, norm_ffn,
           w_router, w_gate, w_up, w_down, norm_final):
    assert x.shape[0] == 1 and w_mod.shape[0] == 1
    n, d = x.shape[1], x.shape[2]
    xl, cx = x[0], ctx[0]
    n_exp = w_router.shape[2]
    cap = max(1, CAP_FACTOR * n // n_exp)
    n_chunk = n // LANES
    n_tile = n // TOK_TILE
    per_tile = TOK_TILE // LANES

    cvec = jnp.stack([c[0], c_ctx], axis=1)
    b_mod2 = b_mod[0][None, :]
    mod_head = _modulation(cvec, w_mod[0], b_mod2, 2 * d)
    mx_head = mod_head[0].reshape(2, 1, d)
    mc = mod_head[1].reshape(2, 1, d)

    w_in_b = w_in[0].astype(BF16)
    nmix = norm_mix[0][None, :]
    ab = _fold_channel_dft(w_fourier[0])
    q, k, v, z, w_out_b, mod_tail = _project_latent(xl, nmix, mx_head[0], mx_head[1], w_in_b, ab,
                                                    w_out[0], cvec, w_mod[0], b_mod2, 2 * d)
    mx = jnp.concatenate([mx_head, mod_tail[0].reshape(4, 1, d)], axis=0)
    kc, vc = _project_context(cx, nmix, mc[0], mc[1], w_in_b[:, Q_DIM:Q_DIM + 2 * KV_DIM])
    ax1, ax2, fx = _attention_and_dft(q, k, v, kc, vc, sink[0], z)

    x1, g, aff = _output_projection(ax1, ax2, fx, xl, w_out_b, mx[2], norm_ffn[0][None, :],
                                    mx[3], mx[4], w_router[0].astype(BF16))

    aff2 = aff.T.reshape(n_exp * n_chunk, LANES)
    posm, cnt, off = _route(aff2, cap, n_exp)
    by_tile = lambda a: a.reshape(n_exp, n_tile, TOK_TILE).transpose(1, 0, 2)
    posm_t, aff_t = by_tile(posm), by_tile(aff2)
    cnt1 = cnt[:, 0].reshape(n_exp * n_tile, per_tile).sum(axis=1)
    off1 = off[:, 0].reshape(n_exp * n_tile, per_tile)[:, 0]

    xs = _gather(off1, cnt1, g, posm_t, aff_t, cap)
    y = _experts(xs.reshape(n_exp, cap + WINDOW_ROWS, d + GATE_LANES),
                 w_gate[0], w_up[0], w_down[0], cap)
    out = _combine(off1, cnt1, y.reshape(n_exp * cap, d), posm_t, x1, mx[5], norm_final[None, :],
                   cap)
    return out[None]
```

```python
import functools
import math

import jax
import jax.numpy as jnp
from jax import lax
from jax.experimental import pallas as pl
from jax.experimental.pallas import tpu as pltpu

F32 = jnp.float32
BF16 = jnp.bfloat16
I32 = jnp.int32

HEAD_DIM = 128
N_HEADS = 8
N_KV_HEADS = 2
GROUP = N_HEADS // N_KV_HEADS
WINDOW = 128
Q_DIM = N_HEADS * HEAD_DIM
KV_DIM = N_KV_HEADS * HEAD_DIM
N_FGROUPS = 4
FG = 256
F_DIM = N_FGROUPS * FG
D_IN = Q_DIM + 2 * KV_DIM + F_DIM
N_EXPERTS = 16
CAP_FACTOR = 2
GRID_W = 64
ROPE_THETA = 10000.0
RMS_EPS = 1e-6
NEG_INF = -1e30
LOG2_E = math.log2(math.e)

LANES = 128
SUBLANES_F32 = 8
VMEM_LIMIT_BYTES = 56 * 1024 * 1024

TOK_TILE = 2 * LANES
PIECE = 48
ROW_ALIGN = 16
WINDOW_ROWS = ROW_ALIGN + PIECE
WINDOW_SIZES = tuple(range(2 * ROW_ALIGN, WINDOW_ROWS + 1, ROW_ALIGN))
GATE_LANES = LANES
DFT_NB = LANES
OUTPROJ_CHUNKS = 2
PROJ_CHUNKS = 2
EXPERT_ROW_SPLIT = 2
NOT_SELECTED = -(1 << 20)


def _cparams(*sem):
    return pltpu.CompilerParams(dimension_semantics=sem, vmem_limit_bytes=VMEM_LIMIT_BYTES)


def _rms_mod(x, nw, shift, scale):
    ms = jnp.mean(x * x, axis=-1, keepdims=True)
    return x * lax.rsqrt(ms + RMS_EPS) * (nw * (1.0 + scale)) + shift


def _mod_kernel(ct_ref, w_ref, b_ref, o_ref):
    ct = ct_ref[...]
    s = ct * jax.nn.sigmoid(ct)
    w = w_ref[...]
    r0 = jnp.sum(s[:, 0:1] * w, axis=0, keepdims=True)
    r1 = jnp.sum(s[:, 1:2] * w, axis=0, keepdims=True)
    o_ref[...] = jnp.concatenate([r0, r1], axis=0) + b_ref[...]


def _ab_kernel(cf_ref, sf_ref, wf_ref, ab_ref):
    wf = wf_ref[0]
    a = jnp.dot(cf_ref[...], wf, preferred_element_type=F32, precision=lax.Precision.HIGHEST)
    b = jnp.dot(sf_ref[...], wf, preferred_element_type=F32, precision=lax.Precision.HIGHEST)
    ab_ref[0] = jnp.concatenate([a, b], axis=1).astype(BF16)


def _mod_fold_kernel(ct_ref, w_ref, b_ref, cf_ref, sf_ref, wf_ref, o_ref, ab_ref):
    _mod_kernel(ct_ref, w_ref, b_ref, o_ref)
    _ab_kernel(cf_ref, sf_ref, wf_ref, ab_ref)


def _modulation_head(ct, w_mod, b_mod, n_cols, w_fourier):
    d = w_mod.shape[0]
    g, fg, _ = w_fourier.shape
    tn = n_cols // g
    idx = jnp.arange(fg, dtype=I32)
    th = ((idx[:, None] * idx[None, :]) % fg).astype(F32) * (2.0 * math.pi / fg)
    scale = 1.0 / math.sqrt(fg)
    cf = jnp.cos(th) * scale
    sf = jnp.sin(th) * scale
    return pl.pallas_call(
        _mod_fold_kernel,
        out_shape=(jax.ShapeDtypeStruct((2, n_cols), F32),
                   jax.ShapeDtypeStruct((g, fg, 2 * fg), BF16)),
        grid=(g,),
        in_specs=[pl.BlockSpec((d, 2), lambda i: (0, 0)),
                  pl.BlockSpec((d, tn), lambda i: (0, i)),
                  pl.BlockSpec((1, tn), lambda i: (0, i)),
                  pl.BlockSpec((fg, fg), lambda i: (0, 0)),
                  pl.BlockSpec((fg, fg), lambda i: (0, 0)),
                  pl.BlockSpec((1, fg, fg), lambda i: (i, 0, 0))],
        out_specs=(pl.BlockSpec((2, tn), lambda i: (0, i)),
                   pl.BlockSpec((1, fg, 2 * fg), lambda i: (i, 0, 0))),
        compiler_params=_cparams("arbitrary"),
        name="modulation_head_fold_channel_dft",
    )(ct, w_mod, b_mod, cf, sf, w_fourier)


def _rope_tables(n):
    quarter = HEAD_DIM // 4
    freqs = ROPE_THETA ** (-jnp.arange(quarter, dtype=F32) / quarter)
    zeros = lambda m: jnp.zeros((m, 2 * quarter), F32)

    def tables(pos, low_half):
        ang = pos[:, None] * freqs[None, :]
        ang = jnp.concatenate([ang, ang], axis=-1)
        cos, sin = jnp.cos(ang), jnp.sin(ang)
        first = (jnp.arange(2 * quarter) < quarter)[None, :]
        parts = (cos, jnp.where(first, -sin, 0.0), jnp.where(first, 0.0, sin))
        pad = zeros(pos.shape[0])
        return jnp.stack([jnp.concatenate([p, pad] if low_half else [pad, p], axis=-1)
                          for p in parts])

    row_tab = tables(jnp.arange(n // GRID_W, dtype=F32), True)
    col_tab = tables(jnp.arange(GRID_W, dtype=F32), False)
    return row_tab, col_tab


def _store_dft_rows(z_ref, plane, val, a0):
    groups = z_ref.shape[1]
    nb = groups * SUBLANES_F32
    for al in range(val.shape[0] // nb):
        z_ref[plane, :, (a0 + al) * SUBLANES_F32:(a0 + al + 1) * SUBLANES_F32, :] = (
            val[al * nb:(al + 1) * nb].reshape(groups, SUBLANES_F32, LANES))


def _proj_kernel(x_ref, nw_ref, sh_ref, sc_ref, w_ref, ab_ref, rt_ref, ct_ref, wo_ref,
                 cvec_ref, wm_ref, bm_ref, cx_ref, csh_ref, csc_ref,
                 q_ref, k_ref, v_ref, z_ref, wo_b_ref, mod_ref, kc_ref, vc_ref):
    wo_b_ref[...] = wo_ref[...].astype(BF16)
    _mod_kernel(cvec_ref, wm_ref, bm_ref, mod_ref)

    @pl.when(pl.program_id(0) == 0)
    def _():
        hc = _rms_mod(cx_ref[...], nw_ref[...], csh_ref[...], csc_ref[...]).astype(BF16)
        kv = jnp.dot(hc, w_ref[:, Q_DIM:Q_DIM + 2 * KV_DIM], preferred_element_type=F32)
        kc_ref[...] = kv[:, :KV_DIM].astype(BF16)
        vc_ref[...] = kv[:, KV_DIM:].astype(BF16)

    quarter = HEAD_DIM // 4
    qscale = LOG2_E / math.sqrt(HEAD_DIM)
    tm = x_ref.shape[0]
    chunk = tm // PROJ_CHUNKS
    for r0 in range(0, tm, chunk):
        rs = slice(r0, r0 + chunk)
        hb = _rms_mod(x_ref[rs, :], nw_ref[...], sh_ref[...], sc_ref[...]).astype(BF16)
        cos, s_up, s_dn = (
            jnp.concatenate(
                [rt_ref[t, r:r + 1, :] + ct_ref[t]
                 for r in range(r0 // GRID_W, (r0 + chunk) // GRID_W)], axis=0)
            for t in range(3))

        def rope(t):
            return (t * cos + pltpu.roll(t, HEAD_DIM - quarter, 1) * s_up
                    + pltpu.roll(t, quarter, 1) * s_dn)

        q = jnp.dot(hb, w_ref[:, :Q_DIM], preferred_element_type=F32)
        for j in range(N_HEADS):
            sl = slice(j * HEAD_DIM, (j + 1) * HEAD_DIM)
            q_ref[rs, sl] = (rope(q[:, sl]) * qscale).astype(BF16)
        k = jnp.dot(hb, w_ref[:, Q_DIM:Q_DIM + KV_DIM], preferred_element_type=F32)
        for j in range(N_KV_HEADS):
            sl = slice(j * HEAD_DIM, (j + 1) * HEAD_DIM)
            k_ref[rs, sl] = rope(k[:, sl]).astype(BF16)
        v = jnp.dot(hb, w_ref[:, Q_DIM + KV_DIM:Q_DIM + 2 * KV_DIM], preferred_element_type=F32)
        v_ref[rs, :] = v.astype(BF16)
        u = jnp.dot(hb, w_ref[:, Q_DIM + 2 * KV_DIM:], preferred_element_type=F32).astype(BF16)
        for g in range(N_FGROUPS):
            pq = jnp.dot(u[:, g * FG:(g + 1) * FG], ab_ref[g], preferred_element_type=F32)
            per_group = FG // LANES
            for t in range(2 * per_group):
                plane = (t // per_group) * (F_DIM // LANES) + g * per_group + t % per_group
                _store_dft_rows(z_ref, plane, pq[:, t * LANES:(t + 1) * LANES], r0 // DFT_NB)


def _project_latent(x, nw, shift, scale, w_in_b, ab, w_out, cvec, w_mod, b_mod, mod_first,
                    ctx, ctx_shift, ctx_scale, tm=512):
    m = ctx.shape[0]
    n, d = x.shape
    steps = n // tm
    wo_rows = w_out.shape[0] // steps
    mod_cols = (w_mod.shape[1] - mod_first) // steps
    mod_blk0 = mod_first // mod_cols
    row_tab, col_tab = _rope_tables(n)
    row = lambda i: (i, 0)
    const2 = lambda i: (0, 0)
    return pl.pallas_call(
        _proj_kernel,
        out_shape=(jax.ShapeDtypeStruct((n, Q_DIM), BF16),
                   jax.ShapeDtypeStruct((n, KV_DIM), BF16),
                   jax.ShapeDtypeStruct((n, KV_DIM), BF16),
                   jax.ShapeDtypeStruct((2 * F_DIM // LANES, DFT_NB // SUBLANES_F32,
                                         (n // DFT_NB) * SUBLANES_F32, LANES), F32),
                   jax.ShapeDtypeStruct(w_out.shape, BF16),
                   jax.ShapeDtypeStruct((2, w_mod.shape[1] - mod_first), F32),
                   jax.ShapeDtypeStruct((m, KV_DIM), BF16),
                   jax.ShapeDtypeStruct((m, KV_DIM), BF16)),
        grid=(n // tm,),
        in_specs=[pl.BlockSpec((tm, d), row),
                  pl.BlockSpec((1, d), const2), pl.BlockSpec((1, d), const2),
                  pl.BlockSpec((1, d), const2),
                  pl.BlockSpec((d, D_IN), const2, pipeline_mode=pl.Buffered(1)),
                  pl.BlockSpec((N_FGROUPS, FG, 2 * FG), lambda i: (0, 0, 0),
                               pipeline_mode=pl.Buffered(1)),
                  pl.BlockSpec((3, tm // GRID_W, HEAD_DIM), lambda i: (0, i, 0)),
                  pl.BlockSpec((3, GRID_W, HEAD_DIM), lambda i: (0, 0, 0)),
                  pl.BlockSpec((wo_rows, w_out.shape[1]), row),
                  pl.BlockSpec(cvec.shape, const2),
                  pl.BlockSpec((w_mod.shape[0], mod_cols), lambda i: (0, mod_blk0 + i)),
                  pl.BlockSpec((1, mod_cols), lambda i: (0, mod_blk0 + i)),
                  pl.BlockSpec((m, d), const2), pl.BlockSpec((1, d), const2),
                  pl.BlockSpec((1, d), const2)],
        out_specs=(pl.BlockSpec((tm, Q_DIM), row), pl.BlockSpec((tm, KV_DIM), row),
                   pl.BlockSpec((tm, KV_DIM), row),
                   pl.BlockSpec((2 * F_DIM // LANES, DFT_NB // SUBLANES_F32,
                                 (tm // DFT_NB) * SUBLANES_F32, LANES), lambda i: (0, 0, i, 0)),
                   pl.BlockSpec((wo_rows, w_out.shape[1]), row),
                   pl.BlockSpec((2, mod_cols), lambda i: (0, i)),
                   pl.BlockSpec((m, KV_DIM), const2), pl.BlockSpec((m, KV_DIM), const2)),
        compiler_params=_cparams("arbitrary"),
        name="project_latent",
    )(x, nw, shift, scale, w_in_b, ab, row_tab, col_tab, w_out, cvec, w_mod, b_mod,
      ctx, ctx_shift, ctx_scale)


def _attn_body(sink_ref, q_ref, kp_ref, km_ref, kn_ref, vp_ref, vm_ref, vn_ref,
               kc_ref, vc_ref, lo_ref, hi_ref, o_ref, *, n_total, tq, first_step):
    i = pl.program_id(0) + first_step
    nsub = tq // WINDOW
    last_blk = n_total // WINDOW - 1
    kwin = jnp.concatenate([kp_ref[...], km_ref[...], kn_ref[...]], axis=0)
    vwin = jnp.concatenate([vp_ref[...], vm_ref[...], vn_ref[...]], axis=0)
    rows = GROUP * WINDOW
    span = 3 * WINDOW
    head_of_row = lax.broadcasted_iota(I32, (rows, 1), 0) // WINDOW
    ones_loc = jnp.ones((span, HEAD_DIM), BF16)
    ones_ctx = jnp.ones((kc_ref.shape[0], HEAD_DIM), BF16)
    nt = (((1,), (1,)), ((), ()))
    for b in range(nsub):
        blk = i * nsub + b
        bias_lo = lo_ref[(blk == 0).astype(I32)]
        bias_hi = hi_ref[(blk == last_blk).astype(I32)]
        for h in range(N_KV_HEADS):
            hs = slice(h * HEAD_DIM, (h + 1) * HEAD_DIM)
            qs = jnp.concatenate(
                [q_ref[b * WINDOW:(b + 1) * WINDOW,
                       (h * GROUP + g) * HEAD_DIM:(h * GROUP + g + 1) * HEAD_DIM]
                 for g in range(GROUP)], axis=0)
            kw = kwin[b * WINDOW:b * WINDOW + span, hs]
            vw = vwin[b * WINDOW:b * WINDOW + span, hs]
            s_loc = lax.dot_general(qs, kw, nt, preferred_element_type=F32)
            s_ctx = lax.dot_general(qs, kc_ref[:, hs], nt, preferred_element_type=F32)
            parts = [s_loc[:, :WINDOW] + bias_lo, s_loc[:, WINDOW:2 * WINDOW],
                     s_loc[:, 2 * WINDOW:] + bias_hi, s_ctx]
            sink_col = jnp.zeros((rows, 1), F32)
            for g in range(GROUP):
                sink_col = jnp.where(head_of_row == g, sink_ref[h * GROUP + g] * LOG2_E, sink_col)
            blocks = parts[:3] + [s_ctx[:, t * WINDOW:(t + 1) * WINDOW]
                                  for t in range(s_ctx.shape[1] // WINDOW)]
            widest = blocks[0]
            for blk_scores in blocks[1:]:
                widest = jnp.maximum(widest, blk_scores)
            m = jnp.maximum(sink_col, jnp.max(widest, axis=1, keepdims=True))
            p = [jnp.exp2(part - m).astype(BF16) for part in parts]
            ov = (jnp.dot(jnp.concatenate(p[:3], axis=1), jnp.concatenate([vw, ones_loc], axis=1),
                          preferred_element_type=F32)
                  + jnp.dot(p[3], jnp.concatenate([vc_ref[:, hs], ones_ctx], axis=1),
                            preferred_element_type=F32))
            o = ov[:, :HEAD_DIM] / (ov[:, HEAD_DIM:] + jnp.exp2(sink_col - m))
            for g in range(GROUP):
                o_ref[b * WINDOW:(b + 1) * WINDOW,
                      (h * GROUP + g) * HEAD_DIM:(h * GROUP + g + 1) * HEAD_DIM] = (
                    o[g * WINDOW:(g + 1) * WINDOW].astype(BF16))


def _band_biases():
    r = jnp.arange(GROUP * WINDOW, dtype=I32)[:, None] % WINDOW
    c = jnp.arange(WINDOW, dtype=I32)[None, :]
    masked = jnp.full((GROUP * WINDOW, WINDOW), NEG_INF, F32)
    lo = jnp.stack([jnp.where(c >= r, 0.0, NEG_INF).astype(F32), masked])
    hi = jnp.stack([jnp.where(c <= r, 0.0, NEG_INF).astype(F32), masked])
    return lo, hi


def _dft_tables(na, nb):
    n = na * nb
    s1 = 2.0 ** (-(int(math.log2(na)) // 2))
    s2 = (1.0 / math.sqrt(n)) / s1
    ka = jnp.arange(na, dtype=I32)
    th_tw = ((jnp.arange(nb, dtype=I32)[:, None] * ka[None, :]) % n).astype(F32) * (2.0 * math.pi / n)
    th_f = ((ka[:, None] * ka[None, :]) % na).astype(F32) * (2.0 * math.pi / na)
    tw = jnp.stack([jnp.cos(th_tw) * s1, -jnp.sin(th_tw) * s1])
    tw = tw.reshape(2, nb // SUBLANES_F32, SUBLANES_F32, na).transpose(0, 1, 3, 2)
    f = jnp.stack([jnp.cos(th_f), -jnp.sin(th_f)])
    kb = jnp.arange(nb, dtype=I32)
    th2 = ((kb[:, None] * kb[None, :]) % nb).astype(F32) * (2.0 * math.pi / nb)
    t2 = (jnp.concatenate([jnp.cos(th2), jnp.sin(th2)], axis=-1) * s2).astype(BF16)
    return tw, f, t2


def _dft1_kernel(z_ref, tw_ref, f_ref, y_ref):
    planes, _, rows, _ = z_ref.shape
    step = SUBLANES_F32
    half, na = planes // 2, rows // step
    fr, fi = f_ref[0], f_ref[1]
    for j in range(step):
        twr, twi = tw_ref[0, 0][:, j:j + 1], tw_ref[1, 0][:, j:j + 1]
        tr, ti = twr * fr - twi * fi, twr * fi + twi * fr
        t_b = jnp.concatenate([jnp.concatenate([tr, ti], axis=1),
                               jnp.concatenate([ti, -tr], axis=1)], axis=0).astype(BF16)
        sel = pl.ds(j, na, stride=step)
        p = jnp.concatenate([z_ref[cc, 0, sel, :] for cc in range(half)], axis=1)
        q = jnp.concatenate([z_ref[cc, 0, sel, :] for cc in range(half, planes)], axis=1)
        xs = jnp.concatenate([p, q], axis=0).astype(BF16)
        y = jnp.dot(t_b, xs, preferred_element_type=F32)
        for cc in range(half):
            y_ref[cc, 0, sel, :] = y[:na, cc * LANES:(cc + 1) * LANES]
            y_ref[half + cc, 0, sel, :] = y[na:, cc * LANES:(cc + 1) * LANES]


def _dft2_kernel(y_ref, t_ref, o_ref):
    planes, groups, rows, _ = y_ref.shape
    step = SUBLANES_F32
    half, nb = planes // 2, groups * step

    def tokens_b(cc, j):
        return y_ref[cc, :, j * step:(j + 1) * step, :].reshape(nb, LANES)

    for j in range(rows // step):
        yr = jnp.concatenate([tokens_b(cc, j) for cc in range(half)], axis=1)
        yi = jnp.concatenate([tokens_b(cc, j) for cc in range(half, planes)], axis=1)
        xs = jnp.concatenate([yr, yi], axis=0).astype(BF16)
        out = jnp.dot(t_ref[...], xs, preferred_element_type=F32)
        for cc in range(half):
            o_ref[cc, 0, pl.ds(j, nb, stride=step), :] = out[:, cc * LANES:(cc + 1) * LANES]


def _attn_dft_kernel(*refs, n_total, tq, first_step, stage):
    attn_in, dft_in, (o_ref, d_out) = refs[:12], refs[12:-2], refs[-2:]
    _attn_body(*attn_in, o_ref, n_total=n_total, tq=tq, first_step=first_step)
    (_dft1_kernel if stage == 1 else _dft2_kernel)(*dft_in, d_out)


def _attention_and_dft(q, k, v, kc, vc, sink, z):
    n, m = q.shape[0], kc.shape[0]
    planes, groups, rows, _ = z.shape
    step = SUBLANES_F32
    nb, na = groups * step, rows // step
    tw, f, t2 = _dft_tables(na, nb)
    bias_lo, bias_hi = _band_biases()
    nblk = n // WINDOW
    half = n // 2

    def call(stage, steps, first_row, dft_in, dft_in_specs, d_out_shape, d_out_spec):
        tq = half // steps
        nsub = tq // WINDOW
        first_step = first_row // tq
        main = lambda i, s: (i + first_step, 0)
        prev = lambda i, s: (jnp.maximum((i + first_step) * nsub - 1, 0), 0)
        nxt = lambda i, s: (jnp.minimum((i + first_step) * nsub + nsub, nblk - 1), 0)
        const = lambda i, s: (0, 0)
        kv_specs = [pl.BlockSpec((WINDOW, KV_DIM), prev), pl.BlockSpec((tq, KV_DIM), main),
                    pl.BlockSpec((WINDOW, KV_DIM), nxt)]
        bias_spec = pl.BlockSpec(bias_lo.shape, lambda i, s: (0, 0, 0))
        return pl.pallas_call(
            functools.partial(_attn_dft_kernel, n_total=n, tq=tq, first_step=first_step,
                              stage=stage),
            out_shape=(jax.ShapeDtypeStruct((half, Q_DIM), BF16), d_out_shape),
            grid_spec=pltpu.PrefetchScalarGridSpec(
                num_scalar_prefetch=1,
                grid=(steps,),
                in_specs=[pl.BlockSpec((tq, Q_DIM), main)] + kv_specs + kv_specs
                         + [pl.BlockSpec((m, KV_DIM), const), pl.BlockSpec((m, KV_DIM), const),
                            bias_spec, bias_spec] + dft_in_specs,
                out_specs=(pl.BlockSpec((tq, Q_DIM), lambda i, s: (i, 0)), d_out_spec)),
            compiler_params=_cparams("arbitrary"),
            name=f"attention_half{stage}_dft_stage{stage}",
        )(sink, q, k, k, k, v, v, v, kc, vc, bias_lo, bias_hi, *dft_in)

    ax_first, y = call(
        1, groups, 0, [z, tw, f],
        [pl.BlockSpec((planes, 1, rows, LANES), lambda i, s: (0, i, 0, 0)),
         pl.BlockSpec((2, 1, na, step), lambda i, s: (0, i, 0, 0)),
         pl.BlockSpec((2, na, na), lambda i, s: (0, 0, 0))],
        jax.ShapeDtypeStruct(z.shape, F32),
        pl.BlockSpec((planes, 1, rows, LANES), lambda i, s: (0, i, 0, 0)))
    ax_second, fx = call(
        2, na // step, half, [y, t2],
        [pl.BlockSpec((planes, groups, step * step, LANES), lambda i, s: (0, 0, i, 0)),
         pl.BlockSpec((nb, 2 * nb), lambda i, s: (0, 0))],
        jax.ShapeDtypeStruct((planes // 2, na // step, nb * step, LANES), F32),
        pl.BlockSpec((planes // 2, 1, nb * step, LANES), lambda i, s: (0, i, 0, 0)))
    return ax_first, ax_second, fx


def _outproj_kernel(ax1_ref, ax2_ref, fx_ref, x_ref, wo_ref, gate_ref, nw_ref, sh_ref, sc_ref,
                    wr_ref, x1_ref, g_ref, aff_ref):
    planes, groups, rows, _ = fx_ref.shape
    in_first_half = pl.program_id(0) < pl.num_programs(0) // 2
    na = groups * SUBLANES_F32
    n_kb = rows // SUBLANES_F32
    kb_per_chunk = max(1, n_kb // OUTPROJ_CHUNKS)
    for first_kb in range(0, n_kb, kb_per_chunk):
        rs = slice(first_kb * na, (first_kb + kb_per_chunk) * na)
        fx = jnp.concatenate(
            [jnp.concatenate(
                [fx_ref[cc, :, kb * SUBLANES_F32:(kb + 1) * SUBLANES_F32, :].reshape(na, LANES)
                 for kb in range(first_kb, first_kb + kb_per_chunk)], axis=0)
             for cc in range(planes)], axis=1).astype(BF16)
        ax = jnp.where(in_first_half, ax1_ref[rs, :], ax2_ref[rs, :])
        acc = (jnp.dot(ax, wo_ref[:Q_DIM, :], preferred_element_type=F32)
               + jnp.dot(fx, wo_ref[Q_DIM:, :], preferred_element_type=F32))
        x1 = x_ref[rs, :] + gate_ref[...] * acc
        x1_ref[rs, :] = x1
        gb = _rms_mod(x1, nw_ref[...], sh_ref[...], sc_ref[...]).astype(BF16)
        g_ref[rs, :] = gb
        logits = jnp.dot(gb, wr_ref[...], preferred_element_type=F32)
        e = jnp.exp(logits - jnp.max(logits, axis=1, keepdims=True))
        aff_ref[rs, :] = e / jnp.sum(e, axis=1, keepdims=True)


def _output_projection(ax1, ax2, fx, x, wo_b, gate, nw, shift, scale, wr_b, tm=512):
    n, d = x.shape
    ne = wr_b.shape[1]
    half_steps = n // tm // 2
    row = lambda i: (i, 0)
    const = lambda i: (0, 0)
    vec = pl.BlockSpec((1, d), const)
    return pl.pallas_call(
        _outproj_kernel,
        out_shape=(jax.ShapeDtypeStruct((n, d), F32), jax.ShapeDtypeStruct((n, d), BF16),
                   jax.ShapeDtypeStruct((n, ne), F32)),
        grid=(n // tm,),
        in_specs=[pl.BlockSpec((tm, Q_DIM), lambda i: (jnp.minimum(i, half_steps - 1), 0)),
                  pl.BlockSpec((tm, Q_DIM), lambda i: (jnp.maximum(i - half_steps, 0), 0)),
                  pl.BlockSpec((fx.shape[0], fx.shape[1], tm // (fx.shape[1] * SUBLANES_F32)
                                * SUBLANES_F32, LANES), lambda i: (0, 0, i, 0)),
                  pl.BlockSpec((tm, d), row),
                  pl.BlockSpec((Q_DIM + F_DIM, d), const, pipeline_mode=pl.Buffered(1)),
                  vec, vec, vec, vec, pl.BlockSpec((d, ne), const)],
        out_specs=(pl.BlockSpec((tm, d), row), pl.BlockSpec((tm, d), row),
                   pl.BlockSpec((tm, ne), row)),
        compiler_params=_cparams("arbitrary"),
        name="output_projection",
    )(ax1, ax2, fx, x, wo_b, gate, nw, shift, scale, wr_b)


def _route_kernel(aff_ref, posm_ref, cnt_ref, off_ref, *, cap, n_exp, n_chunk):
    aff_all = aff_ref[...]
    blocks = [aff_all[e * n_chunk:(e + 1) * n_chunk] for e in range(n_exp)]

    def total(mask):
        s = jnp.sum(jnp.where(mask, 1.0, 0.0), axis=0, keepdims=True)
        return jnp.sum(s, axis=1, keepdims=True)

    def search(it, thr_bits):
        bit = jnp.left_shift(jnp.int32(1), 30 - it)
        out = []
        for e in range(n_exp):
            cand = thr_bits[e] | bit
            enough = total(blocks[e] >= lax.bitcast_convert_type(cand, F32)) >= cap
            out.append(jnp.where(enough, cand, thr_bits[e]))
        return tuple(out)

    thr_bits = lax.fori_loop(0, 31, search, tuple(jnp.zeros((1, 1), I32) for _ in range(n_exp)))
    thr = [lax.bitcast_convert_type(t, F32) for t in thr_bits]

    rl = lax.broadcasted_iota(I32, (LANES, LANES), 0)
    cl = lax.broadcasted_iota(I32, (LANES, LANES), 1)
    before = jnp.where(rl < cl, 1.0, 0.0).astype(BF16)
    ones = jnp.ones((LANES, LANES), BF16)
    rc = lax.broadcasted_iota(I32, (n_chunk, n_chunk), 0)
    cc = lax.broadcasted_iota(I32, (n_chunk, n_chunk), 1)
    earlier = jnp.where(cc < rc, 1.0, 0.0).astype(BF16)

    def excl_cumsum(mask):
        mb = jnp.where(mask, 1.0, 0.0).astype(BF16)
        within = jnp.dot(mb, before, preferred_element_type=F32)
        rowtot = jnp.dot(mb, ones, preferred_element_type=F32)
        choff = jnp.dot(earlier, rowtot.astype(BF16), preferred_element_type=F32)
        return within + choff, rowtot, choff

    for e in range(n_exp):
        gt = blocks[e] > thr[e]
        eq = blocks[e] == thr[e]
        need = cap - total(gt)
        tie_rank, _, _ = excl_cumsum(eq)
        sel = gt | (eq & (tie_rank < need))
        pos, rowtot, choff = excl_cumsum(sel)
        rs = slice(e * n_chunk, (e + 1) * n_chunk)
        posm_ref[rs, :] = jnp.where(sel, pos.astype(I32), NOT_SELECTED)
        cnt_ref[rs, :] = rowtot.astype(I32)
        off_ref[rs, :] = choff.astype(I32)


def _route(aff2, cap, n_exp):
    rows = aff2.shape[0]
    n_chunk = rows // n_exp
    spec = pl.BlockSpec((rows, LANES), lambda i: (0, 0))
    shp = jax.ShapeDtypeStruct((rows, LANES), I32)
    return pl.pallas_call(
        functools.partial(_route_kernel, cap=cap, n_exp=n_exp, n_chunk=n_chunk),
        out_shape=(shp, shp, shp),
        grid=(1,),
        in_specs=[spec],
        out_specs=(spec, spec, spec),
        compiler_params=_cparams("arbitrary"),
        name="expert_choice_routing",
    )(aff2)


def _window_matches(posm, firsts, win_starts):
    r = lax.broadcasted_iota(I32, (WINDOW_ROWS, TOK_TILE), 0)
    out = []
    for e, (first, start) in enumerate(zip(firsts, win_starts)):
        pm = posm[e:e + 1, :]
        pm = jnp.where((pm >= first) & (pm < first + PIECE), pm, NOT_SELECTED)
        out.append((pm - start) == r)
    return out


def _align_down(v):
    return (v // ROW_ALIGN) * ROW_ALIGN


def _for_window_size(rows_needed, fn):
    below = None
    for k, size in enumerate(WINDOW_SIZES):
        fits = rows_needed <= size if k + 1 < len(WINDOW_SIZES) else None
        cond = fits if below is None else (below if fits is None else jnp.logical_and(below, fits))
        pl.when(cond)(functools.partial(fn, size))
        below = rows_needed > size


def _gather_kernel(off_ref, cnt_ref, g_ref, posm_ref, aff_ref, xs_hbm,
                   stg, tail, sem, npass_ref, sent_ref, *, n_exp, n_chunk, cap):
    c = pl.program_id(0)
    stride = cap + WINDOW_ROWS

    @pl.when(c == 0)
    def _():
        npass_ref[0] = 0
        tail[...] = jnp.zeros_like(tail)

    def window_copy(slot, e, dst, size=WINDOW_ROWS):
        return pltpu.make_async_copy(stg.at[slot, pl.ds(e * WINDOW_ROWS, size)],
                                     xs_hbm.at[pl.ds(dst, size)], sem.at[slot, e])

    def start_window(slot, e, dst, rows_used):
        _for_window_size(rows_used, lambda size: window_copy(slot, e, dst, size).start())
        sent_ref[slot * n_exp + e] = rows_used

    def wait_window(slot, e):
        _for_window_size(sent_ref[slot * n_exp + e],
                         lambda size: window_copy(slot, e, 0, size).wait())

    @pl.when(c == 0)
    def _():
        stg[0] = jnp.zeros(stg.shape[1:], stg.dtype)
        for e in range(n_exp):
            window_copy(0, e, e * stride + cap).start()
        for e in range(n_exp):
            window_copy(0, e, 0).wait()

    most = cnt_ref[c]
    for e in range(1, n_exp):
        most = jnp.maximum(most, cnt_ref[e * n_chunk + c])
    passes = (most + PIECE - 1) // PIECE

    def one_pass(p, carry):
        done = npass_ref[0]
        slot = done % 2
        posm = posm_ref[0]
        aff = aff_ref[0]
        firsts, win_starts, next_shift, rows_used = [], [], [], []
        for e in range(n_exp):
            o, n_e = off_ref[e * n_chunk + c], cnt_ref[e * n_chunk + c]
            first = o + jnp.minimum(p * PIECE, n_e)
            after = o + jnp.minimum((p + 1) * PIECE, n_e)
            firsts.append(first)
            win_starts.append(_align_down(first))
            next_shift.append(_align_down(after) - _align_down(first))
            rows_used.append(after - _align_down(first))
        matches = _window_matches(posm, firsts, win_starts)
        onehot = jnp.concatenate([jnp.where(mt, 1.0, 0.0).astype(BF16) for mt in matches], axis=0)
        new = jnp.dot(onehot, g_ref[...], preferred_element_type=F32)
        lane = lax.broadcasted_iota(I32, (WINDOW_ROWS, GATE_LANES), 1)
        for e in range(n_exp):
            lo = e * WINDOW_ROWS
            gate = jnp.sum(jnp.where(matches[e], aff[e:e + 1, :], 0.0), axis=1, keepdims=True)
            hi = gate.astype(BF16).astype(F32)
            mid = (gate - hi).astype(BF16).astype(F32)
            low = gate - hi - mid
            parts = jnp.where(lane == 0, hi, jnp.where(lane == 1, mid,
                                                       jnp.where(lane == 2, low, 0.0)))
            rows = jnp.concatenate([new[lo:lo + WINDOW_ROWS], parts], axis=1)
            stg[slot, lo:lo + ROW_ALIGN] = (rows[:ROW_ALIGN] + tail[e].astype(F32)).astype(BF16)
            stg[slot, lo + ROW_ALIGN:lo + WINDOW_ROWS] = rows[ROW_ALIGN:].astype(BF16)

        dsts = [pl.multiple_of(e * stride + win_starts[e], ROW_ALIGN) for e in range(n_exp)]

        @pl.when(done > 0)
        def _():
            for e in range(n_exp):
                wait_window(1 - slot, e)
                start_window(slot, e, dsts[e], rows_used[e])

        @pl.when(done == 0)
        def _():
            for e in range(n_exp):
                start_window(slot, e, dsts[e], rows_used[e])

        for e in range(n_exp):
            src = pl.ds(pl.multiple_of(e * WINDOW_ROWS + next_shift[e], ROW_ALIGN), ROW_ALIGN)
            tail[e] = stg[slot, src, :]
        npass_ref[0] = done + 1
        return carry

    lax.fori_loop(0, passes, one_pass, 0)

    @pl.when(c == pl.num_programs(0) - 1)
    def _():
        done = npass_ref[0]

        @pl.when(done > 0)
        def _():
            for e in range(n_exp):
                wait_window((done - 1) % 2, e)


def _gather(off, cnt, g, posm_t, aff_t, cap):
    n, d = g.shape
    n_chunk, n_exp, _ = posm_t.shape
    rows = n_exp * (cap + WINDOW_ROWS)
    tile3 = lambda i, o, c: (i, 0, 0)
    return pl.pallas_call(
        functools.partial(_gather_kernel, n_exp=n_exp, n_chunk=n_chunk, cap=cap),
        out_shape=jax.ShapeDtypeStruct((rows, d + GATE_LANES), BF16),
        grid_spec=pltpu.PrefetchScalarGridSpec(
            num_scalar_prefetch=2,
            grid=(n_chunk,),
            in_specs=[pl.BlockSpec((TOK_TILE, d), lambda i, o, c: (i, 0)),
                      pl.BlockSpec((1, n_exp, TOK_TILE), tile3),
                      pl.BlockSpec((1, n_exp, TOK_TILE), tile3)],
            out_specs=pl.BlockSpec(memory_space=pl.ANY),
            scratch_shapes=[pltpu.VMEM((2, n_exp * WINDOW_ROWS, d + GATE_LANES), BF16),
                            pltpu.VMEM((n_exp, ROW_ALIGN, d + GATE_LANES), BF16),
                            pltpu.SemaphoreType.DMA((2, n_exp)),
                            pltpu.SMEM((1,), I32),
                            pltpu.SMEM((2 * n_exp,), I32)]),
        compiler_params=_cparams("arbitrary"),
        name="gather_expert_tokens",
    )(off, cnt, g, posm_t, aff_t)


def _expert_kernel(xs_ref, wg_ref, wu_ref, wd_ref, y_ref, h_ref, gate_ref, *, nf, tf, tn):
    j = pl.program_id(1)
    d = wg_ref.shape[1]
    cap = xs_ref.shape[1]
    halves = [slice(r * (cap // EXPERT_ROW_SPLIT), (r + 1) * (cap // EXPERT_ROW_SPLIT))
              for r in range(EXPERT_ROW_SPLIT)]

    @pl.when(j == 0)
    def _():
        parts = xs_ref[0, :, d:].astype(F32)
        gate = parts[:, 0:1] + parts[:, 1:2] + parts[:, 2:3]
        gate_ref[...] = jnp.broadcast_to(gate, gate_ref.shape)

    @pl.when(j < nf)
    def _():
        wg = wg_ref[0].astype(BF16)
        wu = wu_ref[0].astype(BF16)
        for rs in halves:
            xs = xs_ref[0, rs, :d]
            gp = jnp.dot(xs, wg, preferred_element_type=F32)
            up = jnp.dot(xs, wu, preferred_element_type=F32)
            h = ((gp * jax.nn.sigmoid(gp)) * up).astype(BF16)
            for jj in range(nf):
                @pl.when(j == jj)
                def _():
                    h_ref[rs, jj * tf:(jj + 1) * tf] = h

    @pl.when(j >= nf)
    def _():
        wd = wd_ref[0].astype(BF16)
        for rs in halves:
            y = jnp.dot(h_ref[rs, :], wd, preferred_element_type=F32)
            y_ref[0, rs, :] = (y * gate_ref[rs, 0:1]).astype(BF16)


def _experts(xs3, w_gate, w_up, w_down, cap, tf=512, tn=512):
    n_exp, d, ff = w_gate.shape
    tf, tn = min(tf, ff), min(tn, d)
    nf, nn = ff // tf, d // tn

    def ahead(e, j):
        return jnp.minimum(e + (j >= nf).astype(I32), n_exp - 1)

    def ff_tile(e, j):
        return jnp.where(j >= nf, jnp.where(e == n_exp - 1, nf - 1, 0), j)

    return pl.pallas_call(
        functools.partial(_expert_kernel, nf=nf, tf=tf, tn=tn),
        out_shape=jax.ShapeDtypeStruct((n_exp, cap, d), BF16),
        grid=(n_exp, nf + nn),
        in_specs=[pl.BlockSpec((1, cap, d + GATE_LANES), lambda e, j: (ahead(e, j), 0, 0)),
                  pl.BlockSpec((1, d, tf), lambda e, j: (ahead(e, j), 0, ff_tile(e, j))),
                  pl.BlockSpec((1, d, tf), lambda e, j: (ahead(e, j), 0, ff_tile(e, j))),
                  pl.BlockSpec((1, ff, tn), lambda e, j: (e, 0, jnp.maximum(j - nf, 0)))],
        out_specs=pl.BlockSpec((1, cap, tn), lambda e, j: (e, 0, jnp.maximum(j - nf, 0))),
        scratch_shapes=[pltpu.VMEM((cap, ff), BF16), pltpu.VMEM((cap, LANES), F32)],
        compiler_params=_cparams("arbitrary", "arbitrary"),
        name="swiglu_experts",
    )(xs3, w_gate, w_up, w_down)


def _combine_kernel(off_ref, cnt_ref, y_hbm, posm_ref, x1_ref, gate_ref, nw_ref, o_ref,
                    stg, acc_ref, sem, *, n_exp, n_chunk, cap):
    c = pl.program_id(0)
    last_start = n_exp * cap - WINDOW_ROWS

    def firsts_of(tile, p):
        return [off_ref[e * n_chunk + tile] + p * PIECE for e in range(n_exp)]

    def starts_of(tile, p):
        return [pl.multiple_of(jnp.minimum(e * cap + _align_down(first), last_start), ROW_ALIGN)
                for e, first in enumerate(firsts_of(tile, p))]

    def rows_used_of(tile, p):
        out = []
        for e, (first, start) in enumerate(zip(firsts_of(tile, p), starts_of(tile, p))):
            left = jnp.clip(cnt_ref[e * n_chunk + tile] - p * PIECE, 0, PIECE)
            out.append(e * cap + first + left - start)
        return out

    def piece_copy(slot, e, src, size=WINDOW_ROWS):
        return pltpu.make_async_copy(y_hbm.at[pl.ds(src, size)],
                                     stg.at[slot, pl.ds(e * WINDOW_ROWS, size)], sem.at[slot])

    def fetch(tile, p, slot):
        for e, (src, used) in enumerate(zip(starts_of(tile, p), rows_used_of(tile, p))):
            _for_window_size(used, lambda size: piece_copy(slot, e, src, size).start())

    def wait_fetch(tile, p, slot):
        for e, used in enumerate(rows_used_of(tile, p)):
            _for_window_size(used, lambda size: piece_copy(slot, e, 0, size).wait())

    def contribution(p, slot, tokens):
        starts = [s - e * cap for e, s in enumerate(starts_of(c, p))]
        matches = _window_matches(posm_ref[0], firsts_of(c, p), starts)
        onehot = jnp.concatenate([jnp.where(mt[:, tokens], 1.0, 0.0).astype(BF16)
                                  for mt in matches], axis=0)
        return lax.dot_general(onehot, stg[slot], (((0,), (0,)), ((), ())),
                               preferred_element_type=F32)

    def finish(tokens, moe):
        x2 = x1_ref[tokens, :] + gate_ref[...] * moe
        ms = jnp.mean(x2 * x2, axis=-1, keepdims=True)
        o_ref[tokens, :] = x2 * lax.rsqrt(ms + RMS_EPS) * nw_ref[...]

    slot = c % 2

    @pl.when(c == 0)
    def _():
        stg[...] = jnp.zeros_like(stg)
        fetch(0, 0, 0)

    @pl.when(c + 1 < pl.num_programs(0))
    def _():
        fetch(c + 1, 0, 1 - slot)

    most = cnt_ref[c]
    for e in range(1, n_exp):
        most = jnp.maximum(most, cnt_ref[e * n_chunk + c])
    passes = (most + PIECE - 1) // PIECE
    wait_fetch(c, 0, slot)

    @pl.when(passes <= 1)
    def _():
        for t0 in range(0, TOK_TILE, LANES):
            tokens = slice(t0, t0 + LANES)
            finish(tokens, contribution(0, slot, tokens))

    @pl.when(passes > 1)
    def _():
        everyone = slice(0, TOK_TILE)
        acc_ref[...] = contribution(0, slot, everyone)

        def extra_pass(p, carry):
            fetch(c, p, slot)
            wait_fetch(c, p, slot)
            acc_ref[...] += contribution(p, slot, everyone)
            return carry

        lax.fori_loop(1, passes, extra_pass, 0)
        finish(everyone, acc_ref[...])


def _combine(off, cnt, y, posm_t, x1, gate, nw, cap):
    n, d = x1.shape
    n_chunk, n_exp, _ = posm_t.shape
    return pl.pallas_call(
        functools.partial(_combine_kernel, n_exp=n_exp, n_chunk=n_chunk, cap=cap),
        out_shape=jax.ShapeDtypeStruct((n, d), F32),
        grid_spec=pltpu.PrefetchScalarGridSpec(
            num_scalar_prefetch=2,
            grid=(n_chunk,),
            in_specs=[pl.BlockSpec(memory_space=pl.ANY),
                      pl.BlockSpec((1, n_exp, TOK_TILE), lambda i, o, c: (i, 0, 0)),
                      pl.BlockSpec((TOK_TILE, d), lambda i, o, c: (i, 0)),
                      pl.BlockSpec((1, d), lambda i, o, c: (0, 0)),
                      pl.BlockSpec((1, d), lambda i, o, c: (0, 0))],
            out_specs=pl.BlockSpec((TOK_TILE, d), lambda i, o, c: (i, 0)),
            scratch_shapes=[pltpu.VMEM((2, n_exp * WINDOW_ROWS, d), BF16),
                            pltpu.VMEM((TOK_TILE, d), F32),
                            pltpu.SemaphoreType.DMA((2,))]),
        compiler_params=_cparams("arbitrary"),
        name="combine_experts_final_norm",
    )(off, cnt, y, posm_t, x1, gate, nw)


def kernel(x, c, ctx, c_ctx, w_mod, b_mod, norm_mix, w_in, sink, w_fourier, w_out, norm_ffn,
           w_router, w_gate, w_up, w_down, norm_final):
    assert x.shape[0] == 1 and w_mod.shape[0] == 1
    n, d = x.shape[1], x.shape[2]
    xl, cx = x[0], ctx[0]
    n_exp = w_router.shape[2]
    cap = max(1, CAP_FACTOR * n // n_exp)
    n_chunk = n // LANES
    n_tile = n // TOK_TILE
    per_tile = TOK_TILE // LANES

    cvec = jnp.stack([c[0], c_ctx], axis=1)
    b_mod2 = b_mod[0][None, :]
    mod_head, ab = _modulation_head(cvec, w_mod[0], b_mod2, 2 * d, w_fourier[0])
    mx_head = mod_head[0].reshape(2, 1, d)
    mc = mod_head[1].reshape(2, 1, d)

    w_in_b = w_in[0].astype(BF16)
    nmix = norm_mix[0][None, :]
    q, k, v, z, w_out_b, mod_tail, kc, vc = _project_latent(
        xl, nmix, mx_head[0], mx_head[1], w_in_b, ab, w_out[0], cvec, w_mod[0], b_mod2, 2 * d,
        cx, mc[0], mc[1])
    mx = jnp.concatenate([mx_head, mod_tail[0].reshape(4, 1, d)], axis=0)
    ax1, ax2, fx = _attention_and_dft(q, k, v, kc, vc, sink[0], z)

    x1, g, aff = _output_projection(ax1, ax2, fx, xl, w_out_b, mx[2], norm_ffn[0][None, :],
                                    mx[3], mx[4], w_router[0].astype(BF16))

    aff2 = aff.T.reshape(n_exp * n_chunk, LANES)
    posm, cnt, off = _route(aff2, cap, n_exp)
    by_tile = lambda a: a.reshape(n_exp, n_tile, TOK_TILE).transpose(1, 0, 2)
    posm_t, aff_t = by_tile(posm), by_tile(aff2)
    cnt1 = cnt[:, 0].reshape(n_exp * n_tile, per_tile).sum(axis=1)
    off1 = off[:, 0].reshape(n_exp * n_tile, per_tile)[:, 0]

    xs = _gather(off1, cnt1, g, posm_t, aff_t, cap)
    y = _experts(xs.reshape(n_exp, cap + WINDOW_ROWS, d + GATE_LANES),
                 w_gate[0], w_up[0], w_down[0], cap)
    out = _combine(off1, cnt1, y.reshape(n_exp * cap, d), posm_t, x1, mx[5], norm_final[None, :],
                   cap)
    return out[None]
```

```python
import functools
import math

import jax
import jax.numpy as jnp
from jax import lax
from jax.experimental import pallas as pl
from jax.experimental.pallas import tpu as pltpu

F32 = jnp.float32
BF16 = jnp.bfloat16
I32 = jnp.int32

HEAD_DIM = 128
N_HEADS = 8
N_KV_HEADS = 2
GROUP = N_HEADS // N_KV_HEADS
WINDOW = 128
Q_DIM = N_HEADS * HEAD_DIM
KV_DIM = N_KV_HEADS * HEAD_DIM
N_FGROUPS = 4
FG = 256
F_DIM = N_FGROUPS * FG
D_IN = Q_DIM + 2 * KV_DIM + F_DIM
N_EXPERTS = 16
CAP_FACTOR = 2
GRID_W = 64
ROPE_THETA = 10000.0
RMS_EPS = 1e-6
NEG_INF = -1e30
LOG2_E = math.log2(math.e)

LANES = 128
SUBLANES_F32 = 8
VMEM_LIMIT_BYTES = 56 * 1024 * 1024

TOK_TILE = 2 * LANES
PIECE = 48
ROW_ALIGN = 16
WINDOW_ROWS = ROW_ALIGN + PIECE
WINDOW_SIZES = tuple(range(2 * ROW_ALIGN, WINDOW_ROWS + 1, ROW_ALIGN))
GATE_LANES = LANES
DFT_NB = LANES
OUTPROJ_CHUNKS = 2
PROJ_CHUNKS = 2
EXPERT_ROW_SPLIT = 2
NOT_SELECTED = -(1 << 20)


def _cparams(*sem):
    return pltpu.CompilerParams(dimension_semantics=sem, vmem_limit_bytes=VMEM_LIMIT_BYTES)


def _rms_mod(x, nw, shift, scale):
    ms = jnp.mean(x * x, axis=-1, keepdims=True)
    return x * lax.rsqrt(ms + RMS_EPS) * (nw * (1.0 + scale)) + shift


def _mod_kernel(ct_ref, w_ref, b_ref, o_ref):
    ct = ct_ref[...]
    s = ct * jax.nn.sigmoid(ct)
    w = w_ref[...]
    r0 = jnp.sum(s[:, 0:1] * w, axis=0, keepdims=True)
    r1 = jnp.sum(s[:, 1:2] * w, axis=0, keepdims=True)
    o_ref[...] = jnp.concatenate([r0, r1], axis=0) + b_ref[...]


def _ab_kernel(cf_ref, sf_ref, wf_ref, ab_ref):
    wf = wf_ref[0]
    a = jnp.dot(cf_ref[...], wf, preferred_element_type=F32, precision=lax.Precision.HIGHEST)
    b = jnp.dot(sf_ref[...], wf, preferred_element_type=F32, precision=lax.Precision.HIGHEST)
    ab_ref[0] = jnp.concatenate([a, b], axis=1).astype(BF16)


def _mod_fold_kernel(ct_ref, w_ref, b_ref, cf_ref, sf_ref, wf_ref, o_ref, ab_ref):
    _mod_kernel(ct_ref, w_ref, b_ref, o_ref)
    _ab_kernel(cf_ref, sf_ref, wf_ref, ab_ref)


def _modulation_head(ct, w_mod, b_mod, n_cols, w_fourier):
    d = w_mod.shape[0]
    g, fg, _ = w_fourier.shape
    tn = n_cols // g
    idx = jnp.arange(fg, dtype=I32)
    th = ((idx[:, None] * idx[None, :]) % fg).astype(F32) * (2.0 * math.pi / fg)
    scale = 1.0 / math.sqrt(fg)
    cf = jnp.cos(th) * scale
    sf = jnp.sin(th) * scale
    return pl.pallas_call(
        _mod_fold_kernel,
        out_shape=(jax.ShapeDtypeStruct((2, n_cols), F32),
                   jax.ShapeDtypeStruct((g, fg, 2 * fg), BF16)),
        grid=(g,),
        in_specs=[pl.BlockSpec((d, 2), lambda i: (0, 0)),
                  pl.BlockSpec((d, tn), lambda i: (0, i)),
                  pl.BlockSpec((1, tn), lambda i: (0, i)),
                  pl.BlockSpec((fg, fg), lambda i: (0, 0)),
                  pl.BlockSpec((fg, fg), lambda i: (0, 0)),
                  pl.BlockSpec((1, fg, fg), lambda i: (i, 0, 0))],
        out_specs=(pl.BlockSpec((2, tn), lambda i: (0, i)),
                   pl.BlockSpec((1, fg, 2 * fg), lambda i: (i, 0, 0))),
        compiler_params=_cparams("arbitrary"),
        name="modulation_head_fold_channel_dft",
    )(ct, w_mod, b_mod, cf, sf, w_fourier)


def _rope_tables(n):
    quarter = HEAD_DIM // 4
    freqs = ROPE_THETA ** (-jnp.arange(quarter, dtype=F32) / quarter)
    zeros = lambda m: jnp.zeros((m, 2 * quarter), F32)

    def tables(pos, low_half):
        ang = pos[:, None] * freqs[None, :]
        ang = jnp.concatenate([ang, ang], axis=-1)
        cos, sin = jnp.cos(ang), jnp.sin(ang)
        first = (jnp.arange(2 * quarter) < quarter)[None, :]
        parts = (cos, jnp.where(first, -sin, 0.0), jnp.where(first, 0.0, sin))
        pad = zeros(pos.shape[0])
        return jnp.stack([jnp.concatenate([p, pad] if low_half else [pad, p], axis=-1)
                          for p in parts])

    row_tab = tables(jnp.arange(n // GRID_W, dtype=F32), True)
    col_tab = tables(jnp.arange(GRID_W, dtype=F32), False)
    return row_tab, col_tab


def _store_dft_rows(z_ref, plane, val, a0):
    groups = z_ref.shape[1]
    nb = groups * SUBLANES_F32
    for al in range(val.shape[0] // nb):
        z_ref[plane, :, (a0 + al) * SUBLANES_F32:(a0 + al + 1) * SUBLANES_F32, :] = (
            val[al * nb:(al + 1) * nb].reshape(groups, SUBLANES_F32, LANES))


def _proj_kernel(x_ref, nw_ref, sh_ref, sc_ref, w_ref, ab_ref, rt_ref, ct_ref, wo_ref,
                 cvec_ref, wm_ref, bm_ref, cx_ref, csh_ref, csc_ref,
                 q_ref, k_ref, v_ref, z_ref, wo_b_ref, mod_ref, kc_ref, vc_ref):
    wo_b_ref[...] = wo_ref[...].astype(BF16)
    _mod_kernel(cvec_ref, wm_ref, bm_ref, mod_ref)

    @pl.when(pl.program_id(0) == 0)
    def _():
        hc = _rms_mod(cx_ref[...], nw_ref[...], csh_ref[...], csc_ref[...]).astype(BF16)
        kv = jnp.dot(hc, w_ref[:, Q_DIM:Q_DIM + 2 * KV_DIM], preferred_element_type=F32)
        kc_ref[...] = kv[:, :KV_DIM].astype(BF16)
        vc_ref[...] = kv[:, KV_DIM:].astype(BF16)

    quarter = HEAD_DIM // 4
    qscale = LOG2_E / math.sqrt(HEAD_DIM)
    tm = x_ref.shape[0]
    chunk = tm // PROJ_CHUNKS
    for r0 in range(0, tm, chunk):
        rs = slice(r0, r0 + chunk)
        hb = _rms_mod(x_ref[rs, :], nw_ref[...], sh_ref[...], sc_ref[...]).astype(BF16)
        cos, s_up, s_dn = (
            jnp.concatenate(
                [rt_ref[t, r:r + 1, :] + ct_ref[t]
                 for r in range(r0 // GRID_W, (r0 + chunk) // GRID_W)], axis=0)
            for t in range(3))

        def rope(t):
            return (t * cos + pltpu.roll(t, HEAD_DIM - quarter, 1) * s_up
                    + pltpu.roll(t, quarter, 1) * s_dn)

        q = jnp.dot(hb, w_ref[:, :Q_DIM], preferred_element_type=F32)
        for j in range(N_HEADS):
            sl = slice(j * HEAD_DIM, (j + 1) * HEAD_DIM)
            q_ref[rs, sl] = (rope(q[:, sl]) * qscale).astype(BF16)
        k = jnp.dot(hb, w_ref[:, Q_DIM:Q_DIM + KV_DIM], preferred_element_type=F32)
        for j in range(N_KV_HEADS):
            sl = slice(j * HEAD_DIM, (j + 1) * HEAD_DIM)
            k_ref[rs, sl] = rope(k[:, sl]).astype(BF16)
        v = jnp.dot(hb, w_ref[:, Q_DIM + KV_DIM:Q_DIM + 2 * KV_DIM], preferred_element_type=F32)
        v_ref[rs, :] = v.astype(BF16)
        u = jnp.dot(hb, w_ref[:, Q_DIM + 2 * KV_DIM:], preferred_element_type=F32).astype(BF16)
        for g in range(N_FGROUPS):
            pq = jnp.dot(u[:, g * FG:(g + 1) * FG], ab_ref[g], preferred_element_type=F32)
            per_group = FG // LANES
            for t in range(2 * per_group):
                plane = (t // per_group) * (F_DIM // LANES) + g * per_group + t % per_group
                _store_dft_rows(z_ref, plane, pq[:, t * LANES:(t + 1) * LANES], r0 // DFT_NB)


def _project_latent(x, nw, shift, scale, w_in_b, ab, w_out, cvec, w_mod, b_mod, mod_first,
                    ctx, ctx_shift, ctx_scale, tm=512):
    m = ctx.shape[0]
    n, d = x.shape
    steps = n // tm
    wo_rows = w_out.shape[0] // steps
    mod_cols = (w_mod.shape[1] - mod_first) // steps
    mod_blk0 = mod_first // mod_cols
    row_tab, col_tab = _rope_tables(n)
    row = lambda i: (i, 0)
    const2 = lambda i: (0, 0)
    return pl.pallas_call(
        _proj_kernel,
        out_shape=(jax.ShapeDtypeStruct((n, Q_DIM), BF16),
                   jax.ShapeDtypeStruct((n, KV_DIM), BF16),
                   jax.ShapeDtypeStruct((n, KV_DIM), BF16),
                   jax.ShapeDtypeStruct((2 * F_DIM // LANES, DFT_NB // SUBLANES_F32,
                                         (n // DFT_NB) * SUBLANES_F32, LANES), F32),
                   jax.ShapeDtypeStruct(w_out.shape, BF16),
                   jax.ShapeDtypeStruct((2, w_mod.shape[1] - mod_first), F32),
                   jax.ShapeDtypeStruct((m, KV_DIM), BF16),
                   jax.ShapeDtypeStruct((m, KV_DIM), BF16)),
        grid=(n // tm,),
        in_specs=[pl.BlockSpec((tm, d), row),
                  pl.BlockSpec((1, d), const2), pl.BlockSpec((1, d), const2),
                  pl.BlockSpec((1, d), const2),
                  pl.BlockSpec((d, D_IN), const2, pipeline_mode=pl.Buffered(1)),
                  pl.BlockSpec((N_FGROUPS, FG, 2 * FG), lambda i: (0, 0, 0),
                               pipeline_mode=pl.Buffered(1)),
                  pl.BlockSpec((3, tm // GRID_W, HEAD_DIM), lambda i: (0, i, 0)),
                  pl.BlockSpec((3, GRID_W, HEAD_DIM), lambda i: (0, 0, 0)),
                  pl.BlockSpec((wo_rows, w_out.shape[1]), row),
                  pl.BlockSpec(cvec.shape, const2),
                  pl.BlockSpec((w_mod.shape[0], mod_cols), lambda i: (0, mod_blk0 + i)),
                  pl.BlockSpec((1, mod_cols), lambda i: (0, mod_blk0 + i)),
                  pl.BlockSpec((m, d), const2), pl.BlockSpec((1, d), const2),
                  pl.BlockSpec((1, d), const2)],
        out_specs=(pl.BlockSpec((tm, Q_DIM), row), pl.BlockSpec((tm, KV_DIM), row),
                   pl.BlockSpec((tm, KV_DIM), row),
                   pl.BlockSpec((2 * F_DIM // LANES, DFT_NB // SUBLANES_F32,
                                 (tm // DFT_NB) * SUBLANES_F32, LANES), lambda i: (0, 0, i, 0)),
                   pl.BlockSpec((wo_rows, w_out.shape[1]), row),
                   pl.BlockSpec((2, mod_cols), lambda i: (0, i)),
                   pl.BlockSpec((m, KV_DIM), const2), pl.BlockSpec((m, KV_DIM), const2)),
        compiler_params=_cparams("arbitrary"),
        name="project_latent",
    )(x, nw, shift, scale, w_in_b, ab, row_tab, col_tab, w_out, cvec, w_mod, b_mod,
      ctx, ctx_shift, ctx_scale)


def _attn_body(sink_ref, q_ref, kp_ref, km_ref, kn_ref, vp_ref, vm_ref, vn_ref,
               kc_ref, vc_ref, lo_ref, hi_ref, o_ref, *, n_total, tq, first_step):
    i = pl.program_id(0) + first_step
    nsub = tq // WINDOW
    last_blk = n_total // WINDOW - 1
    kwin = jnp.concatenate([kp_ref[...], km_ref[...], kn_ref[...]], axis=0)
    vwin = jnp.concatenate([vp_ref[...], vm_ref[...], vn_ref[...]], axis=0)
    rows = GROUP * WINDOW
    span = 3 * WINDOW
    head_of_row = lax.broadcasted_iota(I32, (rows, 1), 0) // WINDOW
    ones_loc = jnp.ones((span, HEAD_DIM), BF16)
    ones_ctx = jnp.ones((kc_ref.shape[0], HEAD_DIM), BF16)
    nt = (((1,), (1,)), ((), ()))
    for b in range(nsub):
        blk = i * nsub + b
        bias_lo = lo_ref[(blk == 0).astype(I32)]
        bias_hi = hi_ref[(blk == last_blk).astype(I32)]
        for h in range(N_KV_HEADS):
            hs = slice(h * HEAD_DIM, (h + 1) * HEAD_DIM)
            qs = jnp.concatenate(
                [q_ref[b * WINDOW:(b + 1) * WINDOW,
                       (h * GROUP + g) * HEAD_DIM:(h * GROUP + g + 1) * HEAD_DIM]
                 for g in range(GROUP)], axis=0)
            kw = kwin[b * WINDOW:b * WINDOW + span, hs]
            vw = vwin[b * WINDOW:b * WINDOW + span, hs]
            s_loc = lax.dot_general(qs, kw, nt, preferred_element_type=F32)
            s_ctx = lax.dot_general(qs, kc_ref[:, hs], nt, preferred_element_type=F32)
            parts = [s_loc[:, :WINDOW] + bias_lo, s_loc[:, WINDOW:2 * WINDOW],
                     s_loc[:, 2 * WINDOW:] + bias_hi, s_ctx]
            sink_col = jnp.zeros((rows, 1), F32)
            for g in range(GROUP):
                sink_col = jnp.where(head_of_row == g, sink_ref[h * GROUP + g] * LOG2_E, sink_col)
            blocks = parts[:3] + [s_ctx[:, t * WINDOW:(t + 1) * WINDOW]
                                  for t in range(s_ctx.shape[1] // WINDOW)]
            widest = blocks[0]
            for blk_scores in blocks[1:]:
                widest = jnp.maximum(widest, blk_scores)
            m = jnp.maximum(sink_col, jnp.max(widest, axis=1, keepdims=True))
            p = [jnp.exp2(part - m).astype(BF16) for part in parts]
            ov = (jnp.dot(jnp.concatenate(p[:3], axis=1), jnp.concatenate([vw, ones_loc], axis=1),
                          preferred_element_type=F32)
                  + jnp.dot(p[3], jnp.concatenate([vc_ref[:, hs], ones_ctx], axis=1),
                            preferred_element_type=F32))
            o = ov[:, :HEAD_DIM] / (ov[:, HEAD_DIM:] + jnp.exp2(sink_col - m))
            for g in range(GROUP):
                o_ref[b * WINDOW:(b + 1) * WINDOW,
                      (h * GROUP + g) * HEAD_DIM:(h * GROUP + g + 1) * HEAD_DIM] = (
                    o[g * WINDOW:(g + 1) * WINDOW].astype(BF16))


def _band_biases():
    r = jnp.arange(GROUP * WINDOW, dtype=I32)[:, None] % WINDOW
    c = jnp.arange(WINDOW, dtype=I32)[None, :]
    masked = jnp.full((GROUP * WINDOW, WINDOW), NEG_INF, F32)
    lo = jnp.stack([jnp.where(c >= r, 0.0, NEG_INF).astype(F32), masked])
    hi = jnp.stack([jnp.where(c <= r, 0.0, NEG_INF).astype(F32), masked])
    return lo, hi


def _dft_tables(na, nb):
    n = na * nb
    s1 = 2.0 ** (-(int(math.log2(na)) // 2))
    s2 = (1.0 / math.sqrt(n)) / s1
    ka = jnp.arange(na, dtype=I32)
    th_tw = ((jnp.arange(nb, dtype=I32)[:, None] * ka[None, :]) % n).astype(F32) * (2.0 * math.pi / n)
    th_f = ((ka[:, None] * ka[None, :]) % na).astype(F32) * (2.0 * math.pi / na)
    tw = jnp.stack([jnp.cos(th_tw) * s1, -jnp.sin(th_tw) * s1])
    tw = tw.reshape(2, nb // SUBLANES_F32, SUBLANES_F32, na).transpose(0, 1, 3, 2)
    f = jnp.stack([jnp.cos(th_f), -jnp.sin(th_f)])
    kb = jnp.arange(nb, dtype=I32)
    th2 = ((kb[:, None] * kb[None, :]) % nb).astype(F32) * (2.0 * math.pi / nb)
    t2 = (jnp.concatenate([jnp.cos(th2), jnp.sin(th2)], axis=-1) * s2).astype(BF16)
    return tw, f, t2


def _dft1_kernel(z_ref, tw_ref, f_ref, y_ref):
    planes, _, rows, _ = z_ref.shape
    step = SUBLANES_F32
    half, na = planes // 2, rows // step
    fr, fi = f_ref[0], f_ref[1]
    for j in range(step):
        twr, twi = tw_ref[0, 0][:, j:j + 1], tw_ref[1, 0][:, j:j + 1]
        tr, ti = twr * fr - twi * fi, twr * fi + twi * fr
        t_b = jnp.concatenate([jnp.concatenate([tr, ti], axis=1),
                               jnp.concatenate([ti, -tr], axis=1)], axis=0).astype(BF16)
        sel = pl.ds(j, na, stride=step)
        p = jnp.concatenate([z_ref[cc, 0, sel, :] for cc in range(half)], axis=1)
        q = jnp.concatenate([z_ref[cc, 0, sel, :] for cc in range(half, planes)], axis=1)
        xs = jnp.concatenate([p, q], axis=0).astype(BF16)
        y = jnp.dot(t_b, xs, preferred_element_type=F32)
        for cc in range(half):
            y_ref[cc, 0, sel, :] = y[:na, cc * LANES:(cc + 1) * LANES]
            y_ref[half + cc, 0, sel, :] = y[na:, cc * LANES:(cc + 1) * LANES]


def _dft2_kernel(y_ref, t_ref, o_ref):
    planes, groups, rows, _ = y_ref.shape
    step = SUBLANES_F32
    half, nb = planes // 2, groups * step

    def tokens_b(cc, j):
        return y_ref[cc, :, j * step:(j + 1) * step, :].reshape(nb, LANES)

    for j in range(rows // step):
        yr = jnp.concatenate([tokens_b(cc, j) for cc in range(half)], axis=1)
        yi = jnp.concatenate([tokens_b(cc, j) for cc in range(half, planes)], axis=1)
        xs = jnp.concatenate([yr, yi], axis=0).astype(BF16)
        out = jnp.dot(t_ref[...], xs, preferred_element_type=F32)
        for cc in range(half):
            o_ref[cc, 0, pl.ds(j, nb, stride=step), :] = out[:, cc * LANES:(cc + 1) * LANES]


def _attn_dft_kernel(*refs, n_total, tq, first_step, stage):
    attn_in, dft_in, (o_ref, d_out) = refs[:12], refs[12:-2], refs[-2:]
    _attn_body(*attn_in, o_ref, n_total=n_total, tq=tq, first_step=first_step)
    (_dft1_kernel if stage == 1 else _dft2_kernel)(*dft_in, d_out)


def _attention_and_dft(q, k, v, kc, vc, sink, z):
    n, m = q.shape[0], kc.shape[0]
    planes, groups, rows, _ = z.shape
    step = SUBLANES_F32
    nb, na = groups * step, rows // step
    tw, f, t2 = _dft_tables(na, nb)
    bias_lo, bias_hi = _band_biases()
    nblk = n // WINDOW
    half = n // 2

    def call(stage, steps, first_row, dft_in, dft_in_specs, d_out_shape, d_out_spec):
        tq = half // steps
        nsub = tq // WINDOW
        first_step = first_row // tq
        main = lambda i, s: (i + first_step, 0)
        prev = lambda i, s: (jnp.maximum((i + first_step) * nsub - 1, 0), 0)
        nxt = lambda i, s: (jnp.minimum((i + first_step) * nsub + nsub, nblk - 1), 0)
        const = lambda i, s: (0, 0)
        kv_specs = [pl.BlockSpec((WINDOW, KV_DIM), prev), pl.BlockSpec((tq, KV_DIM), main),
                    pl.BlockSpec((WINDOW, KV_DIM), nxt)]
        bias_spec = pl.BlockSpec(bias_lo.shape, lambda i, s: (0, 0, 0))
        return pl.pallas_call(
            functools.partial(_attn_dft_kernel, n_total=n, tq=tq, first_step=first_step,
                              stage=stage),
            out_shape=(jax.ShapeDtypeStruct((half, Q_DIM), BF16), d_out_shape),
            grid_spec=pltpu.PrefetchScalarGridSpec(
                num_scalar_prefetch=1,
                grid=(steps,),
                in_specs=[pl.BlockSpec((tq, Q_DIM), main)] + kv_specs + kv_specs
                         + [pl.BlockSpec((m, KV_DIM), const), pl.BlockSpec((m, KV_DIM), const),
                            bias_spec, bias_spec] + dft_in_specs,
                out_specs=(pl.BlockSpec((tq, Q_DIM), lambda i, s: (i, 0)), d_out_spec)),
            compiler_params=_cparams("arbitrary"),
            name=f"attention_half{stage}_dft_stage{stage}",
        )(sink, q, k, k, k, v, v, v, kc, vc, bias_lo, bias_hi, *dft_in)

    ax_first, y = call(
        1, groups, 0, [z, tw, f],
        [pl.BlockSpec((planes, 1, rows, LANES), lambda i, s: (0, i, 0, 0)),
         pl.BlockSpec((2, 1, na, step), lambda i, s: (0, i, 0, 0)),
         pl.BlockSpec((2, na, na), lambda i, s: (0, 0, 0))],
        jax.ShapeDtypeStruct(z.shape, F32),
        pl.BlockSpec((planes, 1, rows, LANES), lambda i, s: (0, i, 0, 0)))
    ax_second, fx = call(
        2, na // step, half, [y, t2],
        [pl.BlockSpec((planes, groups, step * step, LANES), lambda i, s: (0, 0, i, 0)),
         pl.BlockSpec((nb, 2 * nb), lambda i, s: (0, 0))],
        jax.ShapeDtypeStruct((planes // 2, na // step, nb * step, LANES), F32),
        pl.BlockSpec((planes // 2, 1, nb * step, LANES), lambda i, s: (0, i, 0, 0)))
    return ax_first, ax_second, fx


def _outproj_kernel(ax1_ref, ax2_ref, fx_ref, x_ref, wo_ref, gate_ref, nw_ref, sh_ref, sc_ref,
                    wr_ref, x1_ref, g_ref, aff_ref):
    planes, groups, rows, _ = fx_ref.shape
    in_first_half = pl.program_id(0) < pl.num_programs(0) // 2
    na = groups * SUBLANES_F32
    n_kb = rows // SUBLANES_F32
    kb_per_chunk = max(1, n_kb // OUTPROJ_CHUNKS)
    for first_kb in range(0, n_kb, kb_per_chunk):
        rs = slice(first_kb * na, (first_kb + kb_per_chunk) * na)
        fx = jnp.concatenate(
            [jnp.concatenate(
                [fx_ref[cc, :, kb * SUBLANES_F32:(kb + 1) * SUBLANES_F32, :].reshape(na, LANES)
                 for kb in range(first_kb, first_kb + kb_per_chunk)], axis=0)
             for cc in range(planes)], axis=1).astype(BF16)
        ax = jnp.where(in_first_half, ax1_ref[rs, :], ax2_ref[rs, :])
        acc = (jnp.dot(ax, wo_ref[:Q_DIM, :], preferred_element_type=F32)
               + jnp.dot(fx, wo_ref[Q_DIM:, :], preferred_element_type=F32))
        x1 = x_ref[rs, :] + gate_ref[...] * acc
        x1_ref[rs, :] = x1
        gb = _rms_mod(x1, nw_ref[...], sh_ref[...], sc_ref[...]).astype(BF16)
        g_ref[rs, :] = gb
        logits = jnp.dot(gb, wr_ref[...], preferred_element_type=F32)
        e = jnp.exp(logits - jnp.max(logits, axis=1, keepdims=True))
        aff_ref[rs, :] = e / jnp.sum(e, axis=1, keepdims=True)


def _output_projection(ax1, ax2, fx, x, wo_b, gate, nw, shift, scale, wr_b, tm=512):
    n, d = x.shape
    ne = wr_b.shape[1]
    half_steps = n // tm // 2
    row = lambda i: (i, 0)
    const = lambda i: (0, 0)
    vec = pl.BlockSpec((1, d), const)
    return pl.pallas_call(
        _outproj_kernel,
        out_shape=(jax.ShapeDtypeStruct((n, d), F32), jax.ShapeDtypeStruct((n, d), BF16),
                   jax.ShapeDtypeStruct((n, ne), F32)),
        grid=(n // tm,),
        in_specs=[pl.BlockSpec((tm, Q_DIM), lambda i: (jnp.minimum(i, half_steps - 1), 0)),
                  pl.BlockSpec((tm, Q_DIM), lambda i: (jnp.maximum(i - half_steps, 0), 0)),
                  pl.BlockSpec((fx.shape[0], fx.shape[1], tm // (fx.shape[1] * SUBLANES_F32)
                                * SUBLANES_F32, LANES), lambda i: (0, 0, i, 0)),
                  pl.BlockSpec((tm, d), row),
                  pl.BlockSpec((Q_DIM + F_DIM, d), const, pipeline_mode=pl.Buffered(1)),
                  vec, vec, vec, vec, pl.BlockSpec((d, ne), const)],
        out_specs=(pl.BlockSpec((tm, d), row), pl.BlockSpec((tm, d), row),
                   pl.BlockSpec((tm, ne), row)),
        compiler_params=_cparams("arbitrary"),
        name="output_projection",
    )(ax1, ax2, fx, x, wo_b, gate, nw, shift, scale, wr_b)


def _route_kernel(aff_ref, posm_ref, cnt_ref, off_ref, *, cap, n_exp, n_chunk):
    aff_all = aff_ref[...]
    blocks = [aff_all[e * n_chunk:(e + 1) * n_chunk] for e in range(n_exp)]

    def total(mask):
        s = jnp.sum(jnp.where(mask, 1.0, 0.0), axis=0, keepdims=True)
        return jnp.sum(s, axis=1, keepdims=True)

    def search(it, thr_bits):
        bit = jnp.left_shift(jnp.int32(1), 30 - it)
        out = []
        for e in range(n_exp):
            cand = thr_bits[e] | bit
            enough = total(blocks[e] >= lax.bitcast_convert_type(cand, F32)) >= cap
            out.append(jnp.where(enough, cand, thr_bits[e]))
        return tuple(out)

    thr_bits = lax.fori_loop(0, 31, search, tuple(jnp.zeros((1, 1), I32) for _ in range(n_exp)))
    thr = [lax.bitcast_convert_type(t, F32) for t in thr_bits]

    rl = lax.broadcasted_iota(I32, (LANES, LANES), 0)
    cl = lax.broadcasted_iota(I32, (LANES, LANES), 1)
    before = jnp.where(rl < cl, 1.0, 0.0).astype(BF16)
    ones = jnp.ones((LANES, LANES), BF16)
    rc = lax.broadcasted_iota(I32, (n_chunk, n_chunk), 0)
    cc = lax.broadcasted_iota(I32, (n_chunk, n_chunk), 1)
    earlier = jnp.where(cc < rc, 1.0, 0.0).astype(BF16)

    def excl_cumsum(mask):
        mb = jnp.where(mask, 1.0, 0.0).astype(BF16)
        within = jnp.dot(mb, before, preferred_element_type=F32)
        rowtot = jnp.dot(mb, ones, preferred_element_type=F32)
        choff = jnp.dot(earlier, rowtot.astype(BF16), preferred_element_type=F32)
        return within + choff, rowtot, choff

    for e in range(n_exp):
        gt = blocks[e] > thr[e]
        eq = blocks[e] == thr[e]
        need = cap - total(gt)
        tie_rank, _, _ = excl_cumsum(eq)
        sel = gt | (eq & (tie_rank < need))
        pos, rowtot, choff = excl_cumsum(sel)
        rs = slice(e * n_chunk, (e + 1) * n_chunk)
        posm_ref[rs, :] = jnp.where(sel, pos.astype(I32), NOT_SELECTED)
        cnt_ref[rs, :] = rowtot.astype(I32)
        off_ref[rs, :] = choff.astype(I32)


def _route(aff2, cap, n_exp):
    rows = aff2.shape[0]
    n_chunk = rows // n_exp
    spec = pl.BlockSpec((rows, LANES), lambda i: (0, 0))
    shp = jax.ShapeDtypeStruct((rows, LANES), I32)
    return pl.pallas_call(
        functools.partial(_route_kernel, cap=cap, n_exp=n_exp, n_chunk=n_chunk),
        out_shape=(shp, shp, shp),
        grid=(1,),
        in_specs=[spec],
        out_specs=(spec, spec, spec),
        compiler_params=_cparams("arbitrary"),
        name="expert_choice_routing",
    )(aff2)


def _window_matches(posm, firsts, win_starts):
    r = lax.broadcasted_iota(I32, (WINDOW_ROWS, TOK_TILE), 0)
    out = []
    for e, (first, start) in enumerate(zip(firsts, win_starts)):
        pm = posm[e:e + 1, :]
        pm = jnp.where((pm >= first) & (pm < first + PIECE), pm, NOT_SELECTED)
        out.append((pm - start) == r)
    return out


def _align_down(v):
    return (v // ROW_ALIGN) * ROW_ALIGN


def _for_window_size(rows_needed, fn):
    below = None
    for k, size in enumerate(WINDOW_SIZES):
        fits = rows_needed <= size if k + 1 < len(WINDOW_SIZES) else None
        cond = fits if below is None else (below if fits is None else jnp.logical_and(below, fits))
        pl.when(cond)(functools.partial(fn, size))
        below = rows_needed > size


def _gather_kernel(off_ref, cnt_ref, g_ref, posm_ref, aff_ref, xs_hbm,
                   stg, tail, sem, npass_ref, sent_ref, *, n_exp, n_chunk, cap):
    c = pl.program_id(0)
    stride = cap + WINDOW_ROWS

    @pl.when(c == 0)
    def _():
        npass_ref[0] = 0
        tail[...] = jnp.zeros_like(tail)

    def window_copy(slot, e, dst, size=WINDOW_ROWS):
        return pltpu.make_async_copy(stg.at[slot, pl.ds(e * WINDOW_ROWS, size)],
                                     xs_hbm.at[pl.ds(dst, size)], sem.at[slot, e])

    def start_window(slot, e, dst, rows_used):
        _for_window_size(rows_used, lambda size: window_copy(slot, e, dst, size).start())
        sent_ref[slot * n_exp + e] = rows_used

    def wait_window(slot, e):
        _for_window_size(sent_ref[slot * n_exp + e],
                         lambda size: window_copy(slot, e, 0, size).wait())

    @pl.when(c == 0)
    def _():
        stg[0] = jnp.zeros(stg.shape[1:], stg.dtype)
        for e in range(n_exp):
            window_copy(0, e, e * stride + cap).start()
        for e in range(n_exp):
            window_copy(0, e, 0).wait()

    most = cnt_ref[c]
    for e in range(1, n_exp):
        most = jnp.maximum(most, cnt_ref[e * n_chunk + c])
    passes = (most + PIECE - 1) // PIECE

    def one_pass(p, carry):
        done = npass_ref[0]
        slot = done % 2
        posm = posm_ref[0]
        aff = aff_ref[0]
        firsts, win_starts, next_shift, rows_used = [], [], [], []
        for e in range(n_exp):
            o, n_e = off_ref[e * n_chunk + c], cnt_ref[e * n_chunk + c]
            first = o + jnp.minimum(p * PIECE, n_e)
            after = o + jnp.minimum((p + 1) * PIECE, n_e)
            firsts.append(first)
            win_starts.append(_align_down(first))
            next_shift.append(_align_down(after) - _align_down(first))
            rows_used.append(after - _align_down(first))
        matches = _window_matches(posm, firsts, win_starts)
        onehot = jnp.concatenate([jnp.where(mt, 1.0, 0.0).astype(BF16) for mt in matches], axis=0)
        new = jnp.dot(onehot, g_ref[...], preferred_element_type=F32)
        lane = lax.broadcasted_iota(I32, (WINDOW_ROWS, GATE_LANES), 1)
        for e in range(n_exp):
            lo = e * WINDOW_ROWS
            gate = jnp.sum(jnp.where(matches[e], aff[e:e + 1, :], 0.0), axis=1, keepdims=True)
            hi = gate.astype(BF16).astype(F32)
            mid = (gate - hi).astype(BF16).astype(F32)
            low = gate - hi - mid
            parts = jnp.where(lane == 0, hi, jnp.where(lane == 1, mid,
                                                       jnp.where(lane == 2, low, 0.0)))
            rows = jnp.concatenate([new[lo:lo + WINDOW_ROWS], parts], axis=1)
            stg[slot, lo:lo + ROW_ALIGN] = (rows[:ROW_ALIGN] + tail[e].astype(F32)).astype(BF16)
            stg[slot, lo + ROW_ALIGN:lo + WINDOW_ROWS] = rows[ROW_ALIGN:].astype(BF16)

        dsts = [pl.multiple_of(e * stride + win_starts[e], ROW_ALIGN) for e in range(n_exp)]

        @pl.when(done > 0)
        def _():
            for e in range(n_exp):
                wait_window(1 - slot, e)
                start_window(slot, e, dsts[e], rows_used[e])

        @pl.when(done == 0)
        def _():
            for e in range(n_exp):
                start_window(slot, e, dsts[e], rows_used[e])

        for e in range(n_exp):
            src = pl.ds(pl.multiple_of(e * WINDOW_ROWS + next_shift[e], ROW_ALIGN), ROW_ALIGN)
            tail[e] = stg[slot, src, :]
        npass_ref[0] = done + 1
        return carry

    lax.fori_loop(0, passes, one_pass, 0)

    @pl.when(c == pl.num_programs(0) - 1)
    def _():
        done = npass_ref[0]

        @pl.when(done > 0)
        def _():
            for e in range(n_exp):
                wait_window((done - 1) % 2, e)


def _gather(off, cnt, g, posm_t, aff_t, cap):
    n, d = g.shape
    n_chunk, n_exp, _ = posm_t.shape
    rows = n_exp * (cap + WINDOW_ROWS)
    tile3 = lambda i, o, c: (i, 0, 0)
    return pl.pallas_call(
        functools.partial(_gather_kernel, n_exp=n_exp, n_chunk=n_chunk, cap=cap),
        out_shape=jax.ShapeDtypeStruct((rows, d + GATE_LANES), BF16),
        grid_spec=pltpu.PrefetchScalarGridSpec(
            num_scalar_prefetch=2,
            grid=(n_chunk,),
            in_specs=[pl.BlockSpec((TOK_TILE, d), lambda i, o, c: (i, 0)),
                      pl.BlockSpec((1, n_exp, TOK_TILE), tile3),
                      pl.BlockSpec((1, n_exp, TOK_TILE), tile3)],
            out_specs=pl.BlockSpec(memory_space=pl.ANY),
            scratch_shapes=[pltpu.VMEM((2, n_exp * WINDOW_ROWS, d + GATE_LANES), BF16),
                            pltpu.VMEM((n_exp, ROW_ALIGN, d + GATE_LANES), BF16),
                            pltpu.SemaphoreType.DMA((2, n_exp)),
                            pltpu.SMEM((1,), I32),
                            pltpu.SMEM((2 * n_exp,), I32)]),
        compiler_params=_cparams("arbitrary"),
        name="gather_expert_tokens",
    )(off, cnt, g, posm_t, aff_t)


def _expert_kernel(xs_hbm, wg_ref, wu_ref, wd_ref, y_ref, xs_buf, h_ref, gate_ref, sem,
                   *, nf, tf, tn):
    e, j = pl.program_id(0), pl.program_id(1)
    d = wg_ref.shape[1]
    cap = xs_buf.shape[0]
    halves = [slice(r * (cap // EXPERT_ROW_SPLIT), (r + 1) * (cap // EXPERT_ROW_SPLIT))
              for r in range(EXPERT_ROW_SPLIT)]

    def rows_copy(expert):
        return pltpu.make_async_copy(xs_hbm.at[expert, pl.ds(0, cap)], xs_buf, sem.at[0])

    @pl.when((e == 0) & (j == 0))
    def _():
        rows_copy(0).start()

    @pl.when(j == 0)
    def _():
        rows_copy(e).wait()
        parts = xs_buf[:, d:].astype(F32)
        gate = parts[:, 0:1] + parts[:, 1:2] + parts[:, 2:3]
        gate_ref[...] = jnp.broadcast_to(gate, gate_ref.shape)

    @pl.when((j == nf) & (e + 1 < pl.num_programs(0)))
    def _():
        rows_copy(e + 1).start()

    @pl.when(j < nf)
    def _():
        wg = wg_ref[0].astype(BF16)
        wu = wu_ref[0].astype(BF16)
        for rs in halves:
            xs = xs_buf[rs, :d]
            gp = jnp.dot(xs, wg, preferred_element_type=F32)
            up = jnp.dot(xs, wu, preferred_element_type=F32)
            h = ((gp * jax.nn.sigmoid(gp)) * up).astype(BF16)
            for jj in range(nf):
                @pl.when(j == jj)
                def _():
                    h_ref[rs, jj * tf:(jj + 1) * tf] = h

    @pl.when(j >= nf)
    def _():
        wd = wd_ref[0].astype(BF16)
        for rs in halves:
            y = jnp.dot(h_ref[rs, :], wd, preferred_element_type=F32)
            y_ref[0, rs, :] = (y * gate_ref[rs, 0:1]).astype(BF16)


def _experts(xs3, w_gate, w_up, w_down, cap, tf=512, tn=1024):
    n_exp, d, ff = w_gate.shape
    tf, tn = min(tf, ff), min(tn, d)
    nf, nn = ff // tf, d // tn

    def ahead(e, j):
        return jnp.minimum(e + (j >= nf).astype(I32), n_exp - 1)

    def ff_tile(e, j):
        return jnp.where(j >= nf, jnp.where(e == n_exp - 1, nf - 1, 0), j)

    return pl.pallas_call(
        functools.partial(_expert_kernel, nf=nf, tf=tf, tn=tn),
        out_shape=jax.ShapeDtypeStruct((n_exp, cap, d), BF16),
        grid=(n_exp, nf + nn),
        in_specs=[pl.BlockSpec(memory_space=pl.ANY),
                  pl.BlockSpec((1, d, tf), lambda e, j: (ahead(e, j), 0, ff_tile(e, j))),
                  pl.BlockSpec((1, d, tf), lambda e, j: (ahead(e, j), 0, ff_tile(e, j))),
                  pl.BlockSpec((1, ff, tn), lambda e, j: (e, 0, jnp.maximum(j - nf, 0)))],
        out_specs=pl.BlockSpec((1, cap, tn), lambda e, j: (e, 0, jnp.maximum(j - nf, 0))),
        scratch_shapes=[pltpu.VMEM((cap, d + GATE_LANES), BF16), pltpu.VMEM((cap, ff), BF16),
                        pltpu.VMEM((cap, LANES), F32), pltpu.SemaphoreType.DMA((1,))],
        compiler_params=_cparams("arbitrary", "arbitrary"),
        name="swiglu_experts",
    )(xs3, w_gate, w_up, w_down)


def _combine_kernel(off_ref, cnt_ref, y_hbm, posm_ref, x1_ref, gate_ref, nw_ref, o_ref,
                    stg, acc_ref, sem, *, n_exp, n_chunk, cap):
    c = pl.program_id(0)
    last_start = n_exp * cap - WINDOW_ROWS

    def firsts_of(tile, p):
        return [off_ref[e * n_chunk + tile] + p * PIECE for e in range(n_exp)]

    def starts_of(tile, p):
        return [pl.multiple_of(jnp.minimum(e * cap + _align_down(first), last_start), ROW_ALIGN)
                for e, first in enumerate(firsts_of(tile, p))]

    def rows_used_of(tile, p):
        out = []
        for e, (first, start) in enumerate(zip(firsts_of(tile, p), starts_of(tile, p))):
            left = jnp.clip(cnt_ref[e * n_chunk + tile] - p * PIECE, 0, PIECE)
            out.append(e * cap + first + left - start)
        return out

    def piece_copy(slot, e, src, size=WINDOW_ROWS):
        return pltpu.make_async_copy(y_hbm.at[pl.ds(src, size)],
                                     stg.at[slot, pl.ds(e * WINDOW_ROWS, size)], sem.at[slot])

    def fetch(tile, p, slot):
        for e, (src, used) in enumerate(zip(starts_of(tile, p), rows_used_of(tile, p))):
            _for_window_size(used, lambda size: piece_copy(slot, e, src, size).start())

    def wait_fetch(tile, p, slot):
        for e, used in enumerate(rows_used_of(tile, p)):
            _for_window_size(used, lambda size: piece_copy(slot, e, 0, size).wait())

    def contribution(p, slot, tokens):
        starts = [s - e * cap for e, s in enumerate(starts_of(c, p))]
        matches = _window_matches(posm_ref[0], firsts_of(c, p), starts)
        onehot = jnp.concatenate([jnp.where(mt[:, tokens], 1.0, 0.0).astype(BF16)
                                  for mt in matches], axis=0)
        return lax.dot_general(onehot, stg[slot], (((0,), (0,)), ((), ())),
                               preferred_element_type=F32)

    def finish(tokens, moe):
        x2 = x1_ref[tokens, :] + gate_ref[...] * moe
        ms = jnp.mean(x2 * x2, axis=-1, keepdims=True)
        o_ref[tokens, :] = x2 * lax.rsqrt(ms + RMS_EPS) * nw_ref[...]

    slot = c % 2

    @pl.when(c == 0)
    def _():
        stg[...] = jnp.zeros_like(stg)
        fetch(0, 0, 0)

    @pl.when(c + 1 < pl.num_programs(0))
    def _():
        fetch(c + 1, 0, 1 - slot)

    most = cnt_ref[c]
    for e in range(1, n_exp):
        most = jnp.maximum(most, cnt_ref[e * n_chunk + c])
    passes = (most + PIECE - 1) // PIECE
    wait_fetch(c, 0, slot)

    @pl.when(passes <= 1)
    def _():
        for t0 in range(0, TOK_TILE, LANES):
            tokens = slice(t0, t0 + LANES)
            finish(tokens, contribution(0, slot, tokens))

    @pl.when(passes > 1)
    def _():
        everyone = slice(0, TOK_TILE)
        acc_ref[...] = contribution(0, slot, everyone)

        def extra_pass(p, carry):
            fetch(c, p, slot)
            wait_fetch(c, p, slot)
            acc_ref[...] += contribution(p, slot, everyone)
            return carry

        lax.fori_loop(1, passes, extra_pass, 0)
        finish(everyone, acc_ref[...])


def _combine(off, cnt, y, posm_t, x1, gate, nw, cap):
    n, d = x1.shape
    n_chunk, n_exp, _ = posm_t.shape
    return pl.pallas_call(
        functools.partial(_combine_kernel, n_exp=n_exp, n_chunk=n_chunk, cap=cap),
        out_shape=jax.ShapeDtypeStruct((n, d), F32),
        grid_spec=pltpu.PrefetchScalarGridSpec(
            num_scalar_prefetch=2,
            grid=(n_chunk,),
            in_specs=[pl.BlockSpec(memory_space=pl.ANY),
                      pl.BlockSpec((1, n_exp, TOK_TILE), lambda i, o, c: (i, 0, 0)),
                      pl.BlockSpec((TOK_TILE, d), lambda i, o, c: (i, 0)),
                      pl.BlockSpec((1, d), lambda i, o, c: (0, 0)),
                      pl.BlockSpec((1, d), lambda i, o, c: (0, 0))],
            out_specs=pl.BlockSpec((TOK_TILE, d), lambda i, o, c: (i, 0)),
            scratch_shapes=[pltpu.VMEM((2, n_exp * WINDOW_ROWS, d), BF16),
                            pltpu.VMEM((TOK_TILE, d), F32),
                            pltpu.SemaphoreType.DMA((2,))]),
        compiler_params=_cparams("arbitrary"),
        name="combine_experts_final_norm",
    )(off, cnt, y, posm_t, x1, gate, nw)


def kernel(x, c, ctx, c_ctx, w_mod, b_mod, norm_mix, w_in, sink, w_fourier, w_out, norm_ffn,
           w_router, w_gate, w_up, w_down, norm_final):
    assert x.shape[0] == 1 and w_mod.shape[0] == 1
    n, d = x.shape[1], x.shape[2]
    xl, cx = x[0], ctx[0]
    n_exp = w_router.shape[2]
    cap = max(1, CAP_FACTOR * n // n_exp)
    n_chunk = n // LANES
    n_tile = n // TOK_TILE
    per_tile = TOK_TILE // LANES

    cvec = jnp.stack([c[0], c_ctx], axis=1)
    b_mod2 = b_mod[0][None, :]
    mod_head, ab = _modulation_head(cvec, w_mod[0], b_mod2, 2 * d, w_fourier[0])
    mx_head = mod_head[0].reshape(2, 1, d)
    mc = mod_head[1].reshape(2, 1, d)

    w_in_b = w_in[0].astype(BF16)
    nmix = norm_mix[0][None, :]
    q, k, v, z, w_out_b, mod_tail, kc, vc = _project_latent(
        xl, nmix, mx_head[0], mx_head[1], w_in_b, ab, w_out[0], cvec, w_mod[0], b_mod2, 2 * d,
        cx, mc[0], mc[1])
    mx = jnp.concatenate([mx_head, mod_tail[0].reshape(4, 1, d)], axis=0)
    ax1, ax2, fx = _attention_and_dft(q, k, v, kc, vc, sink[0], z)

    x1, g, aff = _output_projection(ax1, ax2, fx, xl, w_out_b, mx[2], norm_ffn[0][None, :],
                                    mx[3], mx[4], w_router[0].astype(BF16))

    aff2 = aff.T.reshape(n_exp * n_chunk, LANES)
    posm, cnt, off = _route(aff2, cap, n_exp)
    by_tile = lambda a: a.reshape(n_exp, n_tile, TOK_TILE).transpose(1, 0, 2)
    posm_t, aff_t = by_tile(posm), by_tile(aff2)
    cnt1 = cnt[:, 0].reshape(n_exp * n_tile, per_tile).sum(axis=1)
    off1 = off[:, 0].reshape(n_exp * n_tile, per_tile)[:, 0]

    xs = _gather(off1, cnt1, g, posm_t, aff_t, cap)
    y = _experts(xs.reshape(n_exp, cap + WINDOW_ROWS, d + GATE_LANES),
                 w_gate[0], w_up[0], w_down[0], cap)
    out = _combine(off1, cnt1, y.reshape(n_exp * cap, d), posm_t, x1, mx[5], norm_final[None, :],
                   cap)
    return out[None]
```

```python
import functools
import math

import jax
import jax.numpy as jnp
from jax import lax
from jax.experimental import pallas as pl
from jax.experimental.pallas import tpu as pltpu

F32 = jnp.float32
BF16 = jnp.bfloat16
I32 = jnp.int32

HEAD_DIM = 128
N_HEADS = 8
N_KV_HEADS = 2
GROUP = N_HEADS // N_KV_HEADS
WINDOW = 128
Q_DIM = N_HEADS * HEAD_DIM
KV_DIM = N_KV_HEADS * HEAD_DIM
N_FGROUPS = 4
FG = 256
F_DIM = N_FGROUPS * FG
D_IN = Q_DIM + 2 * KV_DIM + F_DIM
N_EXPERTS = 16
CAP_FACTOR = 2
GRID_W = 64
ROPE_THETA = 10000.0
RMS_EPS = 1e-6
NEG_INF = -1e30
LOG2_E = math.log2(math.e)

LANES = 128
SUBLANES_F32 = 8
VMEM_LIMIT_BYTES = 56 * 1024 * 1024

TOK_TILE = 2 * LANES
PIECE = 48
ROW_ALIGN = 16
WINDOW_ROWS = ROW_ALIGN + PIECE
WINDOW_SIZES = tuple(range(2 * ROW_ALIGN, WINDOW_ROWS + 1, ROW_ALIGN))
GATE_LANES = LANES
DFT_NB = LANES
OUTPROJ_CHUNKS = 2
PROJ_CHUNKS = 2
EXPERT_ROW_SPLIT = 2
NOT_SELECTED = -(1 << 20)


def _cparams(*sem):
    return pltpu.CompilerParams(dimension_semantics=sem, vmem_limit_bytes=VMEM_LIMIT_BYTES)


def _rms_mod(x, nw, shift, scale):
    ms = jnp.mean(x * x, axis=-1, keepdims=True)
    return x * lax.rsqrt(ms + RMS_EPS) * (nw * (1.0 + scale)) + shift


def _mod_kernel(ct_ref, w_ref, b_ref, o_ref):
    ct = ct_ref[...]
    s = ct * jax.nn.sigmoid(ct)
    w = w_ref[...]
    r0 = jnp.sum(s[:, 0:1] * w, axis=0, keepdims=True)
    r1 = jnp.sum(s[:, 1:2] * w, axis=0, keepdims=True)
    o_ref[...] = jnp.concatenate([r0, r1], axis=0) + b_ref[...]


def _ab_kernel(cf_ref, sf_ref, wf_ref, ab_ref):
    wf = wf_ref[0]
    a = jnp.dot(cf_ref[...], wf, preferred_element_type=F32, precision=lax.Precision.HIGHEST)
    b = jnp.dot(sf_ref[...], wf, preferred_element_type=F32, precision=lax.Precision.HIGHEST)
    ab_ref[0] = jnp.concatenate([a, b], axis=1).astype(BF16)


def _mod_fold_kernel(ct_ref, w_ref, b_ref, cf_ref, sf_ref, wf_ref, o_ref, ab_ref):
    _mod_kernel(ct_ref, w_ref, b_ref, o_ref)
    _ab_kernel(cf_ref, sf_ref, wf_ref, ab_ref)


def _modulation_head(ct, w_mod, b_mod, n_cols, w_fourier):
    d = w_mod.shape[0]
    g, fg, _ = w_fourier.shape
    tn = n_cols // g
    idx = jnp.arange(fg, dtype=I32)
    th = ((idx[:, None] * idx[None, :]) % fg).astype(F32) * (2.0 * math.pi / fg)
    scale = 1.0 / math.sqrt(fg)
    cf = jnp.cos(th) * scale
    sf = jnp.sin(th) * scale
    return pl.pallas_call(
        _mod_fold_kernel,
        out_shape=(jax.ShapeDtypeStruct((2, n_cols), F32),
                   jax.ShapeDtypeStruct((g, fg, 2 * fg), BF16)),
        grid=(g,),
        in_specs=[pl.BlockSpec((d, 2), lambda i: (0, 0)),
                  pl.BlockSpec((d, tn), lambda i: (0, i)),
                  pl.BlockSpec((1, tn), lambda i: (0, i)),
                  pl.BlockSpec((fg, fg), lambda i: (0, 0)),
                  pl.BlockSpec((fg, fg), lambda i: (0, 0)),
                  pl.BlockSpec((1, fg, fg), lambda i: (i, 0, 0))],
        out_specs=(pl.BlockSpec((2, tn), lambda i: (0, i)),
                   pl.BlockSpec((1, fg, 2 * fg), lambda i: (i, 0, 0))),
        compiler_params=_cparams("arbitrary"),
        name="modulation_head_fold_channel_dft",
    )(ct, w_mod, b_mod, cf, sf, w_fourier)


def _rope_tables(n):
    quarter = HEAD_DIM // 4
    freqs = ROPE_THETA ** (-jnp.arange(quarter, dtype=F32) / quarter)
    zeros = lambda m: jnp.zeros((m, 2 * quarter), F32)

    def tables(pos, low_half):
        ang = pos[:, None] * freqs[None, :]
        ang = jnp.concatenate([ang, ang], axis=-1)
        cos, sin = jnp.cos(ang), jnp.sin(ang)
        first = (jnp.arange(2 * quarter) < quarter)[None, :]
        parts = (cos, jnp.where(first, -sin, 0.0), jnp.where(first, 0.0, sin))
        pad = zeros(pos.shape[0])
        return jnp.stack([jnp.concatenate([p, pad] if low_half else [pad, p], axis=-1)
                          for p in parts])

    row_tab = tables(jnp.arange(n // GRID_W, dtype=F32), True)
    col_tab = tables(jnp.arange(GRID_W, dtype=F32), False)
    return row_tab, col_tab


def _store_dft_rows(z_ref, plane, val, a0):
    groups = z_ref.shape[1]
    nb = groups * SUBLANES_F32
    for al in range(val.shape[0] // nb):
        z_ref[plane, :, (a0 + al) * SUBLANES_F32:(a0 + al + 1) * SUBLANES_F32, :] = (
            val[al * nb:(al + 1) * nb].reshape(groups, SUBLANES_F32, LANES))


def _proj_kernel(x_ref, nw_ref, sh_ref, sc_ref, w_ref, ab_ref, rt_ref, ct_ref, wo_ref,
                 cvec_ref, wm_ref, bm_ref, q_ref, k_ref, v_ref, z_ref, wo_b_ref, mod_ref):
    wo_b_ref[...] = wo_ref[...].astype(BF16)
    _mod_kernel(cvec_ref, wm_ref, bm_ref, mod_ref)
    quarter = HEAD_DIM // 4
    qscale = LOG2_E / math.sqrt(HEAD_DIM)
    tm = x_ref.shape[0]
    chunk = tm // PROJ_CHUNKS
    for r0 in range(0, tm, chunk):
        rs = slice(r0, r0 + chunk)
        hb = _rms_mod(x_ref[rs, :], nw_ref[...], sh_ref[...], sc_ref[...]).astype(BF16)
        cos, s_up, s_dn = (
            jnp.concatenate(
                [rt_ref[t, r:r + 1, :] + ct_ref[t]
                 for r in range(r0 // GRID_W, (r0 + chunk) // GRID_W)], axis=0)
            for t in range(3))

        def rope(t):
            return (t * cos + pltpu.roll(t, HEAD_DIM - quarter, 1) * s_up
                    + pltpu.roll(t, quarter, 1) * s_dn)

        q = jnp.dot(hb, w_ref[:, :Q_DIM], preferred_element_type=F32)
        for j in range(N_HEADS):
            sl = slice(j * HEAD_DIM, (j + 1) * HEAD_DIM)
            q_ref[rs, sl] = (rope(q[:, sl]) * qscale).astype(BF16)
        k = jnp.dot(hb, w_ref[:, Q_DIM:Q_DIM + KV_DIM], preferred_element_type=F32)
        for j in range(N_KV_HEADS):
            sl = slice(j * HEAD_DIM, (j + 1) * HEAD_DIM)
            k_ref[rs, sl] = rope(k[:, sl]).astype(BF16)
        v = jnp.dot(hb, w_ref[:, Q_DIM + KV_DIM:Q_DIM + 2 * KV_DIM], preferred_element_type=F32)
        v_ref[rs, :] = v.astype(BF16)
        u = jnp.dot(hb, w_ref[:, Q_DIM + 2 * KV_DIM:], preferred_element_type=F32).astype(BF16)
        for g in range(N_FGROUPS):
            pq = jnp.dot(u[:, g * FG:(g + 1) * FG], ab_ref[g], preferred_element_type=F32)
            per_group = FG // LANES
            for t in range(2 * per_group):
                plane = (t // per_group) * (F_DIM // LANES) + g * per_group + t % per_group
                _store_dft_rows(z_ref, plane, pq[:, t * LANES:(t + 1) * LANES], r0 // DFT_NB)


def _project_latent(x, nw, shift, scale, w_in_b, ab, w_out, cvec, w_mod, b_mod, mod_first, tm=512):
    n, d = x.shape
    steps = n // tm
    wo_rows = w_out.shape[0] // steps
    mod_cols = (w_mod.shape[1] - mod_first) // steps
    mod_blk0 = mod_first // mod_cols
    row_tab, col_tab = _rope_tables(n)
    row = lambda i: (i, 0)
    const2 = lambda i: (0, 0)
    return pl.pallas_call(
        _proj_kernel,
        out_shape=(jax.ShapeDtypeStruct((n, Q_DIM), BF16),
                   jax.ShapeDtypeStruct((n, KV_DIM), BF16),
                   jax.ShapeDtypeStruct((n, KV_DIM), BF16),
                   jax.ShapeDtypeStruct((2 * F_DIM // LANES, DFT_NB // SUBLANES_F32,
                                         (n // DFT_NB) * SUBLANES_F32, LANES), F32),
                   jax.ShapeDtypeStruct(w_out.shape, BF16),
                   jax.ShapeDtypeStruct((2, w_mod.shape[1] - mod_first), F32)),
        grid=(n // tm,),
        in_specs=[pl.BlockSpec((tm, d), row),
                  pl.BlockSpec((1, d), const2), pl.BlockSpec((1, d), const2),
                  pl.BlockSpec((1, d), const2),
                  pl.BlockSpec((d, D_IN), const2, pipeline_mode=pl.Buffered(1)),
                  pl.BlockSpec((N_FGROUPS, FG, 2 * FG), lambda i: (0, 0, 0),
                               pipeline_mode=pl.Buffered(1)),
                  pl.BlockSpec((3, tm // GRID_W, HEAD_DIM), lambda i: (0, i, 0)),
                  pl.BlockSpec((3, GRID_W, HEAD_DIM), lambda i: (0, 0, 0)),
                  pl.BlockSpec((wo_rows, w_out.shape[1]), row),
                  pl.BlockSpec(cvec.shape, const2),
                  pl.BlockSpec((w_mod.shape[0], mod_cols), lambda i: (0, mod_blk0 + i)),
                  pl.BlockSpec((1, mod_cols), lambda i: (0, mod_blk0 + i))],
        out_specs=(pl.BlockSpec((tm, Q_DIM), row), pl.BlockSpec((tm, KV_DIM), row),
                   pl.BlockSpec((tm, KV_DIM), row),
                   pl.BlockSpec((2 * F_DIM // LANES, DFT_NB // SUBLANES_F32,
                                 (tm // DFT_NB) * SUBLANES_F32, LANES), lambda i: (0, 0, i, 0)),
                   pl.BlockSpec((wo_rows, w_out.shape[1]), row),
                   pl.BlockSpec((2, mod_cols), lambda i: (0, i))),
        compiler_params=_cparams("arbitrary"),
        name="project_latent",
    )(x, nw, shift, scale, w_in_b, ab, row_tab, col_tab, w_out, cvec, w_mod, b_mod)


def _ctx_kernel(x_ref, nw_ref, sh_ref, sc_ref, w_ref, k_ref, v_ref):
    hb = _rms_mod(x_ref[...], nw_ref[...], sh_ref[...], sc_ref[...]).astype(BF16)
    kv = jnp.dot(hb, w_ref[...], preferred_element_type=F32)
    k_ref[...] = kv[:, :KV_DIM].astype(BF16)
    v_ref[...] = kv[:, KV_DIM:].astype(BF16)


def _project_context(ctx, nw, shift, scale, w_kv_b):
    m, d = ctx.shape
    full = lambda shp: pl.BlockSpec(shp, lambda i: (0, 0))
    return pl.pallas_call(
        _ctx_kernel,
        out_shape=(jax.ShapeDtypeStruct((m, KV_DIM), BF16),
                   jax.ShapeDtypeStruct((m, KV_DIM), BF16)),
        grid=(1,),
        in_specs=[full((m, d)), full((1, d)), full((1, d)), full((1, d)), full((d, 2 * KV_DIM))],
        out_specs=(full((m, KV_DIM)), full((m, KV_DIM))),
        compiler_params=_cparams("arbitrary"),
        name="project_context",
    )(ctx, nw, shift, scale, w_kv_b)


def _attn_body(sink_ref, q_ref, kp_ref, km_ref, kn_ref, vp_ref, vm_ref, vn_ref,
               kc_ref, vc_ref, lo_ref, hi_ref, o_ref, *, n_total, tq, first_step):
    i = pl.program_id(0) + first_step
    nsub = tq // WINDOW
    last_blk = n_total // WINDOW - 1
    kwin = jnp.concatenate([kp_ref[...], km_ref[...], kn_ref[...]], axis=0)
    vwin = jnp.concatenate([vp_ref[...], vm_ref[...], vn_ref[...]], axis=0)
    rows = GROUP * WINDOW
    span = 3 * WINDOW
    head_of_row = lax.broadcasted_iota(I32, (rows, 1), 0) // WINDOW
    ones_loc = jnp.ones((span, HEAD_DIM), BF16)
    ones_ctx = jnp.ones((kc_ref.shape[0], HEAD_DIM), BF16)
    nt = (((1,), (1,)), ((), ()))
    for b in range(nsub):
        blk = i * nsub + b
        bias_lo = lo_ref[(blk == 0).astype(I32)]
        bias_hi = hi_ref[(blk == last_blk).astype(I32)]
        for h in range(N_KV_HEADS):
            hs = slice(h * HEAD_DIM, (h + 1) * HEAD_DIM)
            qs = jnp.concatenate(
                [q_ref[b * WINDOW:(b + 1) * WINDOW,
                       (h * GROUP + g) * HEAD_DIM:(h * GROUP + g + 1) * HEAD_DIM]
                 for g in range(GROUP)], axis=0)
            kw = kwin[b * WINDOW:b * WINDOW + span, hs]
            vw = vwin[b * WINDOW:b * WINDOW + span, hs]
            s_loc = lax.dot_general(qs, kw, nt, preferred_element_type=F32)
            s_ctx = lax.dot_general(qs, kc_ref[:, hs], nt, preferred_element_type=F32)
            parts = [s_loc[:, :WINDOW] + bias_lo, s_loc[:, WINDOW:2 * WINDOW],
                     s_loc[:, 2 * WINDOW:] + bias_hi, s_ctx]
            sink_col = jnp.zeros((rows, 1), F32)
            for g in range(GROUP):
                sink_col = jnp.where(head_of_row == g, sink_ref[h * GROUP + g] * LOG2_E, sink_col)
            blocks = parts[:3] + [s_ctx[:, t * WINDOW:(t + 1) * WINDOW]
                                  for t in range(s_ctx.shape[1] // WINDOW)]
            widest = blocks[0]
            for blk_scores in blocks[1:]:
                widest = jnp.maximum(widest, blk_scores)
            m = jnp.maximum(sink_col, jnp.max(widest, axis=1, keepdims=True))
            p = [jnp.exp2(part - m).astype(BF16) for part in parts]
            ov = (jnp.dot(jnp.concatenate(p[:3], axis=1), jnp.concatenate([vw, ones_loc], axis=1),
                          preferred_element_type=F32)
                  + jnp.dot(p[3], jnp.concatenate([vc_ref[:, hs], ones_ctx], axis=1),
                            preferred_element_type=F32))
            o = ov[:, :HEAD_DIM] / (ov[:, HEAD_DIM:] + jnp.exp2(sink_col - m))
            for g in range(GROUP):
                o_ref[b * WINDOW:(b + 1) * WINDOW,
                      (h * GROUP + g) * HEAD_DIM:(h * GROUP + g + 1) * HEAD_DIM] = (
                    o[g * WINDOW:(g + 1) * WINDOW].astype(BF16))


def _band_biases():
    r = jnp.arange(GROUP * WINDOW, dtype=I32)[:, None] % WINDOW
    c = jnp.arange(WINDOW, dtype=I32)[None, :]
    masked = jnp.full((GROUP * WINDOW, WINDOW), NEG_INF, F32)
    lo = jnp.stack([jnp.where(c >= r, 0.0, NEG_INF).astype(F32), masked])
    hi = jnp.stack([jnp.where(c <= r, 0.0, NEG_INF).astype(F32), masked])
    return lo, hi


def _dft_tables(na, nb):
    n = na * nb
    s1 = 2.0 ** (-(int(math.log2(na)) // 2))
    s2 = (1.0 / math.sqrt(n)) / s1
    ka = jnp.arange(na, dtype=I32)
    th_tw = ((jnp.arange(nb, dtype=I32)[:, None] * ka[None, :]) % n).astype(F32) * (2.0 * math.pi / n)
    th_f = ((ka[:, None] * ka[None, :]) % na).astype(F32) * (2.0 * math.pi / na)
    tw = jnp.stack([jnp.cos(th_tw) * s1, -jnp.sin(th_tw) * s1])
    tw = tw.reshape(2, nb // SUBLANES_F32, SUBLANES_F32, na).transpose(0, 1, 3, 2)
    f = jnp.stack([jnp.cos(th_f), -jnp.sin(th_f)])
    kb = jnp.arange(nb, dtype=I32)
    th2 = ((kb[:, None] * kb[None, :]) % nb).astype(F32) * (2.0 * math.pi / nb)
    t2 = (jnp.concatenate([jnp.cos(th2), jnp.sin(th2)], axis=-1) * s2).astype(BF16)
    return tw, f, t2


def _dft1_kernel(z_ref, tw_ref, f_ref, y_ref):
    planes, _, rows, _ = z_ref.shape
    step = SUBLANES_F32
    half, na = planes // 2, rows // step
    fr, fi = f_ref[0], f_ref[1]
    for j in range(step):
        twr, twi = tw_ref[0, 0][:, j:j + 1], tw_ref[1, 0][:, j:j + 1]
        tr, ti = twr * fr - twi * fi, twr * fi + twi * fr
        t_b = jnp.concatenate([jnp.concatenate([tr, ti], axis=1),
                               jnp.concatenate([ti, -tr], axis=1)], axis=0).astype(BF16)
        sel = pl.ds(j, na, stride=step)
        p = jnp.concatenate([z_ref[cc, 0, sel, :] for cc in range(half)], axis=1)
        q = jnp.concatenate([z_ref[cc, 0, sel, :] for cc in range(half, planes)], axis=1)
        xs = jnp.concatenate([p, q], axis=0).astype(BF16)
        y = jnp.dot(t_b, xs, preferred_element_type=F32)
        for cc in range(half):
            y_ref[cc, 0, sel, :] = y[:na, cc * LANES:(cc + 1) * LANES]
            y_ref[half + cc, 0, sel, :] = y[na:, cc * LANES:(cc + 1) * LANES]


def _dft2_kernel(y_ref, t_ref, o_ref):
    planes, groups, rows, _ = y_ref.shape
    step = SUBLANES_F32
    half, nb = planes // 2, groups * step

    def tokens_b(cc, j):
        return y_ref[cc, :, j * step:(j + 1) * step, :].reshape(nb, LANES)

    for j in range(rows // step):
        yr = jnp.concatenate([tokens_b(cc, j) for cc in range(half)], axis=1)
        yi = jnp.concatenate([tokens_b(cc, j) for cc in range(half, planes)], axis=1)
        xs = jnp.concatenate([yr, yi], axis=0).astype(BF16)
        out = jnp.dot(t_ref[...], xs, preferred_element_type=F32)
        for cc in range(half):
            o_ref[cc, 0, pl.ds(j, nb, stride=step), :] = out[:, cc * LANES:(cc + 1) * LANES]


def _attn_dft_kernel(*refs, n_total, tq, first_step, stage):
    attn_in, dft_in, (o_ref, d_out) = refs[:12], refs[12:-2], refs[-2:]
    _attn_body(*attn_in, o_ref, n_total=n_total, tq=tq, first_step=first_step)
    (_dft1_kernel if stage == 1 else _dft2_kernel)(*dft_in, d_out)


def _attention_and_dft(q, k, v, kc, vc, sink, z):
    n, m = q.shape[0], kc.shape[0]
    planes, groups, rows, _ = z.shape
    step = SUBLANES_F32
    nb, na = groups * step, rows // step
    tw, f, t2 = _dft_tables(na, nb)
    bias_lo, bias_hi = _band_biases()
    nblk = n // WINDOW
    half = n // 2

    def call(stage, steps, first_row, dft_in, dft_in_specs, d_out_shape, d_out_spec):
        tq = half // steps
        nsub = tq // WINDOW
        first_step = first_row // tq
        main = lambda i, s: (i + first_step, 0)
        prev = lambda i, s: (jnp.maximum((i + first_step) * nsub - 1, 0), 0)
        nxt = lambda i, s: (jnp.minimum((i + first_step) * nsub + nsub, nblk - 1), 0)
        const = lambda i, s: (0, 0)
        kv_specs = [pl.BlockSpec((WINDOW, KV_DIM), prev), pl.BlockSpec((tq, KV_DIM), main),
                    pl.BlockSpec((WINDOW, KV_DIM), nxt)]
        bias_spec = pl.BlockSpec(bias_lo.shape, lambda i, s: (0, 0, 0))
        return pl.pallas_call(
            functools.partial(_attn_dft_kernel, n_total=n, tq=tq, first_step=first_step,
                              stage=stage),
            out_shape=(jax.ShapeDtypeStruct((half, Q_DIM), BF16), d_out_shape),
            grid_spec=pltpu.PrefetchScalarGridSpec(
                num_scalar_prefetch=1,
                grid=(steps,),
                in_specs=[pl.BlockSpec((tq, Q_DIM), main)] + kv_specs + kv_specs
                         + [pl.BlockSpec((m, KV_DIM), const), pl.BlockSpec((m, KV_DIM), const),
                            bias_spec, bias_spec] + dft_in_specs,
                out_specs=(pl.BlockSpec((tq, Q_DIM), lambda i, s: (i, 0)), d_out_spec)),
            compiler_params=_cparams("arbitrary"),
            name=f"attention_half{stage}_dft_stage{stage}",
        )(sink, q, k, k, k, v, v, v, kc, vc, bias_lo, bias_hi, *dft_in)

    ax_first, y = call(
        1, groups, 0, [z, tw, f],
        [pl.BlockSpec((planes, 1, rows, LANES), lambda i, s: (0, i, 0, 0)),
         pl.BlockSpec((2, 1, na, step), lambda i, s: (0, i, 0, 0)),
         pl.BlockSpec((2, na, na), lambda i, s: (0, 0, 0))],
        jax.ShapeDtypeStruct(z.shape, F32),
        pl.BlockSpec((planes, 1, rows, LANES), lambda i, s: (0, i, 0, 0)))
    ax_second, fx = call(
        2, na // step, half, [y, t2],
        [pl.BlockSpec((planes, groups, step * step, LANES), lambda i, s: (0, 0, i, 0)),
         pl.BlockSpec((nb, 2 * nb), lambda i, s: (0, 0))],
        jax.ShapeDtypeStruct((planes // 2, na // step, nb * step, LANES), F32),
        pl.BlockSpec((planes // 2, 1, nb * step, LANES), lambda i, s: (0, i, 0, 0)))
    return ax_first, ax_second, fx


def _outproj_kernel(ax1_ref, ax2_ref, fx_ref, x_ref, wo_ref, gate_ref, nw_ref, sh_ref, sc_ref,
                    wr_ref, x1_ref, g_ref, aff_ref):
    planes, groups, rows, _ = fx_ref.shape
    in_first_half = pl.program_id(0) < pl.num_programs(0) // 2
    na = groups * SUBLANES_F32
    n_kb = rows // SUBLANES_F32
    kb_per_chunk = max(1, n_kb // OUTPROJ_CHUNKS)
    for first_kb in range(0, n_kb, kb_per_chunk):
        rs = slice(first_kb * na, (first_kb + kb_per_chunk) * na)
        fx = jnp.concatenate(
            [jnp.concatenate(
                [fx_ref[cc, :, kb * SUBLANES_F32:(kb + 1) * SUBLANES_F32, :].reshape(na, LANES)
                 for kb in range(first_kb, first_kb + kb_per_chunk)], axis=0)
             for cc in range(planes)], axis=1).astype(BF16)
        ax = jnp.where(in_first_half, ax1_ref[rs, :], ax2_ref[rs, :])
        acc = (jnp.dot(ax, wo_ref[:Q_DIM, :], preferred_element_type=F32)
               + jnp.dot(fx, wo_ref[Q_DIM:, :], preferred_element_type=F32))
        x1 = x_ref[rs, :] + gate_ref[...] * acc
        x1_ref[rs, :] = x1
        gb = _rms_mod(x1, nw_ref[...], sh_ref[...], sc_ref[...]).astype(BF16)
        g_ref[rs, :] = gb
        logits = jnp.dot(gb, wr_ref[...], preferred_element_type=F32)
        e = jnp.exp(logits - jnp.max(logits, axis=1, keepdims=True))
        aff_ref[rs, :] = e / jnp.sum(e, axis=1, keepdims=True)


def _output_projection(ax1, ax2, fx, x, wo_b, gate, nw, shift, scale, wr_b, tm=512):
    n, d = x.shape
    ne = wr_b.shape[1]
    half_steps = n // tm // 2
    row = lambda i: (i, 0)
    const = lambda i: (0, 0)
    vec = pl.BlockSpec((1, d), const)
    return pl.pallas_call(
        _outproj_kernel,
        out_shape=(jax.ShapeDtypeStruct((n, d), F32), jax.ShapeDtypeStruct((n, d), BF16),
                   jax.ShapeDtypeStruct((n, ne), F32)),
        grid=(n // tm,),
        in_specs=[pl.BlockSpec((tm, Q_DIM), lambda i: (jnp.minimum(i, half_steps - 1), 0)),
                  pl.BlockSpec((tm, Q_DIM), lambda i: (jnp.maximum(i - half_steps, 0), 0)),
                  pl.BlockSpec((fx.shape[0], fx.shape[1], tm // (fx.shape[1] * SUBLANES_F32)
                                * SUBLANES_F32, LANES), lambda i: (0, 0, i, 0)),
                  pl.BlockSpec((tm, d), row),
                  pl.BlockSpec((Q_DIM + F_DIM, d), const, pipeline_mode=pl.Buffered(1)),
                  vec, vec, vec, vec, pl.BlockSpec((d, ne), const)],
        out_specs=(pl.BlockSpec((tm, d), row), pl.BlockSpec((tm, d), row),
                   pl.BlockSpec((tm, ne), row)),
        compiler_params=_cparams("arbitrary"),
        name="output_projection",
    )(ax1, ax2, fx, x, wo_b, gate, nw, shift, scale, wr_b)


def _route_kernel(aff_ref, posm_ref, cnt_ref, off_ref, *, cap, n_exp, n_chunk):
    aff_all = aff_ref[...]
    blocks = [aff_all[e * n_chunk:(e + 1) * n_chunk] for e in range(n_exp)]

    def total(mask):
        s = jnp.sum(jnp.where(mask, 1.0, 0.0), axis=0, keepdims=True)
        return jnp.sum(s, axis=1, keepdims=True)

    def search(it, thr_bits):
        bit = jnp.left_shift(jnp.int32(1), 30 - it)
        out = []
        for e in range(n_exp):
            cand = thr_bits[e] | bit
            enough = total(blocks[e] >= lax.bitcast_convert_type(cand, F32)) >= cap
            out.append(jnp.where(enough, cand, thr_bits[e]))
        return tuple(out)

    thr_bits = lax.fori_loop(0, 31, search, tuple(jnp.zeros((1, 1), I32) for _ in range(n_exp)))
    thr = [lax.bitcast_convert_type(t, F32) for t in thr_bits]

    rl = lax.broadcasted_iota(I32, (LANES, LANES), 0)
    cl = lax.broadcasted_iota(I32, (LANES, LANES), 1)
    before = jnp.where(rl < cl, 1.0, 0.0).astype(BF16)
    ones = jnp.ones((LANES, LANES), BF16)
    rc = lax.broadcasted_iota(I32, (n_chunk, n_chunk), 0)
    cc = lax.broadcasted_iota(I32, (n_chunk, n_chunk), 1)
    earlier = jnp.where(cc < rc, 1.0, 0.0).astype(BF16)

    def excl_cumsum(mask):
        mb = jnp.where(mask, 1.0, 0.0).astype(BF16)
        within = jnp.dot(mb, before, preferred_element_type=F32)
        rowtot = jnp.dot(mb, ones, preferred_element_type=F32)
        choff = jnp.dot(earlier, rowtot.astype(BF16), preferred_element_type=F32)
        return within + choff, rowtot, choff

    for e in range(n_exp):
        gt = blocks[e] > thr[e]
        eq = blocks[e] == thr[e]
        need = cap - total(gt)
        tie_rank, _, _ = excl_cumsum(eq)
        sel = gt | (eq & (tie_rank < need))
        pos, rowtot, choff = excl_cumsum(sel)
        rs = slice(e * n_chunk, (e + 1) * n_chunk)
        posm_ref[rs, :] = jnp.where(sel, pos.astype(I32), NOT_SELECTED)
        cnt_ref[rs, :] = rowtot.astype(I32)
        off_ref[rs, :] = choff.astype(I32)


def _route(aff2, cap, n_exp):
    rows = aff2.shape[0]
    n_chunk = rows // n_exp
    spec = pl.BlockSpec((rows, LANES), lambda i: (0, 0))
    shp = jax.ShapeDtypeStruct((rows, LANES), I32)
    return pl.pallas_call(
        functools.partial(_route_kernel, cap=cap, n_exp=n_exp, n_chunk=n_chunk),
        out_shape=(shp, shp, shp),
        grid=(1,),
        in_specs=[spec],
        out_specs=(spec, spec, spec),
        compiler_params=_cparams("arbitrary"),
        name="expert_choice_routing",
    )(aff2)


def _window_matches(posm, firsts, win_starts):
    r = lax.broadcasted_iota(I32, (WINDOW_ROWS, TOK_TILE), 0)
    out = []
    for e, (first, start) in enumerate(zip(firsts, win_starts)):
        pm = posm[e:e + 1, :]
        pm = jnp.where((pm >= first) & (pm < first + PIECE), pm, NOT_SELECTED)
        out.append((pm - start) == r)
    return out


def _align_down(v):
    return (v // ROW_ALIGN) * ROW_ALIGN


def _for_window_size(rows_needed, fn):
    below = None
    for k, size in enumerate(WINDOW_SIZES):
        fits = rows_needed <= size if k + 1 < len(WINDOW_SIZES) else None
        cond = fits if below is None else (below if fits is None else jnp.logical_and(below, fits))
        pl.when(cond)(functools.partial(fn, size))
        below = rows_needed > size


def _gather_kernel(off_ref, cnt_ref, g_ref, posm_ref, aff_ref, xs_hbm,
                   stg, tail, sem, npass_ref, sent_ref, *, n_exp, n_chunk, cap):
    c = pl.program_id(0)
    stride = cap + WINDOW_ROWS

    @pl.when(c == 0)
    def _():
        npass_ref[0] = 0
        tail[...] = jnp.zeros_like(tail)

    def window_copy(slot, e, dst, size=WINDOW_ROWS):
        return pltpu.make_async_copy(stg.at[slot, pl.ds(e * WINDOW_ROWS, size)],
                                     xs_hbm.at[pl.ds(dst, size)], sem.at[slot, e])

    def start_window(slot, e, dst, rows_used):
        _for_window_size(rows_used, lambda size: window_copy(slot, e, dst, size).start())
        sent_ref[slot * n_exp + e] = rows_used

    def wait_window(slot, e):
        _for_window_size(sent_ref[slot * n_exp + e],
                         lambda size: window_copy(slot, e, 0, size).wait())

    @pl.when(c == 0)
    def _():
        stg[0] = jnp.zeros(stg.shape[1:], stg.dtype)
        for e in range(n_exp):
            window_copy(0, e, e * stride + cap).start()
        for e in range(n_exp):
            window_copy(0, e, 0).wait()

    most = cnt_ref[c]
    for e in range(1, n_exp):
        most = jnp.maximum(most, cnt_ref[e * n_chunk + c])
    passes = (most + PIECE - 1) // PIECE

    def one_pass(p, carry):
        done = npass_ref[0]
        slot = done % 2
        posm = posm_ref[0]
        aff = aff_ref[0]
        firsts, win_starts, next_shift, rows_used = [], [], [], []
        for e in range(n_exp):
            o, n_e = off_ref[e * n_chunk + c], cnt_ref[e * n_chunk + c]
            first = o + jnp.minimum(p * PIECE, n_e)
            after = o + jnp.minimum((p + 1) * PIECE, n_e)
            firsts.append(first)
            win_starts.append(_align_down(first))
            next_shift.append(_align_down(after) - _align_down(first))
            rows_used.append(after - _align_down(first))
        matches = _window_matches(posm, firsts, win_starts)
        onehot = jnp.concatenate([jnp.where(mt, 1.0, 0.0).astype(BF16) for mt in matches], axis=0)
        new = jnp.dot(onehot, g_ref[...], preferred_element_type=F32)
        lane = lax.broadcasted_iota(I32, (WINDOW_ROWS, GATE_LANES), 1)
        for e in range(n_exp):
            lo = e * WINDOW_ROWS
            gate = jnp.sum(jnp.where(matches[e], aff[e:e + 1, :], 0.0), axis=1, keepdims=True)
            hi = gate.astype(BF16).astype(F32)
            mid = (gate - hi).astype(BF16).astype(F32)
            low = gate - hi - mid
            parts = jnp.where(lane == 0, hi, jnp.where(lane == 1, mid,
                                                       jnp.where(lane == 2, low, 0.0)))
            rows = jnp.concatenate([new[lo:lo + WINDOW_ROWS], parts], axis=1)
            stg[slot, lo:lo + ROW_ALIGN] = (rows[:ROW_ALIGN] + tail[e].astype(F32)).astype(BF16)
            stg[slot, lo + ROW_ALIGN:lo + WINDOW_ROWS] = rows[ROW_ALIGN:].astype(BF16)

        dsts = [pl.multiple_of(e * stride + win_starts[e], ROW_ALIGN) for e in range(n_exp)]

        @pl.when(done > 0)
        def _():
            for e in range(n_exp):
                wait_window(1 - slot, e)
                start_window(slot, e, dsts[e], rows_used[e])

        @pl.when(done == 0)
        def _():
            for e in range(n_exp):
                start_window(slot, e, dsts[e], rows_used[e])

        for e in range(n_exp):
            src = pl.ds(pl.multiple_of(e * WINDOW_ROWS + next_shift[e], ROW_ALIGN), ROW_ALIGN)
            tail[e] = stg[slot, src, :]
        npass_ref[0] = done + 1
        return carry

    lax.fori_loop(0, passes, one_pass, 0)

    @pl.when(c == pl.num_programs(0) - 1)
    def _():
        done = npass_ref[0]

        @pl.when(done > 0)
        def _():
            for e in range(n_exp):
                wait_window((done - 1) % 2, e)


def _gather(off, cnt, g, posm_t, aff_t, cap):
    n, d = g.shape
    n_chunk, n_exp, _ = posm_t.shape
    rows = n_exp * (cap + WINDOW_ROWS)
    tile3 = lambda i, o, c: (i, 0, 0)
    return pl.pallas_call(
        functools.partial(_gather_kernel, n_exp=n_exp, n_chunk=n_chunk, cap=cap),
        out_shape=jax.ShapeDtypeStruct((rows, d + GATE_LANES), BF16),
        grid_spec=pltpu.PrefetchScalarGridSpec(
            num_scalar_prefetch=2,
            grid=(n_chunk,),
            in_specs=[pl.BlockSpec((TOK_TILE, d), lambda i, o, c: (i, 0)),
                      pl.BlockSpec((1, n_exp, TOK_TILE), tile3),
                      pl.BlockSpec((1, n_exp, TOK_TILE), tile3)],
            out_specs=pl.BlockSpec(memory_space=pl.ANY),
            scratch_shapes=[pltpu.VMEM((2, n_exp * WINDOW_ROWS, d + GATE_LANES), BF16),
                            pltpu.VMEM((n_exp, ROW_ALIGN, d + GATE_LANES), BF16),
                            pltpu.SemaphoreType.DMA((2, n_exp)),
                            pltpu.SMEM((1,), I32),
                            pltpu.SMEM((2 * n_exp,), I32)]),
        compiler_params=_cparams("arbitrary"),
        name="gather_expert_tokens",
    )(off, cnt, g, posm_t, aff_t)


def _expert_kernel(xs_hbm, wg_ref, wu_ref, wd_ref, y_ref, xs_buf, h_ref, gate_ref, sem,
                   *, nf, tf, tn):
    e, j = pl.program_id(0), pl.program_id(1)
    d = wg_ref.shape[1]
    cap = xs_buf.shape[0]
    halves = [slice(r * (cap // EXPERT_ROW_SPLIT), (r + 1) * (cap // EXPERT_ROW_SPLIT))
              for r in range(EXPERT_ROW_SPLIT)]

    def rows_copy(expert):
        return pltpu.make_async_copy(xs_hbm.at[expert, pl.ds(0, cap)], xs_buf, sem.at[0])

    @pl.when((e == 0) & (j == 0))
    def _():
        rows_copy(0).start()

    @pl.when(j == 0)
    def _():
        rows_copy(e).wait()
        parts = xs_buf[:, d:].astype(F32)
        gate = parts[:, 0:1] + parts[:, 1:2] + parts[:, 2:3]
        gate_ref[...] = jnp.broadcast_to(gate, gate_ref.shape)

    @pl.when((j == nf) & (e + 1 < pl.num_programs(0)))
    def _():
        rows_copy(e + 1).start()

    @pl.when(j < nf)
    def _():
        wg = wg_ref[0].astype(BF16)
        wu = wu_ref[0].astype(BF16)
        for rs in halves:
            xs = xs_buf[rs, :d]
            gp = jnp.dot(xs, wg, preferred_element_type=F32)
            up = jnp.dot(xs, wu, preferred_element_type=F32)
            h = ((gp * jax.nn.sigmoid(gp)) * up).astype(BF16)
            for jj in range(nf):
                @pl.when(j == jj)
                def _():
                    h_ref[rs, jj * tf:(jj + 1) * tf] = h

    @pl.when(j >= nf)
    def _():
        wd = wd_ref[0].astype(BF16)
        for rs in halves:
            y = jnp.dot(h_ref[rs, :], wd, preferred_element_type=F32)
            y_ref[0, rs, :] = (y * gate_ref[rs, 0:1]).astype(BF16)


def _experts(xs3, w_gate, w_up, w_down, cap, tf=512, tn=1024):
    n_exp, d, ff = w_gate.shape
    tf, tn = min(tf, ff), min(tn, d)
    nf, nn = ff // tf, d // tn

    def ahead(e, j):
        return jnp.minimum(e + (j >= nf).astype(I32), n_exp - 1)

    def ff_tile(e, j):
        return jnp.where(j >= nf, jnp.where(e == n_exp - 1, nf - 1, 0), j)

    return pl.pallas_call(
        functools.partial(_expert_kernel, nf=nf, tf=tf, tn=tn),
        out_shape=jax.ShapeDtypeStruct((n_exp, cap, d), BF16),
        grid=(n_exp, nf + nn),
        in_specs=[pl.BlockSpec(memory_space=pl.ANY),
                  pl.BlockSpec((1, d, tf), lambda e, j: (ahead(e, j), 0, ff_tile(e, j))),
                  pl.BlockSpec((1, d, tf), lambda e, j: (ahead(e, j), 0, ff_tile(e, j))),
                  pl.BlockSpec((1, ff, tn), lambda e, j: (e, 0, jnp.maximum(j - nf, 0)))],
        out_specs=pl.BlockSpec((1, cap, tn), lambda e, j: (e, 0, jnp.maximum(j - nf, 0))),
        scratch_shapes=[pltpu.VMEM((cap, d + GATE_LANES), BF16), pltpu.VMEM((cap, ff), BF16),
                        pltpu.VMEM((cap, LANES), F32), pltpu.SemaphoreType.DMA((1,))],
        compiler_params=_cparams("arbitrary", "arbitrary"),
        name="swiglu_experts",
    )(xs3, w_gate, w_up, w_down)


def _combine_kernel(off_ref, cnt_ref, y_hbm, posm_ref, x1_ref, gate_ref, nw_ref, o_ref,
                    stg, acc_ref, sem, *, n_exp, n_chunk, cap):
    c = pl.program_id(0)
    last_start = n_exp * cap - WINDOW_ROWS

    def firsts_of(tile, p):
        return [off_ref[e * n_chunk + tile] + p * PIECE for e in range(n_exp)]

    def starts_of(tile, p):
        return [pl.multiple_of(jnp.minimum(e * cap + _align_down(first), last_start), ROW_ALIGN)
                for e, first in enumerate(firsts_of(tile, p))]

    def rows_used_of(tile, p):
        out = []
        for e, (first, start) in enumerate(zip(firsts_of(tile, p), starts_of(tile, p))):
            left = jnp.clip(cnt_ref[e * n_chunk + tile] - p * PIECE, 0, PIECE)
            out.append(e * cap + first + left - start)
        return out

    def piece_copy(slot, e, src, size=WINDOW_ROWS):
        return pltpu.make_async_copy(y_hbm.at[pl.ds(src, size)],
                                     stg.at[slot, pl.ds(e * WINDOW_ROWS, size)], sem.at[slot])

    def fetch(tile, p, slot):
        for e, (src, used) in enumerate(zip(starts_of(tile, p), rows_used_of(tile, p))):
            _for_window_size(used, lambda size: piece_copy(slot, e, src, size).start())

    def wait_fetch(tile, p, slot):
        for e, used in enumerate(rows_used_of(tile, p)):
            _for_window_size(used, lambda size: piece_copy(slot, e, 0, size).wait())

    def contribution(p, slot, tokens):
        starts = [s - e * cap for e, s in enumerate(starts_of(c, p))]
        matches = _window_matches(posm_ref[0], firsts_of(c, p), starts)
        onehot = jnp.concatenate([jnp.where(mt[:, tokens], 1.0, 0.0).astype(BF16)
                                  for mt in matches], axis=0)
        return lax.dot_general(onehot, stg[slot], (((0,), (0,)), ((), ())),
                               preferred_element_type=F32)

    def finish(tokens, moe):
        x2 = x1_ref[tokens, :] + gate_ref[...] * moe
        ms = jnp.mean(x2 * x2, axis=-1, keepdims=True)
        o_ref[tokens, :] = x2 * lax.rsqrt(ms + RMS_EPS) * nw_ref[...]

    slot = c % 2

    @pl.when(c == 0)
    def _():
        stg[...] = jnp.zeros_like(stg)
        fetch(0, 0, 0)

    @pl.when(c + 1 < pl.num_programs(0))
    def _():
        fetch(c + 1, 0, 1 - slot)

    most = cnt_ref[c]
    for e in range(1, n_exp):
        most = jnp.maximum(most, cnt_ref[e * n_chunk + c])
    passes = (most + PIECE - 1) // PIECE
    wait_fetch(c, 0, slot)

    @pl.when(passes <= 1)
    def _():
        for t0 in range(0, TOK_TILE, LANES):
            tokens = slice(t0, t0 + LANES)
            finish(tokens, contribution(0, slot, tokens))

    @pl.when(passes > 1)
    def _():
        everyone = slice(0, TOK_TILE)
        acc_ref[...] = contribution(0, slot, everyone)

        def extra_pass(p, carry):
            fetch(c, p, slot)
            wait_fetch(c, p, slot)
            acc_ref[...] += contribution(p, slot, everyone)
            return carry

        lax.fori_loop(1, passes, extra_pass, 0)
        finish(everyone, acc_ref[...])


def _combine(off, cnt, y, posm_t, x1, gate, nw, cap):
    n, d = x1.shape
    n_chunk, n_exp, _ = posm_t.shape
    return pl.pallas_call(
        functools.partial(_combine_kernel, n_exp=n_exp, n_chunk=n_chunk, cap=cap),
        out_shape=jax.ShapeDtypeStruct((n, d), F32),
        grid_spec=pltpu.PrefetchScalarGridSpec(
            num_scalar_prefetch=2,
            grid=(n_chunk,),
            in_specs=[pl.BlockSpec(memory_space=pl.ANY),
                      pl.BlockSpec((1, n_exp, TOK_TILE), lambda i, o, c: (i, 0, 0)),
                      pl.BlockSpec((TOK_TILE, d), lambda i, o, c: (i, 0)),
                      pl.BlockSpec((1, d), lambda i, o, c: (0, 0)),
                      pl.BlockSpec((1, d), lambda i, o, c: (0, 0))],
            out_specs=pl.BlockSpec((TOK_TILE, d), lambda i, o, c: (i, 0)),
            scratch_shapes=[pltpu.VMEM((2, n_exp * WINDOW_ROWS, d), BF16),
                            pltpu.VMEM((TOK_TILE, d), F32),
                            pltpu.SemaphoreType.DMA((2,))]),
        compiler_params=_cparams("arbitrary"),
        name="combine_experts_final_norm",
    )(off, cnt, y, posm_t, x1, gate, nw)


def kernel(x, c, ctx, c_ctx, w_mod, b_mod, norm_mix, w_in, sink, w_fourier, w_out, norm_ffn,
           w_router, w_gate, w_up, w_down, norm_final):
    assert x.shape[0] == 1 and w_mod.shape[0] == 1
    n, d = x.shape[1], x.shape[2]
    xl, cx = x[0], ctx[0]
    n_exp = w_router.shape[2]
    cap = max(1, CAP_FACTOR * n // n_exp)
    n_chunk = n // LANES
    n_tile = n // TOK_TILE
    per_tile = TOK_TILE // LANES

    cvec = jnp.stack([c[0], c_ctx], axis=1)
    b_mod2 = b_mod[0][None, :]
    mod_head, ab = _modulation_head(cvec, w_mod[0], b_mod2, 2 * d, w_fourier[0])
    mx_head = mod_head[0].reshape(2, 1, d)
    mc = mod_head[1].reshape(2, 1, d)

    w_in_b = w_in[0].astype(BF16)
    nmix = norm_mix[0][None, :]
    q, k, v, z, w_out_b, mod_tail = _project_latent(xl, nmix, mx_head[0], mx_head[1], w_in_b, ab,
                                                    w_out[0], cvec, w_mod[0], b_mod2, 2 * d)
    mx = jnp.concatenate([mx_head, mod_tail[0].reshape(4, 1, d)], axis=0)
    kc, vc = _project_context(cx, nmix, mc[0], mc[1], w_in_b[:, Q_DIM:Q_DIM + 2 * KV_DIM])
    ax1, ax2, fx = _attention_and_dft(q, k, v, kc, vc, sink[0], z)

    x1, g, aff = _output_projection(ax1, ax2, fx, xl, w_out_b, mx[2], norm_ffn[0][None, :],
                                    mx[3], mx[4], w_router[0].astype(BF16))

    aff2 = aff.T.reshape(n_exp * n_chunk, LANES)
    posm, cnt, off = _route(aff2, cap, n_exp)
    by_tile = lambda a: a.reshape(n_exp, n_tile, TOK_TILE).transpose(1, 0, 2)
    posm_t, aff_t = by_tile(posm), by_tile(aff2)
    cnt1 = cnt[:, 0].reshape(n_exp * n_tile, per_tile).sum(axis=1)
    off1 = off[:, 0].reshape(n_exp * n_tile, per_tile)[:, 0]

    xs = _gather(off1, cnt1, g, posm_t, aff_t, cap)
    y = _experts(xs.reshape(n_exp, cap + WINDOW_ROWS, d + GATE_LANES),
                 w_gate[0], w_up[0], w_down[0], cap)
    out = _combine(off1, cnt1, y.reshape(n_exp * cap, d), posm_t, x1, mx[5], norm_final[None, :],
                   cap)
    return out[None]
```
